```python
import math
import jax
import jax.numpy as jnp
from jax import lax
import numpy as np

D_MODEL = 1024
BATCH = 16
SEQ = 2048
DEPTH = 2
DEC_BATCH = 32
DEC_SEQ = 32
PAST_LEN = 4096

CHUNK = 64
Q_BLOCK = 128
N_BRANCH = 4
BRANCH_WIDTH = 256

S5_GROUP = 16
S5_GROUPS = BRANCH_WIDTH // S5_GROUP
S5_STATE = 64

GLA_HEADS = 4
GLA_DK = BRANCH_WIDTH // 2 // GLA_HEADS
GLA_DV = BRANCH_WIDTH // GLA_HEADS
GLA_RANK = 16
GLA_TAU = 16.0

RWKV_HEADS = 4
RWKV_HEAD = BRANCH_WIDTH // RWKV_HEADS
RWKV_DECAY_RANK = 64
RWKV_A_RANK = 64
RWKV_GATE_RANK = 128
RWKV_COLS = 3 * BRANCH_WIDTH + RWKV_DECAY_RANK + RWKV_A_RANK + RWKV_GATE_RANK
RWKV_GN_EPS = 64e-5

SB_HEADS = 4
SB_HEAD = BRANCH_WIDTH // SB_HEADS

GLA_COLS = 2 * GLA_HEADS * GLA_DK + 2 * BRANCH_WIDTH + GLA_RANK
SB_COLS = 3 * BRANCH_WIDTH
GATE_COLS = N_BRANCH * D_MODEL
N_IN = BRANCH_WIDTH + GLA_COLS + RWKV_COLS + SB_COLS + GATE_COLS

D_FF = -(-8 * D_MODEL // (3 * 256)) * 256

F32 = jnp.float32

kernel_name = "hybrid_streaming_encoder_step"


def rmsnorm(x, g, eps=1e-6):
    xf = x.astype(F32)
    y = xf * lax.rsqrt(jnp.mean(xf * xf, axis=-1, keepdims=True) + eps)
    return (y * g.astype(F32)).astype(x.dtype)


def split_cols(z, sizes):
    idx = np.cumsum(sizes)[:-1].tolist()
    return jnp.split(z, idx, axis=-1)


def _lin_combine(left, right):
    a_l, b_l = left
    a_r, b_r = right
    return a_r * a_l, a_r * b_l + b_r


def s5_branch(u, s0, p):
    Bt, T, _ = u.shape
    uf = u.astype(F32).reshape(Bt, T, S5_GROUPS, S5_GROUP)
    lam = lax.complex(p["s5_a_re"].astype(F32), p["s5_a_im"].astype(F32))
    dt = jnp.exp(p["s5_log_dt"].astype(F32))[:, None]
    a_bar = jnp.exp(lam * dt)
    b_mat = lax.complex(p["s5_b_re"].astype(F32), p["s5_b_im"].astype(F32))
    b_bar = ((a_bar - 1.0) / lam)[..., None] * b_mat
    c_mat = lax.complex(p["s5_c_re"].astype(F32), p["s5_c_im"].astype(F32))
    bu = jnp.einsum("gpc,btgc->btgp", b_bar, uf.astype(jnp.complex64))
    x0 = lax.complex(s0[..., 0].astype(F32), s0[..., 1].astype(F32))
    bu = bu.at[:, 0].add(a_bar * x0)
    a_seq = jnp.broadcast_to(a_bar, bu.shape)
    _, xs = lax.associative_scan(_lin_combine, (a_seq, bu), axis=1)
    y = jnp.einsum("gcp,btgp->btgc", c_mat, xs).real + p["s5_d"].astype(F32) * uf
    y = jax.nn.gelu(y.reshape(Bt, T, BRANCH_WIDTH))
    out = y * jax.nn.sigmoid(y @ p["s5_w_glu"].astype(F32) + p["s5_b_glu"].astype(F32))
    x_last = xs[:, -1]
    return out, jnp.stack([x_last.real, x_last.imag], axis=-1)


def gla_branch(cols, s0, p):
    Bt, T, _ = cols.shape
    q, k, v, g, a_lat = split_cols(cols.astype(F32), (GLA_HEADS * GLA_DK, GLA_HEADS * GLA_DK,
                                                       BRANCH_WIDTH, BRANCH_WIDTH, GLA_RANK))
    q = q.reshape(Bt, T, GLA_HEADS, GLA_DK) * (GLA_DK ** -0.5)
    k = k.reshape(Bt, T, GLA_HEADS, GLA_DK)
    v = v.reshape(Bt, T, GLA_HEADS, GLA_DV)
    log_a = jax.nn.log_sigmoid(a_lat @ p["gla_w_alpha"].astype(F32) + p["gla_b_alpha"].astype(F32)) / GLA_TAU
    log_a = log_a.reshape(Bt, T, GLA_HEADS, GLA_DK)
    L = min(CHUNK, T)
    N = T // L

    def to_chunks(t):
        return t.reshape(Bt, N, L, GLA_HEADS, -1).transpose(1, 0, 3, 2, 4)

    causal = jnp.tril(jnp.ones((L, L), dtype=bool))

    def step(S, inp):
        qc, kc, vc, lac = inp
        b = jnp.cumsum(lac, axis=2)
        diff = b[:, :, :, None, :] - b[:, :, None, :, :]
        decay = jnp.exp(jnp.where(causal[:, :, None], diff, -jnp.inf))
        attn = jnp.einsum("bhtc,bhsc,bhtsc->bhts", qc, kc, decay)
        o = jnp.einsum("bhts,bhsv->bhtv", attn, vc) + jnp.einsum("bhtc,bhcv->bhtv", qc * jnp.exp(b), S)
        b_end = b[:, :, -1:, :]
        S = jnp.exp(b_end[:, :, 0, :])[..., None] * S + jnp.einsum("bhsc,bhsv->bhcv", kc * jnp.exp(b_end - b), vc)
        return S, o

    S_fin, o = lax.scan(step, s0.astype(F32), (to_chunks(q), to_chunks(k), to_chunks(v), to_chunks(log_a)))
    o = o.transpose(1, 0, 3, 2, 4).reshape(Bt, T, GLA_HEADS, GLA_DV)
    o = rmsnorm(o, p["gla_norm"].reshape(GLA_HEADS, GLA_DV))
    out = o.reshape(Bt, T, BRANCH_WIDTH) * jax.nn.silu(g)
    return out, S_fin


def rwkv_branch(cols, shift0, s0, p):
    Bt, T, _ = cols.shape
    z = cols.astype(F32)
    prev = jnp.concatenate([shift0[:, None].astype(F32), z[:, :-1]], axis=1)
    zs = z + (prev - z) * p["rwkv_mu"].astype(F32)
    r, k, v, w_lat, a_lat, g_lat = split_cols(zs, (BRANCH_WIDTH, BRANCH_WIDTH, BRANCH_WIDTH,
                                                   RWKV_DECAY_RANK, RWKV_A_RANK, RWKV_GATE_RANK))
    w = -jax.nn.softplus(-(p["rwkv_w0"].astype(F32) + jnp.tanh(w_lat) @ p["rwkv_w2"].astype(F32))) - 0.5
    decay = jnp.exp(-jnp.exp(w))
    a = jax.nn.sigmoid(p["rwkv_a0"].astype(F32) + a_lat @ p["rwkv_a2"].astype(F32))
    g = jax.nn.sigmoid(g_lat) @ p["rwkv_g2"].astype(F32)

    def heads(t):
        return t.reshape(Bt, T, RWKV_HEADS, RWKV_HEAD)

    kk = heads(k * p["rwkv_k_k"].astype(F32))
    kk = kk / jnp.maximum(jnp.linalg.norm(kk, axis=-1, keepdims=True), 1e-12)
    k = k * (1.0 + (a - 1.0) * p["rwkv_k_a"].astype(F32))
    r_h, w_h, k_h, v_h, a_h = heads(r), heads(decay), heads(k), heads(v), heads(a)
    vec_a = -kk
    vec_b = kk * a_h

    def step(S, inp):
        r_t, w_t, k_t, v_t, a_t, b_t = inp
        Sa = jnp.einsum("bhvk,bhk->bhv", S, a_t)
        S = S * w_t[:, :, None, :] + Sa[..., None] * b_t[:, :, None, :] + v_t[..., None] * k_t[:, :, None, :]
        return S, jnp.einsum("bhvk,bhk->bhv", S, r_t)

    seq = tuple(t.transpose(1, 0, 2, 3) for t in (r_h, w_h, k_h, v_h, vec_a, vec_b))
    S_fin, y = lax.scan(step, s0.astype(F32), seq)
    y = y.transpose(1, 0, 2, 3)
    mu = jnp.mean(y, axis=-1, keepdims=True)
    var = jnp.mean(jnp.square(y - mu), axis=-1, keepdims=True)
    y = (y - mu) * lax.rsqrt(var + RWKV_GN_EPS) * p["rwkv_ln_w"].astype(F32).reshape(RWKV_HEADS, RWKV_HEAD) \
        + p["rwkv_ln_b"].astype(F32).reshape(RWKV_HEADS, RWKV_HEAD)
    bonus = jnp.sum(r_h * k_h * p["rwkv_r_k"].astype(F32).reshape(RWKV_HEADS, RWKV_HEAD), axis=-1, keepdims=True)
    y = y + bonus * v_h
    out = y.reshape(Bt, T, BRANCH_WIDTH) * g
    return out, z[:, -1], S_fin


def sb_branch(cols, past_k, past_v, p):
    Bt, T, _ = cols.shape
    q, k, v = split_cols(cols.astype(F32), (BRANCH_WIDTH, BRANCH_WIDTH, BRANCH_WIDTH))
    q = rmsnorm(q.reshape(Bt, T, SB_HEADS, SB_HEAD), p["sb_q_norm"])
    k = rmsnorm(k.reshape(Bt, T, SB_HEADS, SB_HEAD), p["sb_k_norm"])
    v = v.reshape(Bt, T, SB_HEADS, SB_HEAD)
    P = past_k.shape[1]
    k_all = jnp.concatenate([past_k.astype(F32), k], axis=1)
    v_all = jnp.concatenate([past_v.astype(F32), v], axis=1)
    blk = min(Q_BLOCK, T)
    outs = []
    for i in range(T // blk):
        n_keys = P + (i + 1) * blk
        qb = q[:, i * blk:(i + 1) * blk]
        logits = jnp.einsum("bthd,bshd->bhts", qb, k_all[:, :n_keys]) * (SB_HEAD ** -0.5)
        q_pos = P + i * blk + jnp.arange(blk)
        mask = jnp.arange(n_keys)[None, :] < q_pos[:, None]
        log_1m = jnp.where(mask, jax.nn.log_sigmoid(-logits), 0.0)
        between = lax.cumsum(log_1m, axis=3, reverse=True) - log_1m
        weights = jnp.where(mask, jnp.exp(jax.nn.log_sigmoid(logits) + between), 0.0)
        outs.append(jnp.einsum("bhts,bshd->bthd", weights, v_all[:, :n_keys]))
    o = jnp.concatenate(outs, axis=1).reshape(Bt, T, BRANCH_WIDTH)
    return o, k, v


def trunk_layer(x, past_k, past_v, s5_0, gla_0, rwkv_0, shift_0, p):
    Bt, T, _ = x.shape
    h = rmsnorm(x, p["norm_mix"])
    z = h @ p["w_in"]
    u_s5, c_gla, c_rwkv, c_sb, c_gate = split_cols(z, (BRANCH_WIDTH, GLA_COLS, RWKV_COLS, SB_COLS, GATE_COLS))
    o_a, s5_new = s5_branch(u_s5, s5_0, p)
    o_b, gla_new = gla_branch(c_gla, gla_0, p)
    o_c, shift_new, rwkv_new = rwkv_branch(c_rwkv, shift_0, rwkv_0, p)
    o_d, k_new, v_new = sb_branch(c_sb, past_k, past_v, p)
    gates = jax.nn.sigmoid(c_gate.astype(F32)).reshape(Bt, T, N_BRANCH, D_MODEL)
    wb = p["w_branch"]
    merged = (gates[:, :, 0] * (o_a @ wb[0]) + gates[:, :, 1] * (o_b @ wb[1])
              + gates[:, :, 2] * (o_c @ wb[2]) + gates[:, :, 3] * (o_d @ wb[3]))
    x = x + (merged @ p["w_out"]).astype(x.dtype)
    h2 = rmsnorm(x, p["norm_ffn"])
    ffn = (jax.nn.silu(h2 @ p["w_ffn_gate"]) * (h2 @ p["w_ffn_up"])) @ p["w_ffn_down"]
    x = x + ffn.astype(x.dtype)
    return x, (k_new, v_new, s5_new, gla_new, rwkv_new, shift_new)


def run_trunk(x, past_k, past_v, s5_0, gla_0, rwkv_0, shift_0, weights):
    per_layer = []
    for l in range(DEPTH):
        p = {name: arr[l] for name, arr in weights.items()}
        x, st = trunk_layer(x, past_k[l], past_v[l], s5_0[l], gla_0[l], rwkv_0[l], shift_0[l], p)
        per_layer.append(st)
    stacked = [jnp.stack([st[i] for st in per_layer]) for i in range(6)]
    return x, stacked


def setup_inputs(seed: int = 0) -> dict:
    key = jax.random.key(seed)
    ks = iter(jax.random.split(key, 64))

    def nrm(shape, scale):
        return jax.random.normal(next(ks), shape, F32) * scale

    def unif(shape, lo, hi):
        return jax.random.uniform(next(ks), shape, F32, lo, hi)

    L, D, G, P, W = DEPTH, D_MODEL, S5_GROUPS, S5_STATE, BRANCH_WIDTH
    inp = {}
    inp["x_prompt"] = nrm((BATCH, SEQ, D), 1.0)
    inp["x_sample"] = nrm((DEC_BATCH, DEC_SEQ, D), 1.0)
    inp["cache_sb_k"] = nrm((L, DEC_BATCH, PAST_LEN, SB_HEADS, SB_HEAD), 1.0)
    inp["cache_sb_v"] = nrm((L, DEC_BATCH, PAST_LEN, SB_HEADS, SB_HEAD), 1.0)
    inp["state_s5"] = nrm((L, DEC_BATCH, G, P, 2), 0.5)
    inp["state_gla"] = nrm((L, DEC_BATCH, GLA_HEADS, GLA_DK, GLA_DV), 0.3)
    inp["state_rwkv"] = nrm((L, DEC_BATCH, RWKV_HEADS, RWKV_HEAD, RWKV_HEAD), 0.3)
    inp["state_rwkv_shift"] = nrm((L, DEC_BATCH, RWKV_COLS), 1.0)
    inp["norm_mix"] = 1.0 + nrm((L, D), 0.02)
    inp["w_in"] = nrm((L, D, N_IN), D ** -0.5)
    inp["s5_a_re"] = -0.5 + nrm((L, G, P), 0.01)
    inp["s5_a_im"] = math.pi * jnp.arange(P, dtype=F32) + nrm((L, G, P), 0.01)
    inp["s5_log_dt"] = unif((L, G), math.log(1e-3), math.log(1e-1))
    inp["s5_b_re"] = nrm((L, G, P, S5_GROUP), (2 * S5_GROUP) ** -0.5)
    inp["s5_b_im"] = nrm((L, G, P, S5_GROUP), (2 * S5_GROUP) ** -0.5)
    inp["s5_c_re"] = nrm((L, G, S5_GROUP, P), (2 * P) ** -0.5)
    inp["s5_c_im"] = nrm((L, G, S5_GROUP, P), (2 * P) ** -0.5)
    inp["s5_d"] = nrm((L, G, S5_GROUP), 1.0)
    inp["s5_w_glu"] = nrm((L, W, W), W ** -0.5)
    inp["s5_b_glu"] = nrm((L, W), 0.01)
    inp["gla_w_alpha"] = nrm((L, GLA_RANK, GLA_HEADS * GLA_DK), GLA_RANK ** -0.5)
    inp["gla_b_alpha"] = nrm((L, GLA_HEADS * GLA_DK), 0.1)
    inp["gla_norm"] = 1.0 + nrm((L, W), 0.02)
    inp["rwkv_mu"] = unif((L, RWKV_COLS), 0.0, 1.0)
    inp["rwkv_w0"] = unif((L, W), -6.0, 1.0)
    inp["rwkv_w2"] = nrm((L, RWKV_DECAY_RANK, W), 0.5 * RWKV_DECAY_RANK ** -0.5)
    inp["rwkv_a0"] = nrm((L, W), 0.1)
    inp["rwkv_a2"] = nrm((L, RWKV_A_RANK, W), RWKV_A_RANK ** -0.5)
    inp["rwkv_g2"] = nrm((L, RWKV_GATE_RANK, W), RWKV_GATE_RANK ** -0.5)
    inp["rwkv_k_k"] = 0.85 + nrm((L, W), 0.02)
    inp["rwkv_k_a"] = 1.0 + nrm((L, W), 0.02)
    inp["rwkv_r_k"] = nrm((L, W), 0.1)
    inp["rwkv_ln_w"] = 1.0 + nrm((L, W), 0.02)
    inp["rwkv_ln_b"] = nrm((L, W), 0.01)
    inp["sb_q_norm"] = 1.0 + nrm((L, SB_HEAD), 0.02)
    inp["sb_k_norm"] = 1.0 + nrm((L, SB_HEAD), 0.02)
    inp["w_branch"] = nrm((L, N_BRANCH, W, D), W ** -0.5)
    inp["w_out"] = nrm((L, D, D), 0.5 * D ** -0.5)
    inp["norm_ffn"] = 1.0 + nrm((L, D), 0.02)
    inp["w_ffn_gate"] = nrm((L, D, D_FF), D ** -0.5)
    inp["w_ffn_up"] = nrm((L, D, D_FF), D ** -0.5)
    inp["w_ffn_down"] = nrm((L, D_FF, D), D_FF ** -0.5)
    return inp


def reference(x_prompt, x_sample, cache_sb_k, cache_sb_v, state_s5, state_gla, state_rwkv, state_rwkv_shift,
              norm_mix, w_in, s5_a_re, s5_a_im, s5_log_dt, s5_b_re, s5_b_im, s5_c_re, s5_c_im, s5_d,
              s5_w_glu, s5_b_glu, gla_w_alpha, gla_b_alpha, gla_norm, rwkv_mu, rwkv_w0, rwkv_w2, rwkv_a0,
              rwkv_a2, rwkv_g2, rwkv_k_k, rwkv_k_a, rwkv_r_k, rwkv_ln_w, rwkv_ln_b, sb_q_norm, sb_k_norm,
              w_branch, w_out, norm_ffn, w_ffn_gate, w_ffn_up, w_ffn_down):
    weights = dict(norm_mix=norm_mix, w_in=w_in, s5_a_re=s5_a_re, s5_a_im=s5_a_im, s5_log_dt=s5_log_dt,
                   s5_b_re=s5_b_re, s5_b_im=s5_b_im, s5_c_re=s5_c_re, s5_c_im=s5_c_im, s5_d=s5_d,
                   s5_w_glu=s5_w_glu, s5_b_glu=s5_b_glu, gla_w_alpha=gla_w_alpha, gla_b_alpha=gla_b_alpha,
                   gla_norm=gla_norm, rwkv_mu=rwkv_mu, rwkv_w0=rwkv_w0, rwkv_w2=rwkv_w2, rwkv_a0=rwkv_a0,
                   rwkv_a2=rwkv_a2, rwkv_g2=rwkv_g2, rwkv_k_k=rwkv_k_k, rwkv_k_a=rwkv_k_a, rwkv_r_k=rwkv_r_k,
                   rwkv_ln_w=rwkv_ln_w, rwkv_ln_b=rwkv_ln_b, sb_q_norm=sb_q_norm, sb_k_norm=sb_k_norm,
                   w_branch=w_branch, w_out=w_out, norm_ffn=norm_ffn, w_ffn_gate=w_ffn_gate,
                   w_ffn_up=w_ffn_up, w_ffn_down=w_ffn_down)
    Bp = x_prompt.shape[0]
    empty_kv = jnp.zeros((DEPTH, Bp, 0, SB_HEADS, SB_HEAD), F32)
    y_prompt, (p_sb_k, p_sb_v, p_s5, p_gla, p_rwkv, p_shift) = run_trunk(
        x_prompt, empty_kv, empty_kv,
        jnp.zeros((DEPTH, Bp, S5_GROUPS, S5_STATE, 2), F32),
        jnp.zeros((DEPTH, Bp, GLA_HEADS, GLA_DK, GLA_DV), F32),
        jnp.zeros((DEPTH, Bp, RWKV_HEADS, RWKV_HEAD, RWKV_HEAD), F32),
        jnp.zeros((DEPTH, Bp, RWKV_COLS), F32), weights)
    y_sample, (s_sb_k, s_sb_v, s_s5, s_gla, s_rwkv, s_shift) = run_trunk(
        x_sample, cache_sb_k, cache_sb_v, state_s5, state_gla, state_rwkv, state_rwkv_shift, weights)
    return (y_prompt, y_sample, p_sb_k, p_sb_v, p_s5, p_gla, p_rwkv, p_shift,
            s_sb_k, s_sb_v, s_s5, s_gla, s_rwkv, s_shift)
```

```python
import functools
import math

import jax
import jax.numpy as jnp
from jax import lax
from jax.experimental import pallas as pl
from jax.experimental.pallas import tpu as pltpu

F32 = jnp.float32
BF16 = jnp.bfloat16

D_MODEL = 1024
WIDTH = 256
N_BRANCH = 4
S5_GROUPS, S5_GROUP, S5_STATE = 16, 16, 64
S5_FLAT = S5_GROUPS * S5_STATE
GLA_HEADS, GLA_DK, GLA_DV, GLA_RANK, GLA_TAU = 4, 32, 64, 16, 16.0
GLA_QK = GLA_HEADS * GLA_DK
GLA_COLS_PAD = 2 * GLA_QK + 2 * WIDTH + 128
RWKV_HEADS, RWKV_HEAD = 4, 64
RWKV_COLS = 1024
RWKV_GN_EPS = 64e-5
SB_HEADS, SB_HEAD = 4, 64
SB_COLS = 3 * WIDTH
GATE_COLS = N_BRANCH * D_MODEL
D_FF = 2816
RMS_EPS = 1e-6

_C_S5 = 0
_C_GLA = _C_S5 + WIDTH
_C_RWKV = _C_GLA + GLA_COLS_PAD
_C_SB = _C_RWKV + RWKV_COLS
_C_GATE = _C_SB + SB_COLS
_C_END = _C_GATE + GATE_COLS

VMEM_LIMIT = 56 * 1024 * 1024


def _params(*sem):
    return pltpu.CompilerParams(dimension_semantics=sem, vmem_limit_bytes=VMEM_LIMIT)


def _const_spec(shape):
    nd = len(shape)
    return pl.BlockSpec(shape, lambda *_: (0,) * nd, pipeline_mode=pl.Buffered(1))


def _mm(a, b):
    return jnp.dot(a.astype(BF16), b.astype(BF16), preferred_element_type=F32)


def _mm_nt(a, b):
    return lax.dot_general(a.astype(BF16), b.astype(BF16), (((1,), (1,)), ((), ())),
                           preferred_element_type=F32)


def _mm_tn(a, b):
    return lax.dot_general(a.astype(BF16), b.astype(BF16), (((0,), (0,)), ((), ())),
                           preferred_element_type=F32)


def _split(a):
    hi = a.astype(BF16)
    lo = (a - hi.astype(F32)).astype(BF16)
    return hi, lo


def _mm_split_lhs(a, b01):
    hi, lo = _split(a)
    return (jnp.dot(hi, b01, preferred_element_type=F32)
            + jnp.dot(lo, b01, preferred_element_type=F32))


def _mm_split_rhs(a01, b):
    hi, lo = _split(b)
    return (jnp.dot(a01, hi, preferred_element_type=F32)
            + jnp.dot(a01, lo, preferred_element_type=F32))


def _log_sigmoid(z):
    return jnp.minimum(z, 0.0) - jnp.log1p(jnp.exp(-jnp.abs(z)))


def _sigmoid(z):
    return 1.0 / (1.0 + jnp.exp(-z))


def _tri(n, strict=False):
    r = lax.broadcasted_iota(jnp.int32, (n, n), 0)
    c = lax.broadcasted_iota(jnp.int32, (n, n), 1)
    return (r > c) if strict else (r >= c)


def _inproj_kernel(x_ref, g_ref, w_ref, s5_ref, gla_ref, rwkv_ref, sb_ref, gate_ref):
    x = x_ref[...]
    h = x * lax.rsqrt(jnp.mean(x * x, axis=-1, keepdims=True) + RMS_EPS) * g_ref[...]
    hb = h.astype(BF16)

    def mm(lo, hi):
        return jnp.dot(hb, w_ref[:, lo:hi], preferred_element_type=F32)

    s5_ref[...] = mm(_C_S5, _C_GLA)
    gla_ref[...] = mm(_C_GLA, _C_RWKV)
    rwkv_ref[...] = mm(_C_RWKV, _C_SB)
    sb_ref[...] = mm(_C_SB, _C_GATE)
    for n in range(N_BRANCH):
        lo = _C_GATE + n * D_MODEL
        gate_ref[:, n * D_MODEL:(n + 1) * D_MODEL] = _sigmoid(mm(lo, lo + D_MODEL)).astype(BF16)


def _inproj(x2d, g, w, tm):
    n = x2d.shape[0]
    row = lambda width: pl.BlockSpec((tm, width), lambda i: (i, 0))
    return pl.pallas_call(
        _inproj_kernel,
        grid=(n // tm,),
        in_specs=[row(D_MODEL), _const_spec((1, D_MODEL)), _const_spec((D_MODEL, _C_END))],
        out_specs=(row(WIDTH), row(GLA_COLS_PAD), row(RWKV_COLS), row(SB_COLS), row(GATE_COLS)),
        out_shape=(jax.ShapeDtypeStruct((n, WIDTH), F32),
                   jax.ShapeDtypeStruct((n, GLA_COLS_PAD), F32),
                   jax.ShapeDtypeStruct((n, RWKV_COLS), F32),
                   jax.ShapeDtypeStruct((n, SB_COLS), F32),
                   jax.ShapeDtypeStruct((n, GATE_COLS), BF16)),
        compiler_params=_params("parallel"),
        name="inproj",
    )(x2d, g, w)


def _s5_kernel(u_ref, x0_ref, bbig_ref, cbig_ref, ar_ref, ai_ref, d_ref, wglu_ref, bglu_ref,
               o_ref, xfin_ref, bu_scr, x_scr, *, nb, lc):
    @pl.when(pl.program_id(0) == 0)
    def _():
        x_scr[...] = x0_ref[...]

    u = u_ref[...]
    bu_scr[...] = _mm(u, bbig_ref[...])
    ar = jnp.broadcast_to(ar_ref[...], (nb, S5_FLAT))
    ai = jnp.broadcast_to(ai_ref[...], (nb, S5_FLAT))

    def step(t, carry):
        xr, xi = carry
        rows = pl.ds(pl.multiple_of(t * nb, nb), nb)
        nxr = ar * xr - ai * xi + bu_scr[rows, 0:S5_FLAT]
        nxi = ar * xi + ai * xr + bu_scr[rows, S5_FLAT:2 * S5_FLAT]
        bu_scr[rows, 0:S5_FLAT] = nxr
        bu_scr[rows, S5_FLAT:2 * S5_FLAT] = nxi
        return nxr, nxi

    xr, xi = lax.fori_loop(0, lc, step, (x_scr[:, 0:S5_FLAT], x_scr[:, S5_FLAT:2 * S5_FLAT]))
    x_scr[:, 0:S5_FLAT] = xr
    x_scr[:, S5_FLAT:2 * S5_FLAT] = xi
    xfin_ref[...] = x_scr[...]

    y = _mm(bu_scr[...], cbig_ref[...]) + d_ref[...] * u
    y = jax.nn.gelu(y)
    o_ref[...] = y * _sigmoid(_mm(y, wglu_ref[...]) + bglu_ref[...])


def _s5(u_tm, x0, p, nb, lc):
    rows = u_tm.shape[0]
    blk = lc * nb
    kern = functools.partial(_s5_kernel, nb=nb, lc=lc)
    return pl.pallas_call(
        kern,
        grid=(rows // blk,),
        in_specs=[pl.BlockSpec((blk, WIDTH), lambda c: (c, 0)),
                  _const_spec((nb, 2 * S5_FLAT)),
                  _const_spec((WIDTH, 2 * S5_FLAT)), _const_spec((2 * S5_FLAT, WIDTH)),
                  _const_spec((1, S5_FLAT)), _const_spec((1, S5_FLAT)), _const_spec((1, WIDTH)),
                  _const_spec((WIDTH, WIDTH)), _const_spec((1, WIDTH))],
        out_specs=(pl.BlockSpec((blk, WIDTH), lambda c: (c, 0)),
                   pl.BlockSpec((nb, 2 * S5_FLAT), lambda c: (0, 0))),
        out_shape=(jax.ShapeDtypeStruct((rows, WIDTH), F32),
                   jax.ShapeDtypeStruct((nb, 2 * S5_FLAT), F32)),
        scratch_shapes=[pltpu.VMEM((blk, 2 * S5_FLAT), F32), pltpu.VMEM((nb, 2 * S5_FLAT), F32)],
        compiler_params=_params("arbitrary"),
        name="s5",
    )(u_tm, x0, p["s5_bbig"], p["s5_cbig"], p["s5_ar"], p["s5_ai"], p["s5_d"], p["s5_w_glu"],
      p["s5_b_glu"])


def _gla_kernel(z_ref, s0_ref, walpha_ref, balpha_ref, gnorm_ref, o_ref, sfin_ref, s_scr, *, L):
    @pl.when(pl.program_id(1) == 0)
    def _():
        s_scr[...] = s0_ref[0]

    z = z_ref[0]
    q = z[:, 0:GLA_QK] * (GLA_DK ** -0.5)
    k = z[:, GLA_QK:2 * GLA_QK]
    v = z[:, 2 * GLA_QK:2 * GLA_QK + WIDTH]
    g = z[:, 2 * GLA_QK + WIDTH:2 * GLA_QK + 2 * WIDTH]
    alat = z[:, 2 * GLA_QK + 2 * WIDTH:]
    log_a = _log_sigmoid(_mm(alat, walpha_ref[...]) + balpha_ref[...]) * (1.0 / GLA_TAU)
    tri = _tri(L)
    b = _mm_split_rhs(tri.astype(BF16), log_a)
    b_end = b[L - 1:L, :]
    qt = q * jnp.exp(b)
    kt = k * jnp.exp(-b)
    kd = k * jnp.exp(b_end - b)
    e_end = jnp.exp(b_end)
    outs = []
    for h in range(GLA_HEADS):
        ks = slice(GLA_DK * h, GLA_DK * (h + 1))
        vs = slice(GLA_DV * h, GLA_DV * (h + 1))
        st = s_scr[h]
        attn = jnp.where(tri, _mm_nt(qt[:, ks], kt[:, ks]), 0.0)
        oh = _mm(attn, v[:, vs]) + _mm_nt(qt[:, ks], st)
        s_scr[h] = st * e_end[:, ks] + _mm_tn(v[:, vs], kd[:, ks])
        oh = oh * lax.rsqrt(jnp.mean(oh * oh, axis=-1, keepdims=True) + RMS_EPS) * gnorm_ref[:, vs]
        outs.append(oh)
    o_ref[0] = jnp.concatenate(outs, axis=-1) * (g * _sigmoid(g))
    sfin_ref[0] = s_scr[...]


def _gla(cols, s0t, p, L):
    bt, t, _ = cols.shape
    kern = functools.partial(_gla_kernel, L=L)
    state = pl.BlockSpec((1, GLA_HEADS, GLA_DV, GLA_DK), lambda b, n: (b, 0, 0, 0))
    return pl.pallas_call(
        kern,
        grid=(bt, t // L),
        in_specs=[pl.BlockSpec((1, L, GLA_COLS_PAD), lambda b, n: (b, n, 0)), state,
                  _const_spec((128, GLA_QK)), _const_spec((1, GLA_QK)), _const_spec((1, WIDTH))],
        out_specs=(pl.BlockSpec((1, L, WIDTH), lambda b, n: (b, n, 0)), state),
        out_shape=(jax.ShapeDtypeStruct((bt, t, WIDTH), F32),
                   jax.ShapeDtypeStruct((bt, GLA_HEADS, GLA_DV, GLA_DK), F32)),
        scratch_shapes=[pltpu.VMEM((GLA_HEADS, GLA_DV, GLA_DK), F32)],
        compiler_params=_params("parallel", "arbitrary"),
        name="gla",
    )(cols, s0t, p["gla_w_alpha"], p["gla_b_alpha"], p["gla_norm"])


def _rwkv_kernel(z_ref, shift0_ref, s0_ref, mu_ref, w0_ref, w2_ref, a0_ref, a2_ref, g2_ref, kk_ref,
                 ka_ref, rk_ref, lnw_ref, lnb_ref, bd_ref, o_ref, shiftfin_ref, sfin_ref,
                 s_scr, prev_scr, *, L):
    @pl.when(pl.program_id(1) == 0)
    def _():
        s_scr[...] = s0_ref[0]
        prev_scr[...] = shift0_ref[0]

    z = z_ref[0]
    first = lax.broadcasted_iota(jnp.int32, (L, RWKV_COLS), 0) == 0
    prev = jnp.where(first, prev_scr[...], pltpu.roll(z, 1, axis=0))
    last = z[L - 1:L, :]
    prev_scr[...] = last
    shiftfin_ref[0] = last
    zs = z + (prev - z) * mu_ref[...]
    r = zs[:, 0:256]
    k = zs[:, 256:512]
    v = zs[:, 512:768]
    w_lat = zs[:, 768:832]
    a_lat = zs[:, 832:896]
    g_lat = zs[:, 896:1024]
    w = -jax.nn.softplus(-(w0_ref[...] + _mm(jnp.tanh(w_lat), w2_ref[...]))) - 0.5
    log_w = -jnp.exp(w)
    a = _sigmoid(a0_ref[...] + _mm(a_lat, a2_ref[...]))
    g = _mm(_sigmoid(g_lat), g2_ref[...])
    kk = k * kk_ref[...]
    kk = kk / jnp.maximum(jnp.sqrt(_mm_split_lhs(kk * kk, bd_ref[...])), 1e-12)
    k = k * (1.0 + (a - 1.0) * ka_ref[...])
    vec_a = -kk
    vec_b = kk * a

    tri = _tri(L)
    stri = _tri(L, strict=True)
    cum = _mm_split_rhs(tri.astype(BF16), log_w)
    p_t = jnp.exp(cum)
    inv_p = jnp.exp(-cum)
    p_end = p_t[L - 1:L, :]
    a_til = vec_a * jnp.exp(cum - log_w)
    b_til = vec_b * inv_p
    k_til = k * inv_p
    r_til = r * p_t
    b_end = b_til * p_end
    k_end = k_til * p_end
    eye = (lax.broadcasted_iota(jnp.int32, (L, L), 0)
           == lax.broadcasted_iota(jnp.int32, (L, L), 1)).astype(F32)

    outs = []
    for h in range(RWKV_HEADS):
        hs = slice(RWKV_HEAD * h, RWKV_HEAD * (h + 1))
        s0 = s_scr[h]
        ah, bh, kh, rh, vh = a_til[:, hs], b_til[:, hs], k_til[:, hs], r_til[:, hs], v[:, hs]
        a_ab = jnp.where(stri, _mm_nt(ah, bh), 0.0)
        a_ak = jnp.where(stri, _mm_nt(ah, kh), 0.0)
        a_rb = jnp.where(tri, _mm_nt(rh, bh), 0.0)
        a_rk = jnp.where(tri, _mm_nt(rh, kh), 0.0)
        tinv = eye + a_ab
        pw = a_ab
        for _ in range(int(math.log2(L)) - 1):
            pw = _mm(pw, pw)
            tinv = tinv + _mm(tinv, pw)
        sa = _mm(tinv, _mm_nt(ah, s0) + _mm(a_ak, vh))
        y = _mm_nt(rh, s0) + _mm(a_rb, sa) + _mm(a_rk, vh)
        s_scr[h] = s0 * p_end[:, hs] + _mm_tn(sa, b_end[:, hs]) + _mm_tn(vh, k_end[:, hs])
        mean = jnp.mean(y, axis=-1, keepdims=True)
        var = jnp.mean(jnp.square(y - mean), axis=-1, keepdims=True)
        y = (y - mean) * lax.rsqrt(var + RWKV_GN_EPS) * lnw_ref[:, hs] + lnb_ref[:, hs]
        bonus = jnp.sum(r[:, hs] * k[:, hs] * rk_ref[:, hs], axis=-1, keepdims=True)
        outs.append(y + bonus * vh)
    o_ref[0] = jnp.concatenate(outs, axis=-1) * g
    sfin_ref[0] = s_scr[...]


def _rwkv(cols, shift0, s0, p, L):
    bt, t, _ = cols.shape
    kern = functools.partial(_rwkv_kernel, L=L)
    state = pl.BlockSpec((1, RWKV_HEADS, RWKV_HEAD, RWKV_HEAD), lambda b, n: (b, 0, 0, 0))
    shift = pl.BlockSpec((1, 1, RWKV_COLS), lambda b, n: (b, 0, 0))
    vec = _const_spec((1, WIDTH))
    return pl.pallas_call(
        kern,
        grid=(bt, t // L),
        in_specs=[pl.BlockSpec((1, L, RWKV_COLS), lambda b, n: (b, n, 0)), shift, state,
                  _const_spec((1, RWKV_COLS)), vec, _const_spec((64, WIDTH)), vec,
                  _const_spec((64, WIDTH)), _const_spec((128, WIDTH)), vec, vec, vec, vec, vec,
                  _const_spec((WIDTH, WIDTH))],
        out_specs=(pl.BlockSpec((1, L, WIDTH), lambda b, n: (b, n, 0)), shift, state),
        out_shape=(jax.ShapeDtypeStruct((bt, t, WIDTH), F32),
                   jax.ShapeDtypeStruct((bt, 1, RWKV_COLS), F32),
                   jax.ShapeDtypeStruct((bt, RWKV_HEADS, RWKV_HEAD, RWKV_HEAD), F32)),
        scratch_shapes=[pltpu.VMEM((RWKV_HEADS, RWKV_HEAD, RWKV_HEAD), F32),
                        pltpu.VMEM((1, RWKV_COLS), F32)],
        compiler_params=_params("parallel", "arbitrary"),
        name="rwkv",
    )(cols, shift0, s0, p["rwkv_mu"], p["rwkv_w0"], p["rwkv_w2"], p["rwkv_a0"], p["rwkv_a2"],
      p["rwkv_g2"], p["rwkv_k_k"], p["rwkv_k_a"], p["rwkv_r_k"], p["rwkv_ln_w"], p["rwkv_ln_b"],
      p["head_ones"])


def _sb_prep_kernel(z_ref, qn_ref, kn_ref, bd_ref, q_ref, k_ref, v_ref):
    z = z_ref[...]
    q = z[:, 0:WIDTH]
    k = z[:, WIDTH:2 * WIDTH]

    def headnorm(x, g):
        ms = _mm_split_lhs(x * x, bd_ref[...]) * (1.0 / SB_HEAD)
        return x * lax.rsqrt(ms + RMS_EPS) * g

    q_ref[...] = headnorm(q, qn_ref[...])
    k_ref[...] = headnorm(k, kn_ref[...])
    v_ref[...] = z[:, 2 * WIDTH:]


def _sb_prep(cols2d, p, tm):
    n = cols2d.shape[0]
    row = lambda width: pl.BlockSpec((tm, width), lambda i: (i, 0))
    out = jax.ShapeDtypeStruct((n, WIDTH), F32)
    return pl.pallas_call(
        _sb_prep_kernel,
        grid=(n // tm,),
        in_specs=[row(SB_COLS), _const_spec((1, WIDTH)), _const_spec((1, WIDTH)),
                  _const_spec((WIDTH, WIDTH))],
        out_specs=(row(WIDTH), row(WIDTH), row(WIDTH)),
        out_shape=(out, out, out),
        compiler_params=_params("parallel"),
        name="sb_prep",
    )(cols2d, p["sb_q_norm"], p["sb_k_norm"], p["head_ones"])


def _sb_block(q, k, v, mask, upper, acc_scr, carry_scr):
    for h in range(SB_HEADS):
        hs = slice(SB_HEAD * h, SB_HEAD * (h + 1))
        z = _mm_nt(q[:, hs], k[:, hs])
        lp = _log_sigmoid(z)
        lm = lp - z
        if mask is not None:
            lm = jnp.where(mask, lm, 0.0)
        between = carry_scr[h] + _mm_split_lhs(lm, upper)
        wgt = jnp.exp(lp + between)
        if mask is not None:
            wgt = jnp.where(mask, wgt, 0.0)
        acc_scr[h] += _mm(wgt, v[:, hs])
        carry_scr[h] += jnp.sum(lm, axis=-1, keepdims=True)


def _later_key_ones(tk):
    return _tri(tk, strict=True).astype(BF16)


def _sb_self_kernel(q_ref, k_ref, v_ref, o_ref, acc_scr, carry_scr, *, tq):
    i = pl.program_id(1)
    j = pl.program_id(2)

    @pl.when(j == 0)
    def _():
        acc_scr[...] = jnp.zeros_like(acc_scr)
        carry_scr[...] = jnp.zeros_like(carry_scr)

    upper = _later_key_ones(tq)

    @pl.when(j == 0)
    def _():
        _sb_block(q_ref[0], k_ref[0], v_ref[0], _tri(tq, strict=True), upper, acc_scr, carry_scr)

    @pl.when(jnp.logical_and(j > 0, j <= i))
    def _():
        _sb_block(q_ref[0], k_ref[0], v_ref[0], None, upper, acc_scr, carry_scr)

    @pl.when(j == pl.num_programs(2) - 1)
    def _():
        o_ref[0] = jnp.concatenate([acc_scr[h] for h in range(SB_HEADS)], axis=-1)


def _sb_self(q, k, v, tq):
    bt, t, _ = q.shape
    nq = t // tq
    kern = functools.partial(_sb_self_kernel, tq=tq)
    kv = pl.BlockSpec((1, tq, WIDTH), lambda b, i, j: (b, jnp.maximum(i - j, 0), 0))
    return pl.pallas_call(
        kern,
        grid=(bt, nq, nq),
        in_specs=[pl.BlockSpec((1, tq, WIDTH), lambda b, i, j: (b, i, 0)), kv, kv],
        out_specs=pl.BlockSpec((1, tq, WIDTH), lambda b, i, j: (b, i, 0)),
        out_shape=jax.ShapeDtypeStruct((bt, t, WIDTH), F32),
        scratch_shapes=[pltpu.VMEM((SB_HEADS, tq, SB_HEAD), F32),
                        pltpu.VMEM((SB_HEADS, tq, 1), F32)],
        compiler_params=_params("parallel", "parallel", "arbitrary"),
        name="sb_self",
    )(q, k, v)


def _sb_past_kernel(q_ref, k_ref, v_ref, pk_ref, pv_ref, o_ref, acc_scr, carry_scr, *, t, tkp):
    j = pl.program_id(1)

    @pl.when(j == 0)
    def _():
        acc_scr[...] = jnp.zeros_like(acc_scr)
        carry_scr[...] = jnp.zeros_like(carry_scr)
        _sb_block(q_ref[0], k_ref[0], v_ref[0], _tri(t, strict=True), _later_key_ones(t),
                  acc_scr, carry_scr)

    @pl.when(j > 0)
    def _():
        _sb_block(q_ref[0], pk_ref[0], pv_ref[0], None, _later_key_ones(tkp), acc_scr, carry_scr)

    @pl.when(j == pl.num_programs(1) - 1)
    def _():
        o_ref[0] = jnp.concatenate([acc_scr[h] for h in range(SB_HEADS)], axis=-1)


def _sb_past(q, k, v, past_k, past_v, tkp):
    bt, t, _ = q.shape
    np_ = past_k.shape[1] // tkp
    kern = functools.partial(_sb_past_kernel, t=t, tkp=tkp)
    new = pl.BlockSpec((1, t, WIDTH), lambda b, j: (b, 0, 0))
    past = pl.BlockSpec((1, tkp, WIDTH), lambda b, j: (b, np_ - jnp.maximum(j, 1), 0))
    return pl.pallas_call(
        kern,
        grid=(bt, np_ + 1),
        in_specs=[new, new, new, past, past],
        out_specs=new,
        out_shape=jax.ShapeDtypeStruct((bt, t, WIDTH), F32),
        scratch_shapes=[pltpu.VMEM((SB_HEADS, t, SB_HEAD), F32),
                        pltpu.VMEM((SB_HEADS, t, 1), F32)],
        compiler_params=_params("parallel", "arbitrary"),
        name="sb_past",
    )(q, k, v, past_k, past_v)


def _merge_ffn_kernel(x_ref, oa_ref, ob_ref, oc_ref, od_ref, gate_ref, wb_ref, wout_ref, nffn_ref,
                      wg_ref, wu_ref, wd_ref, y_ref):
    merged = None
    for n, o_ref in enumerate((oa_ref, ob_ref, oc_ref, od_ref)):
        term = gate_ref[:, n * D_MODEL:(n + 1) * D_MODEL].astype(F32) * _mm(o_ref[...], wb_ref[n])
        merged = term if merged is None else merged + term
    x = x_ref[...] + _mm(merged, wout_ref[...])
    h = x * lax.rsqrt(jnp.mean(x * x, axis=-1, keepdims=True) + RMS_EPS) * nffn_ref[...]
    hb = h.astype(BF16)
    gate = jnp.dot(hb, wg_ref[...], preferred_element_type=F32)
    up = jnp.dot(hb, wu_ref[...], preferred_element_type=F32)
    y_ref[...] = x + _mm(gate * _sigmoid(gate) * up, wd_ref[...])


def _merge_ffn(x2d, oa, ob, oc, od, gates, p, tm):
    n = x2d.shape[0]
    row = lambda width: pl.BlockSpec((tm, width), lambda i: (i, 0))
    return pl.pallas_call(
        _merge_ffn_kernel,
        grid=(n // tm,),
        in_specs=[row(D_MODEL), row(WIDTH), row(WIDTH), row(WIDTH), row(WIDTH), row(GATE_COLS),
                  _const_spec((N_BRANCH, WIDTH, D_MODEL)), _const_spec((D_MODEL, D_MODEL)),
                  _const_spec((1, D_MODEL)), _const_spec((D_MODEL, D_FF)),
                  _const_spec((D_MODEL, D_FF)), _const_spec((D_FF, D_MODEL))],
        out_specs=row(D_MODEL),
        out_shape=jax.ShapeDtypeStruct((n, D_MODEL), F32),
        compiler_params=_params("parallel"),
        name="merge_ffn",
    )(x2d, oa, ob, oc, od, gates, p["w_branch"], p["w_out"], p["norm_ffn"], p["w_ffn_gate"],
      p["w_ffn_up"], p["w_ffn_down"])


def _prep_layer(w):
    p = {}
    w_in = w["w_in"]
    o_gla = WIDTH
    o_alat = o_gla + 2 * GLA_QK + 2 * WIDTH
    o_rwkv = o_alat + GLA_RANK
    o_sb = o_rwkv + RWKV_COLS
    o_gate = o_sb + SB_COLS
    pad = jnp.zeros((D_MODEL, 128 - GLA_RANK), w_in.dtype)
    p["w_in"] = jnp.concatenate(
        [w_in[:, :o_alat], w_in[:, o_alat:o_rwkv], pad, w_in[:, o_rwkv:]], axis=1).astype(BF16)
    p["norm_mix"] = w["norm_mix"].reshape(1, D_MODEL)

    lam = lax.complex(w["s5_a_re"], w["s5_a_im"])
    dt = jnp.exp(w["s5_log_dt"])[:, None]
    a_bar = jnp.exp(lam * dt)
    b_bar = ((a_bar - 1.0) / lam)[..., None] * lax.complex(w["s5_b_re"], w["s5_b_im"])
    eye = jnp.eye(S5_GROUPS, dtype=F32)

    def in_map(m):
        return jnp.einsum("gpc,gh->gchp", m, eye).reshape(WIDTH, S5_FLAT)

    def out_map(m):
        return jnp.einsum("gcp,gh->gphc", m, eye).reshape(S5_FLAT, WIDTH)

    p["s5_bbig"] = jnp.concatenate([in_map(b_bar.real), in_map(b_bar.imag)], axis=1).astype(BF16)
    p["s5_cbig"] = jnp.concatenate([out_map(w["s5_c_re"]), out_map(-w["s5_c_im"])], axis=0).astype(BF16)
    p["s5_ar"] = a_bar.real.reshape(1, S5_FLAT)
    p["s5_ai"] = a_bar.imag.reshape(1, S5_FLAT)
    p["s5_d"] = w["s5_d"].reshape(1, WIDTH)
    p["s5_w_glu"] = w["s5_w_glu"].astype(BF16)
    p["s5_b_glu"] = w["s5_b_glu"].reshape(1, WIDTH)

    p["gla_w_alpha"] = jnp.concatenate(
        [w["gla_w_alpha"], jnp.zeros((128 - GLA_RANK, GLA_QK), F32)], axis=0).astype(BF16)
    p["gla_b_alpha"] = w["gla_b_alpha"].reshape(1, GLA_QK)
    p["gla_norm"] = w["gla_norm"].reshape(1, WIDTH)

    for name in ("rwkv_w0", "rwkv_a0", "rwkv_k_k", "rwkv_k_a", "rwkv_r_k", "rwkv_ln_w", "rwkv_ln_b"):
        p[name] = w[name].reshape(1, WIDTH)
    p["rwkv_mu"] = w["rwkv_mu"].reshape(1, RWKV_COLS)
    for name in ("rwkv_w2", "rwkv_a2", "rwkv_g2"):
        p[name] = w[name].astype(BF16)
    head = jnp.arange(WIDTH) // RWKV_HEAD
    p["head_ones"] = (head[:, None] == head[None, :]).astype(BF16)

    p["sb_q_norm"] = jnp.tile(w["sb_q_norm"] * (SB_HEAD ** -0.5), SB_HEADS).reshape(1, WIDTH)
    p["sb_k_norm"] = jnp.tile(w["sb_k_norm"], SB_HEADS).reshape(1, WIDTH)

    p["w_branch"] = w["w_branch"].astype(BF16)
    p["w_out"] = w["w_out"].astype(BF16)
    p["norm_ffn"] = w["norm_ffn"].reshape(1, D_MODEL)
    for name in ("w_ffn_gate", "w_ffn_up", "w_ffn_down"):
        p[name] = w[name].astype(BF16)
    return p


def _pick(n, prefs):
    for c in prefs:
        if n % c == 0:
            return c
    return n


def _layer(x, past_k, past_v, s5_0, gla_0, rwkv_0, shift_0, p):
    bt, t, _ = x.shape
    n = bt * t
    x2d = x.reshape(n, D_MODEL)
    tm = _pick(n, (256, 128, 64, 32, 16, 8))
    u, c_gla, c_rwkv, c_sb, gates = _inproj(x2d, p["norm_mix"], p["w_in"], tm)

    u_tm = u.reshape(bt, t, WIDTH).transpose(1, 0, 2).reshape(n, WIDTH)
    x0 = jnp.concatenate([s5_0[..., 0].reshape(bt, S5_FLAT), s5_0[..., 1].reshape(bt, S5_FLAT)], axis=1)
    lc = _pick(t, (64, 32, 16, 8))
    o_a_tm, xfin = _s5(u_tm, x0, p, bt, lc)
    o_a = o_a_tm.reshape(t, bt, WIDTH).transpose(1, 0, 2).reshape(n, WIDTH)
    s5_new = jnp.stack([xfin[:, :S5_FLAT].reshape(bt, S5_GROUPS, S5_STATE),
                        xfin[:, S5_FLAT:].reshape(bt, S5_GROUPS, S5_STATE)], axis=-1)

    chunk = _pick(t, (64, 32, 16, 8))
    o_b, gla_t = _gla(c_gla.reshape(bt, t, GLA_COLS_PAD), jnp.swapaxes(gla_0, 2, 3), p, chunk)
    gla_new = jnp.swapaxes(gla_t, 2, 3)

    o_c, shift_new, rwkv_new = _rwkv(c_rwkv.reshape(bt, t, RWKV_COLS),
                                     shift_0.reshape(bt, 1, RWKV_COLS), rwkv_0, p, chunk)

    q, k, v = _sb_prep(c_sb, p, tm)
    q3, k3, v3 = (a.reshape(bt, t, WIDTH) for a in (q, k, v))
    if past_k is None:
        o_d = _sb_self(q3, k3, v3, _pick(t, (128, 64, 32, 16, 8)))
    else:
        pl_ = past_k.shape[1]
        o_d = _sb_past(q3, k3, v3, past_k.reshape(bt, pl_, WIDTH), past_v.reshape(bt, pl_, WIDTH),
                       _pick(pl_, (256, 128, 64, 32, 16, 8)))

    y = _merge_ffn(x2d, o_a, o_b.reshape(n, WIDTH), o_c.reshape(n, WIDTH), o_d.reshape(n, WIDTH),
                   gates, p, tm)
    states = (k3.reshape(bt, t, SB_HEADS, SB_HEAD), v3.reshape(bt, t, SB_HEADS, SB_HEAD),
              s5_new, gla_new, rwkv_new, shift_new.reshape(bt, RWKV_COLS))
    return y.reshape(bt, t, D_MODEL), states


def _trunk(x, past_k, past_v, s5_0, gla_0, rwkv_0, shift_0, layers):
    per_layer = []
    for l, p in enumerate(layers):
        pk = None if past_k is None else past_k[l]
        pv = None if past_v is None else past_v[l]
        x, st = _layer(x, pk, pv, s5_0[l], gla_0[l], rwkv_0[l], shift_0[l], p)
        per_layer.append(st)
    return x, [jnp.stack([st[i] for st in per_layer]) for i in range(6)]


def kernel(x_prompt, x_sample, cache_sb_k, cache_sb_v, state_s5, state_gla, state_rwkv, state_rwkv_shift, norm_mix, w_in, s5_a_re, s5_a_im, s5_log_dt, s5_b_re, s5_b_im, s5_c_re, s5_c_im, s5_d, s5_w_glu, s5_b_glu, gla_w_alpha, gla_b_alpha, gla_norm, rwkv_mu, rwkv_w0, rwkv_w2, rwkv_a0, rwkv_a2, rwkv_g2, rwkv_k_k, rwkv_k_a, rwkv_r_k, rwkv_ln_w, rwkv_ln_b, sb_q_norm, sb_k_norm, w_branch, w_out, norm_ffn, w_ffn_gate, w_ffn_up, w_ffn_down):
    weights = dict(norm_mix=norm_mix, w_in=w_in, s5_a_re=s5_a_re, s5_a_im=s5_a_im, s5_log_dt=s5_log_dt,
                   s5_b_re=s5_b_re, s5_b_im=s5_b_im, s5_c_re=s5_c_re, s5_c_im=s5_c_im, s5_d=s5_d,
                   s5_w_glu=s5_w_glu, s5_b_glu=s5_b_glu, gla_w_alpha=gla_w_alpha, gla_b_alpha=gla_b_alpha,
                   gla_norm=gla_norm, rwkv_mu=rwkv_mu, rwkv_w0=rwkv_w0, rwkv_w2=rwkv_w2, rwkv_a0=rwkv_a0,
                   rwkv_a2=rwkv_a2, rwkv_g2=rwkv_g2, rwkv_k_k=rwkv_k_k, rwkv_k_a=rwkv_k_a, rwkv_r_k=rwkv_r_k,
                   rwkv_ln_w=rwkv_ln_w, rwkv_ln_b=rwkv_ln_b, sb_q_norm=sb_q_norm, sb_k_norm=sb_k_norm,
                   w_branch=w_branch, w_out=w_out, norm_ffn=norm_ffn, w_ffn_gate=w_ffn_gate,
                   w_ffn_up=w_ffn_up, w_ffn_down=w_ffn_down)
    depth = w_in.shape[0]
    layers = [_prep_layer({name: arr[l] for name, arr in weights.items()}) for l in range(depth)]

    bp = x_prompt.shape[0]
    y_prompt, p_states = _trunk(
        x_prompt, None, None,
        jnp.zeros((depth, bp, S5_GROUPS, S5_STATE, 2), F32),
        jnp.zeros((depth, bp, GLA_HEADS, GLA_DK, GLA_DV), F32),
        jnp.zeros((depth, bp, RWKV_HEADS, RWKV_HEAD, RWKV_HEAD), F32),
        jnp.zeros((depth, bp, RWKV_COLS), F32), layers)
    y_sample, s_states = _trunk(x_sample, cache_sb_k, cache_sb_v, state_s5, state_gla, state_rwkv,
                                state_rwkv_shift, layers)
    return (y_prompt, y_sample, *p_states, *s_states)
```

```python
import functools
import math

import jax
import jax.numpy as jnp
from jax import lax
from jax.experimental import pallas as pl
from jax.experimental.pallas import tpu as pltpu

F32 = jnp.float32
BF16 = jnp.bfloat16

D_MODEL = 1024
WIDTH = 256
N_BRANCH = 4
S5_GROUPS, S5_GROUP, S5_STATE = 16, 16, 64
S5_FLAT = S5_GROUPS * S5_STATE
GLA_HEADS, GLA_DK, GLA_DV, GLA_RANK, GLA_TAU = 4, 32, 64, 16, 16.0
GLA_QK = GLA_HEADS * GLA_DK
GLA_COLS_PAD = 2 * GLA_QK + 2 * WIDTH + 128
RWKV_HEADS, RWKV_HEAD = 4, 64
RWKV_COLS = 1024
RWKV_GN_EPS = 64e-5
RWKV_ROWS = 256
SB_HEADS, SB_HEAD = 4, 64
SB_COLS = 3 * WIDTH
GATE_COLS = N_BRANCH * D_MODEL
D_FF = 2816
RMS_EPS = 1e-6

_C_S5 = 0
_C_GLA = _C_S5 + WIDTH
_C_RWKV = _C_GLA + GLA_COLS_PAD
_C_SB = _C_RWKV + RWKV_COLS
_C_GATE = _C_SB + SB_COLS
_C_END = _C_GATE + GATE_COLS

VMEM_LIMIT = 56 * 1024 * 1024


def _params(*sem):
    return pltpu.CompilerParams(dimension_semantics=sem, vmem_limit_bytes=VMEM_LIMIT)


def _const_spec(shape):
    nd = len(shape)
    return pl.BlockSpec(shape, lambda *_: (0,) * nd, pipeline_mode=pl.Buffered(1))


def _mm(a, b):
    return jnp.dot(a.astype(BF16), b.astype(BF16), preferred_element_type=F32)


def _mm_nt(a, b):
    return lax.dot_general(a.astype(BF16), b.astype(BF16), (((1,), (1,)), ((), ())),
                           preferred_element_type=F32)


def _mm_tn(a, b):
    return lax.dot_general(a.astype(BF16), b.astype(BF16), (((0,), (0,)), ((), ())),
                           preferred_element_type=F32)


def _split(a):
    hi = a.astype(BF16)
    lo = (a - hi.astype(F32)).astype(BF16)
    return hi, lo


def _mm_split_lhs(a, b01):
    hi, lo = _split(a)
    return (jnp.dot(hi, b01, preferred_element_type=F32)
            + jnp.dot(lo, b01, preferred_element_type=F32))


def _mm_split_rhs(a01, b):
    hi, lo = _split(b)
    return (jnp.dot(a01, hi, preferred_element_type=F32)
            + jnp.dot(a01, lo, preferred_element_type=F32))


def _log_sigmoid(z):
    return jnp.minimum(z, 0.0) - jnp.log1p(jnp.exp(-jnp.abs(z)))


def _sigmoid(z):
    return 1.0 / (1.0 + jnp.exp(-z))


def _tri(n, strict=False):
    r = lax.broadcasted_iota(jnp.int32, (n, n), 0)
    c = lax.broadcasted_iota(jnp.int32, (n, n), 1)
    return (r > c) if strict else (r >= c)


def _inproj_kernel(x_ref, g_ref, w_ref, s5_ref, gla_ref, rwkv_ref, sb_ref, gate_ref):
    x = x_ref[...]
    h = x * lax.rsqrt(jnp.mean(x * x, axis=-1, keepdims=True) + RMS_EPS) * g_ref[...]
    hb = h.astype(BF16)

    def mm(lo, hi):
        return jnp.dot(hb, w_ref[:, lo:hi], preferred_element_type=F32)

    s5_ref[...] = mm(_C_S5, _C_GLA)
    gla_ref[...] = mm(_C_GLA, _C_RWKV)
    rwkv_ref[...] = mm(_C_RWKV, _C_SB)
    sb_ref[...] = mm(_C_SB, _C_GATE)
    for n in range(N_BRANCH):
        lo = _C_GATE + n * D_MODEL
        gate_ref[:, n * D_MODEL:(n + 1) * D_MODEL] = _sigmoid(mm(lo, lo + D_MODEL)).astype(BF16)


def _time_major_spec(tm, nt):
    return pl.BlockSpec((tm, WIDTH), lambda i: (i % nt, i // nt))


def _inproj(x2d, g, w, tm, nt):
    n = x2d.shape[0]
    row = lambda width: pl.BlockSpec((tm, width), lambda i: (i, 0))
    s5_spec = row(WIDTH) if nt is None else _time_major_spec(tm, nt)
    s5_shape = (n, WIDTH) if nt is None else (nt * tm, n // (nt * tm) * WIDTH)
    return pl.pallas_call(
        _inproj_kernel,
        grid=(n // tm,),
        in_specs=[row(D_MODEL), _const_spec((1, D_MODEL)), _const_spec((D_MODEL, _C_END))],
        out_specs=(s5_spec, row(GLA_COLS_PAD), row(RWKV_COLS), row(SB_COLS), row(GATE_COLS)),
        out_shape=(jax.ShapeDtypeStruct(s5_shape, F32),
                   jax.ShapeDtypeStruct((n, GLA_COLS_PAD), F32),
                   jax.ShapeDtypeStruct((n, RWKV_COLS), F32),
                   jax.ShapeDtypeStruct((n, SB_COLS), F32),
                   jax.ShapeDtypeStruct((n, GATE_COLS), BF16)),
        compiler_params=_params("parallel"),
        name="inproj",
    )(x2d, g, w)


def _s5_kernel(u_ref, x0_ref, bbig_ref, cbig_ref, ar_ref, ai_ref, d_ref, wglu_ref, bglu_ref,
               o_ref, xfin_ref, bu_scr, x_scr, *, nb, lc):
    @pl.when(pl.program_id(0) == 0)
    def _():
        x_scr[...] = x0_ref[...]

    u = u_ref[...]
    bu_scr[...] = _mm(u, bbig_ref[...])
    ar = jnp.broadcast_to(ar_ref[...], (nb, S5_FLAT))
    ai = jnp.broadcast_to(ai_ref[...], (nb, S5_FLAT))

    def step(t, carry):
        xr, xi = carry
        rows = pl.ds(pl.multiple_of(t * nb, nb), nb)
        nxr = ar * xr - ai * xi + bu_scr[rows, 0:S5_FLAT]
        nxi = ar * xi + ai * xr + bu_scr[rows, S5_FLAT:2 * S5_FLAT]
        bu_scr[rows, 0:S5_FLAT] = nxr
        bu_scr[rows, S5_FLAT:2 * S5_FLAT] = nxi
        return nxr, nxi

    xr, xi = lax.fori_loop(0, lc, step, (x_scr[:, 0:S5_FLAT], x_scr[:, S5_FLAT:2 * S5_FLAT]))
    x_scr[:, 0:S5_FLAT] = xr
    x_scr[:, S5_FLAT:2 * S5_FLAT] = xi
    xfin_ref[...] = x_scr[...]

    y = _mm(bu_scr[...], cbig_ref[...]) + d_ref[...] * u
    y = jax.nn.gelu(y)
    o_ref[...] = y * _sigmoid(_mm(y, wglu_ref[...]) + bglu_ref[...])


def _s5(u_tm, x0, p, nb, lc):
    rows = u_tm.shape[0]
    blk = lc * nb
    kern = functools.partial(_s5_kernel, nb=nb, lc=lc)
    return pl.pallas_call(
        kern,
        grid=(rows // blk,),
        in_specs=[pl.BlockSpec((blk, WIDTH), lambda c: (c, 0)),
                  _const_spec((nb, 2 * S5_FLAT)),
                  _const_spec((WIDTH, 2 * S5_FLAT)), _const_spec((2 * S5_FLAT, WIDTH)),
                  _const_spec((1, S5_FLAT)), _const_spec((1, S5_FLAT)), _const_spec((1, WIDTH)),
                  _const_spec((WIDTH, WIDTH)), _const_spec((1, WIDTH))],
        out_specs=(pl.BlockSpec((blk, WIDTH), lambda c: (c, 0)),
                   pl.BlockSpec((nb, 2 * S5_FLAT), lambda c: (0, 0))),
        out_shape=(jax.ShapeDtypeStruct((rows, WIDTH), F32),
                   jax.ShapeDtypeStruct((nb, 2 * S5_FLAT), F32)),
        scratch_shapes=[pltpu.VMEM((blk, 2 * S5_FLAT), F32), pltpu.VMEM((nb, 2 * S5_FLAT), F32)],
        compiler_params=_params("arbitrary"),
        name="s5",
    )(u_tm, x0, p["s5_bbig"], p["s5_cbig"], p["s5_ar"], p["s5_ai"], p["s5_d"], p["s5_w_glu"],
      p["s5_b_glu"])


def _gla_kernel(z_ref, s0_ref, walpha_ref, balpha_ref, gnorm_ref, o_ref, sfin_ref, s_scr, *, nb, L):
    R = nb * L
    DK = GLA_DK

    @pl.when(pl.program_id(1) == 0)
    def _():
        for i in range(nb):
            for h in range(GLA_HEADS):
                s_scr[h, :, DK * i:DK * (i + 1)] = s0_ref[i, h]

    z = z_ref[...].reshape(R, GLA_COLS_PAD)
    q = z[:, 0:GLA_QK] * (GLA_DK ** -0.5)
    k = z[:, GLA_QK:2 * GLA_QK]
    v = z[:, 2 * GLA_QK:2 * GLA_QK + WIDTH]
    g = z[:, 2 * GLA_QK + WIDTH:2 * GLA_QK + 2 * WIDTH]
    alat = z[:, 2 * GLA_QK + 2 * WIDTH:]
    log_a = _log_sigmoid(_mm(alat, walpha_ref[...]) + balpha_ref[...]) * (1.0 / GLA_TAU)
    rr = lax.broadcasted_iota(jnp.int32, (R, R), 0)
    cc = lax.broadcasted_iota(jnp.int32, (R, R), 1)
    tri = jnp.logical_and((rr & -L) == (cc & -L), rr >= cc)
    own = ((lax.broadcasted_iota(jnp.int32, (R, nb * DK), 0) & -L) * DK
           == (lax.broadcasted_iota(jnp.int32, (R, nb * DK), 1) & -DK) * L)
    b = _mm_split_rhs(tri.astype(BF16), log_a)
    b_last = [b[L * (i + 1) - 1:L * (i + 1), :] for i in range(nb)]
    b_end = jnp.concatenate([jnp.broadcast_to(be, (L, GLA_QK)) for be in b_last], axis=0)
    qt = q * jnp.exp(b)
    kt = k * jnp.exp(-b)
    kd = k * jnp.exp(b_end - b)

    def spread(x):
        return jnp.where(own, jnp.concatenate([x] * nb, axis=1), 0.0).astype(BF16)

    outs = []
    for h in range(GLA_HEADS):
        ks = slice(DK * h, DK * (h + 1))
        vs = slice(GLA_DV * h, GLA_DV * (h + 1))
        st = s_scr[h]
        attn = jnp.where(tri, _mm_nt(qt[:, ks], kt[:, ks]), 0.0)
        oh = _mm(attn, v[:, vs]) + _mm_nt(spread(qt[:, ks]), st)
        e_all = jnp.concatenate([jnp.exp(be[:, ks]) for be in b_last], axis=1)
        s_scr[h] = st * e_all + _mm_tn(v[:, vs], spread(kd[:, ks]))
        oh = oh * lax.rsqrt(jnp.mean(oh * oh, axis=-1, keepdims=True) + RMS_EPS) * gnorm_ref[:, vs]
        outs.append(oh)
    o_ref[...] = (jnp.concatenate(outs, axis=-1) * (g * _sigmoid(g))).reshape(nb, L, WIDTH)

    @pl.when(pl.program_id(1) == pl.num_programs(1) - 1)
    def _():
        for i in range(nb):
            for h in range(GLA_HEADS):
                sfin_ref[i, h] = s_scr[h, :, DK * i:DK * (i + 1)]


def _gla(cols, s0t, p, nb, L):
    bt, t, _ = cols.shape
    assert bt % nb == 0 and t % L == 0 and L & (L - 1) == 0
    kern = functools.partial(_gla_kernel, nb=nb, L=L)
    state = pl.BlockSpec((nb, GLA_HEADS, GLA_DV, GLA_DK), lambda b, n: (b, 0, 0, 0))
    return pl.pallas_call(
        kern,
        grid=(bt // nb, t // L),
        in_specs=[pl.BlockSpec((nb, L, GLA_COLS_PAD), lambda b, n: (b, n, 0)), state,
                  _const_spec((128, GLA_QK)), _const_spec((1, GLA_QK)), _const_spec((1, WIDTH))],
        out_specs=(pl.BlockSpec((nb, L, WIDTH), lambda b, n: (b, n, 0)), state),
        out_shape=(jax.ShapeDtypeStruct((bt, t, WIDTH), F32),
                   jax.ShapeDtypeStruct((bt, GLA_HEADS, GLA_DV, GLA_DK), F32)),
        scratch_shapes=[pltpu.VMEM((GLA_HEADS, GLA_DV, nb * GLA_DK), F32)],
        compiler_params=_params("parallel", "arbitrary"),
        name="gla",
    )(cols, s0t, p["gla_w_alpha"], p["gla_b_alpha"], p["gla_norm"])


def _rwkv_kernel(z_ref, shift0_ref, s0_ref, mu_ref, w0_ref, w2_ref, a0_ref, a2_ref, g2_ref, kk_ref,
                 ka_ref, rk_ref, lnw_ref, lnb_ref, bd_ref, o_ref, shiftfin_ref, sfin_ref,
                 s_scr, prev_scr, *, nb, L):
    R = nb * L
    HD = RWKV_HEAD

    @pl.when(pl.program_id(1) == 0)
    def _():
        for i in range(nb):
            for h in range(RWKV_HEADS):
                s_scr[h, :, HD * i:HD * (i + 1)] = s0_ref[i, h]
        prev_scr[...] = shift0_ref[...]

    z = z_ref[...].reshape(R, RWKV_COLS)
    first = (lax.broadcasted_iota(jnp.int32, (R, RWKV_COLS), 0) & (L - 1)) == 0
    carried = jnp.concatenate(
        [jnp.broadcast_to(prev_scr[i], (L, RWKV_COLS)) for i in range(nb)], axis=0)
    prev = jnp.where(first, carried, pltpu.roll(z, 1, axis=0))
    for i in range(nb):
        last = z[L * (i + 1) - 1:L * (i + 1), :]
        prev_scr[i] = last
        shiftfin_ref[i] = last
    zs = z + (prev - z) * mu_ref[...]
    r = zs[:, 0:256]
    k = zs[:, 256:512]
    v = zs[:, 512:768]
    w_lat = zs[:, 768:832]
    a_lat = zs[:, 832:896]
    g_lat = zs[:, 896:1024]
    w = -jax.nn.softplus(-(w0_ref[...] + _mm(jnp.tanh(w_lat), w2_ref[...]))) - 0.5
    log_w = -jnp.exp(w)
    a = _sigmoid(a0_ref[...] + _mm(a_lat, a2_ref[...]))
    g = _mm(_sigmoid(g_lat), g2_ref[...])
    kk = k * kk_ref[...]
    kk = kk / jnp.maximum(jnp.sqrt(_mm_split_lhs(kk * kk, bd_ref[...])), 1e-12)
    k = k * (1.0 + (a - 1.0) * ka_ref[...])
    vec_a = -kk
    vec_b = kk * a

    rr = lax.broadcasted_iota(jnp.int32, (R, R), 0)
    cc = lax.broadcasted_iota(jnp.int32, (R, R), 1)
    same = (rr & -L) == (cc & -L)
    tri = jnp.logical_and(same, rr >= cc)
    stri = jnp.logical_and(same, rr > cc)
    eye = (rr == cc).astype(F32)
    own = ((lax.broadcasted_iota(jnp.int32, (R, nb * HD), 0) & -L) * HD
           == (lax.broadcasted_iota(jnp.int32, (R, nb * HD), 1) & -HD) * L)

    cum = _mm_split_rhs(tri.astype(BF16), log_w)
    p_t = jnp.exp(cum)
    inv_p = jnp.exp(-cum)
    p_last = [p_t[L * (i + 1) - 1:L * (i + 1), :] for i in range(nb)]
    p_end = jnp.concatenate([jnp.broadcast_to(pe, (L, WIDTH)) for pe in p_last], axis=0)
    a_til = vec_a * jnp.exp(cum - log_w)
    b_til = vec_b * inv_p
    k_til = k * inv_p
    r_til = r * p_t
    b_end = b_til * p_end
    k_end = k_til * p_end

    def spread(x):
        return jnp.where(own, jnp.concatenate([x] * nb, axis=1), 0.0).astype(BF16)

    outs = []
    for h in range(RWKV_HEADS):
        hs = slice(HD * h, HD * (h + 1))
        s_all = s_scr[h]
        ah, bh, kh, rh, vh = a_til[:, hs], b_til[:, hs], k_til[:, hs], r_til[:, hs], v[:, hs]
        gram = _mm_nt(jnp.concatenate([ah, rh], axis=0), jnp.concatenate([bh, kh], axis=0))
        a_ab = jnp.where(stri, gram[:R, :R], 0.0)
        a_ak = jnp.where(stri, gram[:R, R:], 0.0).astype(BF16)
        a_r = jnp.concatenate([jnp.where(tri, gram[R:, :R], 0.0).astype(BF16),
                               jnp.where(tri, gram[R:, R:], 0.0).astype(BF16)], axis=1)
        tinv = eye + a_ab
        pw = a_ab
        for _ in range(int(math.log2(L)) - 1):
            pw = _mm(pw, pw)
            tinv = tinv + _mm(tinv, pw)
        from_s = _mm_nt(jnp.concatenate([spread(ah), spread(rh)], axis=0), s_all)
        vb = vh.astype(BF16)
        sa = _mm(tinv, from_s[:R] + jnp.dot(a_ak, vb, preferred_element_type=F32))
        sa_v = jnp.concatenate([sa.astype(BF16), vb], axis=0)
        y = from_s[R:] + jnp.dot(a_r, sa_v, preferred_element_type=F32)
        p_all = jnp.concatenate([pe[:, hs] for pe in p_last], axis=1)
        s_scr[h] = s_all * p_all + _mm_tn(
            sa_v, jnp.concatenate([spread(b_end[:, hs]), spread(k_end[:, hs])], axis=0))
        mean = jnp.mean(y, axis=-1, keepdims=True)
        var = jnp.mean(jnp.square(y - mean), axis=-1, keepdims=True)
        y = (y - mean) * lax.rsqrt(var + RWKV_GN_EPS) * lnw_ref[:, hs] + lnb_ref[:, hs]
        bonus = jnp.sum(r[:, hs] * k[:, hs] * rk_ref[:, hs], axis=-1, keepdims=True)
        outs.append(y + bonus * vh)
    o_ref[...] = (jnp.concatenate(outs, axis=-1) * g).reshape(nb, L, WIDTH)

    @pl.when(pl.program_id(1) == pl.num_programs(1) - 1)
    def _():
        for i in range(nb):
            for h in range(RWKV_HEADS):
                sfin_ref[i, h] = s_scr[h, :, HD * i:HD * (i + 1)]


def _rwkv(cols, shift0, s0, p, nb, L):
    bt, t, _ = cols.shape
    assert bt % nb == 0 and t % L == 0 and L & (L - 1) == 0
    kern = functools.partial(_rwkv_kernel, nb=nb, L=L)
    state = pl.BlockSpec((nb, RWKV_HEADS, RWKV_HEAD, RWKV_HEAD), lambda b, n: (b, 0, 0, 0))
    shift = pl.BlockSpec((nb, 1, RWKV_COLS), lambda b, n: (b, 0, 0))
    vec = _const_spec((1, WIDTH))
    return pl.pallas_call(
        kern,
        grid=(bt // nb, t // L),
        in_specs=[pl.BlockSpec((nb, L, RWKV_COLS), lambda b, n: (b, n, 0)), shift, state,
                  _const_spec((1, RWKV_COLS)), vec, _const_spec((64, WIDTH)), vec,
                  _const_spec((64, WIDTH)), _const_spec((128, WIDTH)), vec, vec, vec, vec, vec,
                  _const_spec((WIDTH, WIDTH))],
        out_specs=(pl.BlockSpec((nb, L, WIDTH), lambda b, n: (b, n, 0)), shift, state),
        out_shape=(jax.ShapeDtypeStruct((bt, t, WIDTH), F32),
                   jax.ShapeDtypeStruct((bt, 1, RWKV_COLS), F32),
                   jax.ShapeDtypeStruct((bt, RWKV_HEADS, RWKV_HEAD, RWKV_HEAD), F32)),
        scratch_shapes=[pltpu.VMEM((RWKV_HEADS, RWKV_HEAD, nb * RWKV_HEAD), F32),
                        pltpu.VMEM((nb, 1, RWKV_COLS), F32)],
        compiler_params=_params("parallel", "arbitrary"),
        name="rwkv",
    )(cols, shift0, s0, p["rwkv_mu"], p["rwkv_w0"], p["rwkv_w2"], p["rwkv_a0"], p["rwkv_a2"],
      p["rwkv_g2"], p["rwkv_k_k"], p["rwkv_k_a"], p["rwkv_r_k"], p["rwkv_ln_w"], p["rwkv_ln_b"],
      p["head_ones"])


def _sb_prep_kernel(z_ref, qn_ref, kn_ref, bd_ref, q_ref, k_ref, v_ref):
    z = z_ref[...]
    q = z[:, 0:WIDTH]
    k = z[:, WIDTH:2 * WIDTH]

    def headnorm(x, g):
        ms = _mm_split_lhs(x * x, bd_ref[...]) * (1.0 / SB_HEAD)
        return x * lax.rsqrt(ms + RMS_EPS) * g

    q_ref[...] = headnorm(q, qn_ref[...])
    k_ref[...] = headnorm(k, kn_ref[...])
    v_ref[...] = z[:, 2 * WIDTH:]


def _sb_prep(cols2d, p, tm):
    n = cols2d.shape[0]
    row = lambda width: pl.BlockSpec((tm, width), lambda i: (i, 0))
    out = jax.ShapeDtypeStruct((n, WIDTH), F32)
    return pl.pallas_call(
        _sb_prep_kernel,
        grid=(n // tm,),
        in_specs=[row(SB_COLS), _const_spec((1, WIDTH)), _const_spec((1, WIDTH)),
                  _const_spec((WIDTH, WIDTH))],
        out_specs=(row(WIDTH), row(WIDTH), row(WIDTH)),
        out_shape=(out, out, out),
        compiler_params=_params("parallel"),
        name="sb_prep",
    )(cols2d, p["sb_q_norm"], p["sb_k_norm"], p["head_ones"])


SB_DEAD_LOG = -110.0
SB_LANES = 128


def _sb_block(qh, k, v, diag, suffix_ones, acc_scr, carry_scr):
    tq = qh[0].shape[0]
    tk = k.shape[0]
    kb = k.astype(BF16)
    z = jnp.concatenate(
        [lax.dot_general(qh[h], kb[:, SB_HEAD * h:SB_HEAD * (h + 1)], (((1,), (1,)), ((), ())),
                         preferred_element_type=F32) for h in range(SB_HEADS)], axis=0)
    lp = _log_sigmoid(z)
    lm = lp - z
    if diag:
        row = lax.broadcasted_iota(jnp.int32, z.shape, 0) & (tq - 1)
        mask = row > lax.broadcasted_iota(jnp.int32, z.shape, 1)
        lm = jnp.where(mask, lm, 0.0)
    sums = _mm_split_lhs(lm, suffix_ones)
    carry = carry_scr[...]
    wgt = jnp.exp(lp + carry[:, :tk] + sums[:, :tk])
    if diag:
        wgt = jnp.where(mask, wgt, 0.0)
    wgt = wgt.astype(BF16)
    vb = v.astype(BF16)
    for h in range(SB_HEADS):
        acc_scr[h] += jnp.dot(wgt[tq * h:tq * (h + 1)], vb[:, SB_HEAD * h:SB_HEAD * (h + 1)],
                              preferred_element_type=F32)
    carry = carry + sums[:, tk:]
    carry_scr[...] = carry
    return jnp.max(carry)


def _suffix_ones(tk):
    return jnp.concatenate([_tri(tk, strict=True).astype(BF16), jnp.ones((tk, SB_LANES), BF16)],
                           axis=1)


def _sb_heads(q):
    return [q[:, SB_HEAD * h:SB_HEAD * (h + 1)].astype(BF16) for h in range(SB_HEADS)]


def _sb_self_kernel(q_ref, k_ref, v_ref, o_ref, acc_scr, carry_scr, *, tq):
    i = pl.program_id(1)
    acc_scr[...] = jnp.zeros_like(acc_scr)
    carry_scr[...] = jnp.zeros_like(carry_scr)
    qh = _sb_heads(q_ref[0])
    ones = _suffix_ones(tq)

    def rows(j):
        return pl.ds(pl.multiple_of(j * tq, tq), tq)

    live = _sb_block(qh, k_ref[0, rows(i), :], v_ref[0, rows(i), :], True, ones, acc_scr, carry_scr)

    def body(state):
        j, _ = state
        m = _sb_block(qh, k_ref[0, rows(j), :], v_ref[0, rows(j), :], False, ones, acc_scr, carry_scr)
        return j - 1, m

    lax.while_loop(lambda s: jnp.logical_and(s[0] >= 0, s[1] > SB_DEAD_LOG), body, (i - 1, live))
    o_ref[0] = jnp.concatenate([acc_scr[h] for h in range(SB_HEADS)], axis=-1)


def _sb_self(q, k, v, tq):
    bt, t, _ = q.shape
    kern = functools.partial(_sb_self_kernel, tq=tq)
    seq = pl.BlockSpec((1, t, WIDTH), lambda b, i: (b, 0, 0))
    blk = pl.BlockSpec((1, tq, WIDTH), lambda b, i: (b, i, 0))
    return pl.pallas_call(
        kern,
        grid=(bt, t // tq),
        in_specs=[blk, seq, seq],
        out_specs=blk,
        out_shape=jax.ShapeDtypeStruct((bt, t, WIDTH), F32),
        scratch_shapes=[pltpu.VMEM((SB_HEADS, tq, SB_HEAD), F32),
                        pltpu.VMEM((SB_HEADS * tq, SB_LANES), F32)],
        compiler_params=_params("parallel", "arbitrary"),
        name="sb_self",
    )(q, k, v)


def _sb_past_kernel(q_ref, k_ref, v_ref, pk_ref, pv_ref, o_ref, acc_scr, carry_scr, *, t, tkp, sub):
    j = pl.program_id(1)
    qh = _sb_heads(q_ref[0])

    @pl.when(j == 0)
    def _():
        acc_scr[...] = jnp.zeros_like(acc_scr)
        carry_scr[...] = jnp.zeros_like(carry_scr)
        _sb_block(qh, k_ref[0], v_ref[0], True, _suffix_ones(t), acc_scr, carry_scr)

    @pl.when(j > 0)
    def _():
        ones = _suffix_ones(sub)

        def body(state):
            s, _ = state
            rows = pl.ds(pl.multiple_of(s * sub, sub), sub)
            m = _sb_block(qh, pk_ref[0, rows, :], pv_ref[0, rows, :], False, ones, acc_scr, carry_scr)
            return s - 1, m

        lax.while_loop(lambda s: jnp.logical_and(s[0] >= 0, s[1] > SB_DEAD_LOG), body,
                       (tkp // sub - 1, jnp.max(carry_scr[...])))

    @pl.when(j == pl.num_programs(1) - 1)
    def _():
        o_ref[0] = jnp.concatenate([acc_scr[h] for h in range(SB_HEADS)], axis=-1)


def _sb_past(q, k, v, past_k, past_v, tkp, sub):
    bt, t, _ = q.shape
    np_ = past_k.shape[1] // tkp
    kern = functools.partial(_sb_past_kernel, t=t, tkp=tkp, sub=sub)
    new = pl.BlockSpec((1, t, WIDTH), lambda b, j: (b, 0, 0))
    past = pl.BlockSpec((1, tkp, WIDTH), lambda b, j: (b, np_ - jnp.maximum(j, 1), 0))
    return pl.pallas_call(
        kern,
        grid=(bt, np_ + 1),
        in_specs=[new, new, new, past, past],
        out_specs=new,
        out_shape=jax.ShapeDtypeStruct((bt, t, WIDTH), F32),
        scratch_shapes=[pltpu.VMEM((SB_HEADS, t, SB_HEAD), F32),
                        pltpu.VMEM((SB_HEADS * t, SB_LANES), F32)],
        compiler_params=_params("parallel", "arbitrary"),
        name="sb_past",
    )(q, k, v, past_k, past_v)


def _merge_ffn_kernel(x_ref, oa_ref, ob_ref, oc_ref, od_ref, gate_ref, wb_ref, wout_ref, nffn_ref,
                      wg_ref, wu_ref, wd_ref, y_ref):
    merged = None
    for n, o_ref in enumerate((oa_ref, ob_ref, oc_ref, od_ref)):
        term = gate_ref[:, n * D_MODEL:(n + 1) * D_MODEL].astype(F32) * _mm(o_ref[...], wb_ref[n])
        merged = term if merged is None else merged + term
    x = x_ref[...] + _mm(merged, wout_ref[...])
    h = x * lax.rsqrt(jnp.mean(x * x, axis=-1, keepdims=True) + RMS_EPS) * nffn_ref[...]
    hb = h.astype(BF16)
    gate = jnp.dot(hb, wg_ref[...], preferred_element_type=F32)
    up = jnp.dot(hb, wu_ref[...], preferred_element_type=F32)
    y_ref[...] = x + _mm(gate * _sigmoid(gate) * up, wd_ref[...])


def _merge_ffn(x2d, oa, ob, oc, od, gates, p, tm, nt):
    n = x2d.shape[0]
    row = lambda width: pl.BlockSpec((tm, width), lambda i: (i, 0))
    oa_spec = row(WIDTH) if nt is None else _time_major_spec(tm, nt)
    return pl.pallas_call(
        _merge_ffn_kernel,
        grid=(n // tm,),
        in_specs=[row(D_MODEL), oa_spec, row(WIDTH), row(WIDTH), row(WIDTH), row(GATE_COLS),
                  _const_spec((N_BRANCH, WIDTH, D_MODEL)), _const_spec((D_MODEL, D_MODEL)),
                  _const_spec((1, D_MODEL)), _const_spec((D_MODEL, D_FF)),
                  _const_spec((D_MODEL, D_FF)), _const_spec((D_FF, D_MODEL))],
        out_specs=row(D_MODEL),
        out_shape=jax.ShapeDtypeStruct((n, D_MODEL), F32),
        compiler_params=_params("parallel"),
        name="merge_ffn",
    )(x2d, oa, ob, oc, od, gates, p["w_branch"], p["w_out"], p["norm_ffn"], p["w_ffn_gate"],
      p["w_ffn_up"], p["w_ffn_down"])


def _prep_layer(w):
    p = {}
    w_in = w["w_in"]
    o_gla = WIDTH
    o_alat = o_gla + 2 * GLA_QK + 2 * WIDTH
    o_rwkv = o_alat + GLA_RANK
    o_sb = o_rwkv + RWKV_COLS
    o_gate = o_sb + SB_COLS
    pad = jnp.zeros((D_MODEL, 128 - GLA_RANK), w_in.dtype)
    p["w_in"] = jnp.concatenate(
        [w_in[:, :o_alat], w_in[:, o_alat:o_rwkv], pad, w_in[:, o_rwkv:]], axis=1).astype(BF16)
    p["norm_mix"] = w["norm_mix"].reshape(1, D_MODEL)

    lam = lax.complex(w["s5_a_re"], w["s5_a_im"])
    dt = jnp.exp(w["s5_log_dt"])[:, None]
    a_bar = jnp.exp(lam * dt)
    b_bar = ((a_bar - 1.0) / lam)[..., None] * lax.complex(w["s5_b_re"], w["s5_b_im"])
    eye = jnp.eye(S5_GROUPS, dtype=F32)

    def in_map(m):
        return jnp.einsum("gpc,gh->gchp", m, eye).reshape(WIDTH, S5_FLAT)

    def out_map(m):
        return jnp.einsum("gcp,gh->gphc", m, eye).reshape(S5_FLAT, WIDTH)

    p["s5_bbig"] = jnp.concatenate([in_map(b_bar.real), in_map(b_bar.imag)], axis=1).astype(BF16)
    p["s5_cbig"] = jnp.concatenate([out_map(w["s5_c_re"]), out_map(-w["s5_c_im"])], axis=0).astype(BF16)
    p["s5_ar"] = a_bar.real.reshape(1, S5_FLAT)
    p["s5_ai"] = a_bar.imag.reshape(1, S5_FLAT)
    p["s5_d"] = w["s5_d"].reshape(1, WIDTH)
    p["s5_w_glu"] = w["s5_w_glu"].astype(BF16)
    p["s5_b_glu"] = w["s5_b_glu"].reshape(1, WIDTH)

    p["gla_w_alpha"] = jnp.concatenate(
        [w["gla_w_alpha"], jnp.zeros((128 - GLA_RANK, GLA_QK), F32)], axis=0).astype(BF16)
    p["gla_b_alpha"] = w["gla_b_alpha"].reshape(1, GLA_QK)
    p["gla_norm"] = w["gla_norm"].reshape(1, WIDTH)

    for name in ("rwkv_w0", "rwkv_a0", "rwkv_k_k", "rwkv_k_a", "rwkv_r_k", "rwkv_ln_w", "rwkv_ln_b"):
        p[name] = w[name].reshape(1, WIDTH)
    p["rwkv_mu"] = w["rwkv_mu"].reshape(1, RWKV_COLS)
    for name in ("rwkv_w2", "rwkv_a2", "rwkv_g2"):
        p[name] = w[name].astype(BF16)
    head = jnp.arange(WIDTH) // RWKV_HEAD
    p["head_ones"] = (head[:, None] == head[None, :]).astype(BF16)

    p["sb_q_norm"] = jnp.tile(w["sb_q_norm"] * (SB_HEAD ** -0.5), SB_HEADS).reshape(1, WIDTH)
    p["sb_k_norm"] = jnp.tile(w["sb_k_norm"], SB_HEADS).reshape(1, WIDTH)

    p["w_branch"] = w["w_branch"].astype(BF16)
    p["w_out"] = w["w_out"].astype(BF16)
    p["norm_ffn"] = w["norm_ffn"].reshape(1, D_MODEL)
    for name in ("w_ffn_gate", "w_ffn_up", "w_ffn_down"):
        p[name] = w[name].astype(BF16)
    return p


def _pick(n, prefs):
    for c in prefs:
        if n % c == 0:
            return c
    return n


def _layer(x, past_k, past_v, s5_0, gla_0, rwkv_0, shift_0, p):
    bt, t, _ = x.shape
    n = bt * t
    x2d = x.reshape(n, D_MODEL)
    tm = _pick(n, (256, 128, 64, 32, 16, 8))
    nt = t // tm if t % tm == 0 else None
    u, c_gla, c_rwkv, c_sb, gates = _inproj(x2d, p["norm_mix"], p["w_in"], tm, nt)

    if nt is None:
        u = u.reshape(bt, t, WIDTH).transpose(1, 0, 2)
    x0 = jnp.concatenate([s5_0[..., 0].reshape(bt, S5_FLAT), s5_0[..., 1].reshape(bt, S5_FLAT)], axis=1)
    lc = _pick(t, (64, 32, 16, 8))
    o_a, xfin = _s5(u.reshape(n, WIDTH), x0, p, bt, lc)
    if nt is None:
        o_a = o_a.reshape(t, bt, WIDTH).transpose(1, 0, 2).reshape(n, WIDTH)
    else:
        o_a = o_a.reshape(t, bt * WIDTH)
    s5_new = jnp.stack([xfin[:, :S5_FLAT].reshape(bt, S5_GROUPS, S5_STATE),
                        xfin[:, S5_FLAT:].reshape(bt, S5_GROUPS, S5_STATE)], axis=-1)

    chunk = _pick(t, (64, 32, 16, 8))
    nseq = _pick(bt, (RWKV_ROWS // chunk, 4, 2, 1))
    o_b, gla_t = _gla(c_gla.reshape(bt, t, GLA_COLS_PAD), jnp.swapaxes(gla_0, 2, 3), p, nseq, chunk)
    gla_new = jnp.swapaxes(gla_t, 2, 3)

    o_c, shift_new, rwkv_new = _rwkv(c_rwkv.reshape(bt, t, RWKV_COLS),
                                     shift_0.reshape(bt, 1, RWKV_COLS), rwkv_0, p, nseq, chunk)

    q, k, v = _sb_prep(c_sb, p, tm)
    q3, k3, v3 = (a.reshape(bt, t, WIDTH) for a in (q, k, v))
    if past_k is None:
        o_d = _sb_self(q3, k3, v3, _pick(t, (128, 64, 32, 16, 8)))
    else:
        pl_ = past_k.shape[1]
        tkp = _pick(pl_, (512, 256, 128, 64, 32, 16, 8))
        o_d = _sb_past(q3, k3, v3, past_k.reshape(bt, pl_, WIDTH), past_v.reshape(bt, pl_, WIDTH),
                       tkp, min(tkp, SB_LANES))

    y = _merge_ffn(x2d, o_a, o_b.reshape(n, WIDTH), o_c.reshape(n, WIDTH), o_d.reshape(n, WIDTH),
                   gates, p, tm, nt)
    states = (k3.reshape(bt, t, SB_HEADS, SB_HEAD), v3.reshape(bt, t, SB_HEADS, SB_HEAD),
              s5_new, gla_new, rwkv_new, shift_new.reshape(bt, RWKV_COLS))
    return y.reshape(bt, t, D_MODEL), states


def _trunk(x, past_k, past_v, s5_0, gla_0, rwkv_0, shift_0, layers):
    per_layer = []
    for l, p in enumerate(layers):
        pk = None if past_k is None else past_k[l]
        pv = None if past_v is None else past_v[l]
        x, st = _layer(x, pk, pv, s5_0[l], gla_0[l], rwkv_0[l], shift_0[l], p)
        per_layer.append(st)
    return x, [jnp.stack([st[i] for st in per_layer]) for i in range(6)]


def kernel(x_prompt, x_sample, cache_sb_k, cache_sb_v, state_s5, state_gla, state_rwkv, state_rwkv_shift, norm_mix, w_in, s5_a_re, s5_a_im, s5_log_dt, s5_b_re, s5_b_im, s5_c_re, s5_c_im, s5_d, s5_w_glu, s5_b_glu, gla_w_alpha, gla_b_alpha, gla_norm, rwkv_mu, rwkv_w0, rwkv_w2, rwkv_a0, rwkv_a2, rwkv_g2, rwkv_k_k, rwkv_k_a, rwkv_r_k, rwkv_ln_w, rwkv_ln_b, sb_q_norm, sb_k_norm, w_branch, w_out, norm_ffn, w_ffn_gate, w_ffn_up, w_ffn_down):
    weights = dict(norm_mix=norm_mix, w_in=w_in, s5_a_re=s5_a_re, s5_a_im=s5_a_im, s5_log_dt=s5_log_dt,
                   s5_b_re=s5_b_re, s5_b_im=s5_b_im, s5_c_re=s5_c_re, s5_c_im=s5_c_im, s5_d=s5_d,
                   s5_w_glu=s5_w_glu, s5_b_glu=s5_b_glu, gla_w_alpha=gla_w_alpha, gla_b_alpha=gla_b_alpha,
                   gla_norm=gla_norm, rwkv_mu=rwkv_mu, rwkv_w0=rwkv_w0, rwkv_w2=rwkv_w2, rwkv_a0=rwkv_a0,
                   rwkv_a2=rwkv_a2, rwkv_g2=rwkv_g2, rwkv_k_k=rwkv_k_k, rwkv_k_a=rwkv_k_a, rwkv_r_k=rwkv_r_k,
                   rwkv_ln_w=rwkv_ln_w, rwkv_ln_b=rwkv_ln_b, sb_q_norm=sb_q_norm, sb_k_norm=sb_k_norm,
                   w_branch=w_branch, w_out=w_out, norm_ffn=norm_ffn, w_ffn_gate=w_ffn_gate,
                   w_ffn_up=w_ffn_up, w_ffn_down=w_ffn_down)
    depth = w_in.shape[0]
    layers = [_prep_layer({name: arr[l] for name, arr in weights.items()}) for l in range(depth)]

    bp = x_prompt.shape[0]
    y_prompt, p_states = _trunk(
        x_prompt, None, None,
        jnp.zeros((depth, bp, S5_GROUPS, S5_STATE, 2), F32),
        jnp.zeros((depth, bp, GLA_HEADS, GLA_DK, GLA_DV), F32),
        jnp.zeros((depth, bp, RWKV_HEADS, RWKV_HEAD, RWKV_HEAD), F32),
        jnp.zeros((depth, bp, RWKV_COLS), F32), layers)
    y_sample, s_states = _trunk(x_sample, cache_sb_k, cache_sb_v, state_s5, state_gla, state_rwkv,
                                state_rwkv_shift, layers)
    return (y_prompt, y_sample, *p_states, *s_states)
```

```python
import functools
import math

import jax
import jax.numpy as jnp
from jax import lax
from jax.experimental import pallas as pl
from jax.experimental.pallas import tpu as pltpu

F32 = jnp.float32
BF16 = jnp.bfloat16

D_MODEL = 1024
WIDTH = 256
N_BRANCH = 4
S5_GROUPS, S5_GROUP, S5_STATE = 16, 16, 64
S5_FLAT = S5_GROUPS * S5_STATE
GLA_HEADS, GLA_DK, GLA_DV, GLA_RANK, GLA_TAU = 4, 32, 64, 16, 16.0
GLA_QK = GLA_HEADS * GLA_DK
GLA_COLS_PAD = 2 * GLA_QK + 2 * WIDTH + 128
RWKV_HEADS, RWKV_HEAD = 4, 64
RWKV_COLS = 1024
RWKV_GN_EPS = 64e-5
RWKV_ROWS = 256
SB_HEADS, SB_HEAD = 4, 64
SB_COLS = 3 * WIDTH
GATE_COLS = N_BRANCH * D_MODEL
D_FF = 2816
RMS_EPS = 1e-6

_C_S5 = 0
_C_GLA = _C_S5 + WIDTH
_C_RWKV = _C_GLA + GLA_COLS_PAD
_C_SBQ = _C_RWKV + RWKV_COLS
_C_GATE = _C_SBQ + WIDTH
_C_END = _C_GATE + GATE_COLS

VMEM_LIMIT = 56 * 1024 * 1024


def _params(*sem):
    return pltpu.CompilerParams(dimension_semantics=sem, vmem_limit_bytes=VMEM_LIMIT)


def _const_spec(shape):
    nd = len(shape)
    return pl.BlockSpec(shape, lambda *_: (0,) * nd, pipeline_mode=pl.Buffered(1))


def _mm(a, b):
    return jnp.dot(a.astype(BF16), b.astype(BF16), preferred_element_type=F32)


def _mm_nt(a, b):
    return lax.dot_general(a.astype(BF16), b.astype(BF16), (((1,), (1,)), ((), ())),
                           preferred_element_type=F32)


def _mm_tn(a, b):
    return lax.dot_general(a.astype(BF16), b.astype(BF16), (((0,), (0,)), ((), ())),
                           preferred_element_type=F32)


def _split(a):
    hi = a.astype(BF16)
    lo = (a - hi.astype(F32)).astype(BF16)
    return hi, lo


def _mm_split_lhs(a, b01):
    hi, lo = _split(a)
    return (jnp.dot(hi, b01, preferred_element_type=F32)
            + jnp.dot(lo, b01, preferred_element_type=F32))


def _mm_split_rhs(a01, b):
    hi, lo = _split(b)
    return (jnp.dot(a01, hi, preferred_element_type=F32)
            + jnp.dot(a01, lo, preferred_element_type=F32))


def _log_sigmoid(z):
    return jnp.minimum(z, 0.0) - jnp.log1p(jnp.exp(-jnp.abs(z)))


def _sigmoid(z):
    return 1.0 / (1.0 + jnp.exp(-z))


def _tri(n, strict=False):
    r = lax.broadcasted_iota(jnp.int32, (n, n), 0)
    c = lax.broadcasted_iota(jnp.int32, (n, n), 1)
    return (r > c) if strict else (r >= c)


def _inproj_kernel(x_ref, g_ref, w_ref, wkv_ref, qn_ref, kn_ref, ones_ref,
                   s5_ref, gla_ref, rwkv_ref, q_ref, kt_ref, vt_ref, gate_ref, *, nseq):
    x = x_ref[...]
    h = x * lax.rsqrt(jnp.mean(x * x, axis=-1, keepdims=True) + RMS_EPS) * g_ref[...]
    hb = h.astype(BF16)

    def mm(lo, hi):
        return jnp.dot(hb, w_ref[:, lo:hi], preferred_element_type=F32)

    s5_ref[...] = mm(_C_S5, _C_GLA)
    gla_ref[...] = mm(_C_GLA, _C_RWKV)
    rwkv_ref[...] = mm(_C_RWKV, _C_SBQ)
    for n in range(N_BRANCH):
        lo = _C_GATE + n * D_MODEL
        gate_ref[:, n * D_MODEL:(n + 1) * D_MODEL] = _sigmoid(mm(lo, lo + D_MODEL)).astype(BF16)

    q = mm(_C_SBQ, _C_GATE)
    q_ms = _mm_split_lhs(q * q, ones_ref[...]) * (1.0 / SB_HEAD)
    q_ref[...] = (q * lax.rsqrt(q_ms + RMS_EPS) * qn_ref[...]).astype(BF16)

    kv_t = lax.dot_general(wkv_ref[...], hb, (((1,), (1,)), ((), ())), preferred_element_type=F32)
    k_t = []
    for hd in range(SB_HEADS):
        kh = kv_t[SB_HEAD * hd:SB_HEAD * (hd + 1)]
        k_t.append(kh * lax.rsqrt(jnp.mean(kh * kh, axis=0, keepdims=True) + RMS_EPS))
    k_t = jnp.concatenate(k_t, axis=0) * kn_ref[...]
    v_t = kv_t[WIDTH:2 * WIDTH]
    t = kt_ref.shape[2]
    for s in range(nseq):
        kt_ref[s] = k_t[:, t * s:t * (s + 1)]
        vt_ref[s] = v_t[:, t * s:t * (s + 1)]


def _time_major_spec(tm, nt):
    return pl.BlockSpec((tm, WIDTH), lambda i: (i % nt, i // nt))


def _inproj(x2d, p, bt, t, tm, nt):
    n = x2d.shape[0]
    row = lambda width: pl.BlockSpec((tm, width), lambda i: (i, 0))
    if nt is None:
        nseq = tm // t
        s5_spec, s5_shape = row(WIDTH), (n, WIDTH)
        kv_spec = pl.BlockSpec((nseq, WIDTH, t), lambda i: (i, 0, 0))
    else:
        nseq = 1
        s5_spec, s5_shape = _time_major_spec(tm, nt), (t, bt * WIDTH)
        kv_spec = pl.BlockSpec((1, WIDTH, tm), lambda i: (i // nt, 0, i % nt))
    kv_shape = jax.ShapeDtypeStruct((bt, WIDTH, t), F32)
    return pl.pallas_call(
        functools.partial(_inproj_kernel, nseq=nseq),
        grid=(n // tm,),
        in_specs=[row(D_MODEL), _const_spec((1, D_MODEL)), _const_spec((D_MODEL, _C_END)),
                  _const_spec((2 * WIDTH, D_MODEL)), _const_spec((1, WIDTH)),
                  _const_spec((WIDTH, 1)), _const_spec((WIDTH, WIDTH))],
        out_specs=(s5_spec, row(GLA_COLS_PAD), row(RWKV_COLS), row(WIDTH), kv_spec, kv_spec,
                   row(GATE_COLS)),
        out_shape=(jax.ShapeDtypeStruct(s5_shape, F32),
                   jax.ShapeDtypeStruct((n, GLA_COLS_PAD), F32),
                   jax.ShapeDtypeStruct((n, RWKV_COLS), F32),
                   jax.ShapeDtypeStruct((n, WIDTH), BF16), kv_shape, kv_shape,
                   jax.ShapeDtypeStruct((n, GATE_COLS), BF16)),
        compiler_params=_params("parallel"),
        name="inproj",
    )(x2d, p["norm_mix"], p["w_in"], p["w_kv_t"], p["sb_q_norm"], p["sb_k_norm"], p["head_ones"])


def _s5_kernel(u_ref, x0_ref, bbig_ref, cbig_ref, ar_ref, ai_ref, d_ref, wglu_ref, bglu_ref,
               o_ref, xfin_ref, bu_scr, x_scr, *, nb, lc):
    @pl.when(pl.program_id(0) == 0)
    def _():
        x_scr[...] = x0_ref[...]

    u = u_ref[...]
    bu_scr[...] = _mm(u, bbig_ref[...])
    ar = jnp.broadcast_to(ar_ref[...], (nb, S5_FLAT))
    ai = jnp.broadcast_to(ai_ref[...], (nb, S5_FLAT))

    def step(t, carry):
        xr, xi = carry
        rows = pl.ds(pl.multiple_of(t * nb, nb), nb)
        nxr = ar * xr - ai * xi + bu_scr[rows, 0:S5_FLAT]
        nxi = ar * xi + ai * xr + bu_scr[rows, S5_FLAT:2 * S5_FLAT]
        bu_scr[rows, 0:S5_FLAT] = nxr
        bu_scr[rows, S5_FLAT:2 * S5_FLAT] = nxi
        return nxr, nxi

    xr, xi = lax.fori_loop(0, lc, step, (x_scr[:, 0:S5_FLAT], x_scr[:, S5_FLAT:2 * S5_FLAT]))
    x_scr[:, 0:S5_FLAT] = xr
    x_scr[:, S5_FLAT:2 * S5_FLAT] = xi
    xfin_ref[...] = x_scr[...]

    y = _mm(bu_scr[...], cbig_ref[...]) + d_ref[...] * u
    y = jax.nn.gelu(y)
    o_ref[...] = y * _sigmoid(_mm(y, wglu_ref[...]) + bglu_ref[...])


def _s5(u_tm, x0, p, nb, lc):
    rows = u_tm.shape[0]
    blk = lc * nb
    kern = functools.partial(_s5_kernel, nb=nb, lc=lc)
    return pl.pallas_call(
        kern,
        grid=(rows // blk,),
        in_specs=[pl.BlockSpec((blk, WIDTH), lambda c: (c, 0)),
                  _const_spec((nb, 2 * S5_FLAT)),
                  _const_spec((WIDTH, 2 * S5_FLAT)), _const_spec((2 * S5_FLAT, WIDTH)),
                  _const_spec((1, S5_FLAT)), _const_spec((1, S5_FLAT)), _const_spec((1, WIDTH)),
                  _const_spec((WIDTH, WIDTH)), _const_spec((1, WIDTH))],
        out_specs=(pl.BlockSpec((blk, WIDTH), lambda c: (c, 0)),
                   pl.BlockSpec((nb, 2 * S5_FLAT), lambda c: (0, 0))),
        out_shape=(jax.ShapeDtypeStruct((rows, WIDTH), F32),
                   jax.ShapeDtypeStruct((nb, 2 * S5_FLAT), F32)),
        scratch_shapes=[pltpu.VMEM((blk, 2 * S5_FLAT), F32), pltpu.VMEM((nb, 2 * S5_FLAT), F32)],
        compiler_params=_params("arbitrary"),
        name="s5",
    )(u_tm, x0, p["s5_bbig"], p["s5_cbig"], p["s5_ar"], p["s5_ai"], p["s5_d"], p["s5_w_glu"],
      p["s5_b_glu"])


def _gla_kernel(z_ref, s0_ref, walpha_ref, balpha_ref, gnorm_ref, o_ref, sfin_ref, s_scr, *, nb, L):
    R = nb * L
    DK = GLA_DK

    @pl.when(pl.program_id(1) == 0)
    def _():
        for i in range(nb):
            for h in range(GLA_HEADS):
                s_scr[h, :, DK * i:DK * (i + 1)] = s0_ref[i, h]

    z = z_ref[...].reshape(R, GLA_COLS_PAD)
    q = z[:, 0:GLA_QK] * (GLA_DK ** -0.5)
    k = z[:, GLA_QK:2 * GLA_QK]
    v = z[:, 2 * GLA_QK:2 * GLA_QK + WIDTH]
    g = z[:, 2 * GLA_QK + WIDTH:2 * GLA_QK + 2 * WIDTH]
    alat = z[:, 2 * GLA_QK + 2 * WIDTH:]
    log_a = _log_sigmoid(_mm(alat, walpha_ref[...]) + balpha_ref[...]) * (1.0 / GLA_TAU)
    rr = lax.broadcasted_iota(jnp.int32, (R, R), 0)
    cc = lax.broadcasted_iota(jnp.int32, (R, R), 1)
    tri = jnp.logical_and((rr & -L) == (cc & -L), rr >= cc)
    own = ((lax.broadcasted_iota(jnp.int32, (R, nb * DK), 0) & -L) * DK
           == (lax.broadcasted_iota(jnp.int32, (R, nb * DK), 1) & -DK) * L)
    b = _mm_split_rhs(tri.astype(BF16), log_a)
    b_last = [b[L * (i + 1) - 1:L * (i + 1), :] for i in range(nb)]
    b_end = jnp.concatenate([jnp.broadcast_to(be, (L, GLA_QK)) for be in b_last], axis=0)
    qt = q * jnp.exp(b)
    kt = k * jnp.exp(-b)
    kd = k * jnp.exp(b_end - b)

    def spread(x):
        return jnp.where(own, jnp.concatenate([x] * nb, axis=1), 0.0).astype(BF16)

    outs = []
    for h in range(GLA_HEADS):
        ks = slice(DK * h, DK * (h + 1))
        vs = slice(GLA_DV * h, GLA_DV * (h + 1))
        st = s_scr[h]
        attn = jnp.where(tri, _mm_nt(qt[:, ks], kt[:, ks]), 0.0)
        oh = _mm(attn, v[:, vs]) + _mm_nt(spread(qt[:, ks]), st)
        e_all = jnp.concatenate([jnp.exp(be[:, ks]) for be in b_last], axis=1)
        s_scr[h] = st * e_all + _mm_tn(v[:, vs], spread(kd[:, ks]))
        oh = oh * lax.rsqrt(jnp.mean(oh * oh, axis=-1, keepdims=True) + RMS_EPS) * gnorm_ref[:, vs]
        outs.append(oh)
    o_ref[...] = (jnp.concatenate(outs, axis=-1) * (g * _sigmoid(g))).reshape(nb, L, WIDTH)

    @pl.when(pl.program_id(1) == pl.num_programs(1) - 1)
    def _():
        for i in range(nb):
            for h in range(GLA_HEADS):
                sfin_ref[i, h] = s_scr[h, :, DK * i:DK * (i + 1)]


def _gla(cols, s0t, p, nb, L):
    bt, t, _ = cols.shape
    assert bt % nb == 0 and t % L == 0 and L & (L - 1) == 0
    kern = functools.partial(_gla_kernel, nb=nb, L=L)
    state = pl.BlockSpec((nb, GLA_HEADS, GLA_DV, GLA_DK), lambda b, n: (b, 0, 0, 0))
    return pl.pallas_call(
        kern,
        grid=(bt // nb, t // L),
        in_specs=[pl.BlockSpec((nb, L, GLA_COLS_PAD), lambda b, n: (b, n, 0)), state,
                  _const_spec((128, GLA_QK)), _const_spec((1, GLA_QK)), _const_spec((1, WIDTH))],
        out_specs=(pl.BlockSpec((nb, L, WIDTH), lambda b, n: (b, n, 0)), state),
        out_shape=(jax.ShapeDtypeStruct((bt, t, WIDTH), F32),
                   jax.ShapeDtypeStruct((bt, GLA_HEADS, GLA_DV, GLA_DK), F32)),
        scratch_shapes=[pltpu.VMEM((GLA_HEADS, GLA_DV, nb * GLA_DK), F32)],
        compiler_params=_params("parallel", "arbitrary"),
        name="gla",
    )(cols, s0t, p["gla_w_alpha"], p["gla_b_alpha"], p["gla_norm"])


def _rwkv_kernel(z_ref, shift0_ref, s0_ref, mu_ref, w0_ref, w2_ref, a0_ref, a2_ref, g2_ref, kk_ref,
                 ka_ref, rk_ref, lnw_ref, lnb_ref, bd_ref, o_ref, shiftfin_ref, sfin_ref,
                 s_scr, prev_scr, *, nb, L):
    R = nb * L
    HD = RWKV_HEAD

    @pl.when(pl.program_id(1) == 0)
    def _():
        for i in range(nb):
            for h in range(RWKV_HEADS):
                s_scr[h, :, HD * i:HD * (i + 1)] = s0_ref[i, h]
        prev_scr[...] = shift0_ref[...]

    z = z_ref[...].reshape(R, RWKV_COLS)
    first = (lax.broadcasted_iota(jnp.int32, (R, RWKV_COLS), 0) & (L - 1)) == 0
    carried = jnp.concatenate(
        [jnp.broadcast_to(prev_scr[i], (L, RWKV_COLS)) for i in range(nb)], axis=0)
    prev = jnp.where(first, carried, pltpu.roll(z, 1, axis=0))
    for i in range(nb):
        last = z[L * (i + 1) - 1:L * (i + 1), :]
        prev_scr[i] = last
        shiftfin_ref[i] = last
    zs = z + (prev - z) * mu_ref[...]
    r = zs[:, 0:256]
    k = zs[:, 256:512]
    v = zs[:, 512:768]
    w_lat = zs[:, 768:832]
    a_lat = zs[:, 832:896]
    g_lat = zs[:, 896:1024]
    w = -jax.nn.softplus(-(w0_ref[...] + _mm(jnp.tanh(w_lat), w2_ref[...]))) - 0.5
    log_w = -jnp.exp(w)
    a = _sigmoid(a0_ref[...] + _mm(a_lat, a2_ref[...]))
    g = _mm(_sigmoid(g_lat), g2_ref[...])
    kk = k * kk_ref[...]
    kk = kk / jnp.maximum(jnp.sqrt(_mm_split_lhs(kk * kk, bd_ref[...])), 1e-12)
    k = k * (1.0 + (a - 1.0) * ka_ref[...])
    vec_a = -kk
    vec_b = kk * a

    rr = lax.broadcasted_iota(jnp.int32, (R, R), 0)
    cc = lax.broadcasted_iota(jnp.int32, (R, R), 1)
    same = (rr & -L) == (cc & -L)
    tri = jnp.logical_and(same, rr >= cc)
    stri = jnp.logical_and(same, rr > cc)
    own = ((lax.broadcasted_iota(jnp.int32, (R, nb * HD), 0) & -L) * HD
           == (lax.broadcasted_iota(jnp.int32, (R, nb * HD), 1) & -HD) * L)

    cum = _mm_split_rhs(tri.astype(BF16), log_w)
    p_t = jnp.exp(cum)
    inv_p = jnp.exp(-cum)
    p_last = [p_t[L * (i + 1) - 1:L * (i + 1), :] for i in range(nb)]
    p_end = jnp.concatenate([jnp.broadcast_to(pe, (L, WIDTH)) for pe in p_last], axis=0)
    a_til = vec_a * jnp.exp(cum - log_w)
    b_til = vec_b * inv_p
    k_til = k * inv_p
    r_til = r * p_t
    b_end = b_til * p_end
    k_end = k_til * p_end

    def spread(x):
        return jnp.where(own, jnp.concatenate([x] * nb, axis=1), 0.0).astype(BF16)

    outs = []
    for h in range(RWKV_HEADS):
        hs = slice(HD * h, HD * (h + 1))
        s_all = s_scr[h]
        ah, bh, kh, rh, vh = a_til[:, hs], b_til[:, hs], k_til[:, hs], r_til[:, hs], v[:, hs]
        gram = _mm_nt(jnp.concatenate([ah, rh], axis=0), jnp.concatenate([bh, kh], axis=0))
        a_ab = jnp.where(stri, gram[:R, :R], 0.0)
        a_ak = jnp.where(stri, gram[:R, R:], 0.0).astype(BF16)
        a_r = jnp.concatenate([jnp.where(tri, gram[R:, :R], 0.0).astype(BF16),
                               jnp.where(tri, gram[R:, R:], 0.0).astype(BF16)], axis=1)
        from_s = _mm_nt(jnp.concatenate([spread(ah), spread(rh)], axis=0), s_all)
        vb = vh.astype(BF16)
        sa = from_s[:R] + jnp.dot(a_ak, vb, preferred_element_type=F32)
        pw = a_ab.astype(BF16)
        sa = sa + jnp.dot(pw, sa.astype(BF16), preferred_element_type=F32)
        for _ in range(int(math.log2(L)) - 1):
            pw = jnp.dot(pw, pw, preferred_element_type=F32).astype(BF16)
            sa = sa + jnp.dot(pw, sa.astype(BF16), preferred_element_type=F32)
        sa_v = jnp.concatenate([sa.astype(BF16), vb], axis=0)
        y = from_s[R:] + jnp.dot(a_r, sa_v, preferred_element_type=F32)
        p_all = jnp.concatenate([pe[:, hs] for pe in p_last], axis=1)
        s_scr[h] = s_all * p_all + _mm_tn(
            sa_v, jnp.concatenate([spread(b_end[:, hs]), spread(k_end[:, hs])], axis=0))
        mean = jnp.mean(y, axis=-1, keepdims=True)
        var = jnp.mean(jnp.square(y - mean), axis=-1, keepdims=True)
        y = (y - mean) * lax.rsqrt(var + RWKV_GN_EPS) * lnw_ref[:, hs] + lnb_ref[:, hs]
        bonus = jnp.sum(r[:, hs] * k[:, hs] * rk_ref[:, hs], axis=-1, keepdims=True)
        outs.append(y + bonus * vh)
    o_ref[...] = (jnp.concatenate(outs, axis=-1) * g).reshape(nb, L, WIDTH)

    @pl.when(pl.program_id(1) == pl.num_programs(1) - 1)
    def _():
        for i in range(nb):
            for h in range(RWKV_HEADS):
                sfin_ref[i, h] = s_scr[h, :, HD * i:HD * (i + 1)]


def _rwkv(cols, shift0, s0, p, nb, L):
    bt, t, _ = cols.shape
    assert bt % nb == 0 and t % L == 0 and L & (L - 1) == 0
    kern = functools.partial(_rwkv_kernel, nb=nb, L=L)
    state = pl.BlockSpec((nb, RWKV_HEADS, RWKV_HEAD, RWKV_HEAD), lambda b, n: (b, 0, 0, 0))
    shift = pl.BlockSpec((nb, 1, RWKV_COLS), lambda b, n: (b, 0, 0))
    vec = _const_spec((1, WIDTH))
    return pl.pallas_call(
        kern,
        grid=(bt // nb, t // L),
        in_specs=[pl.BlockSpec((nb, L, RWKV_COLS), lambda b, n: (b, n, 0)), shift, state,
                  _const_spec((1, RWKV_COLS)), vec, _const_spec((64, WIDTH)), vec,
                  _const_spec((64, WIDTH)), _const_spec((128, WIDTH)), vec, vec, vec, vec, vec,
                  _const_spec((WIDTH, WIDTH))],
        out_specs=(pl.BlockSpec((nb, L, WIDTH), lambda b, n: (b, n, 0)), shift, state),
        out_shape=(jax.ShapeDtypeStruct((bt, t, WIDTH), F32),
                   jax.ShapeDtypeStruct((bt, 1, RWKV_COLS), F32),
                   jax.ShapeDtypeStruct((bt, RWKV_HEADS, RWKV_HEAD, RWKV_HEAD), F32)),
        scratch_shapes=[pltpu.VMEM((RWKV_HEADS, RWKV_HEAD, nb * RWKV_HEAD), F32),
                        pltpu.VMEM((nb, 1, RWKV_COLS), F32)],
        compiler_params=_params("parallel", "arbitrary"),
        name="rwkv",
    )(cols, shift0, s0, p["rwkv_mu"], p["rwkv_w0"], p["rwkv_w2"], p["rwkv_a0"], p["rwkv_a2"],
      p["rwkv_g2"], p["rwkv_k_k"], p["rwkv_k_a"], p["rwkv_r_k"], p["rwkv_ln_w"], p["rwkv_ln_b"],
      p["head_ones"])


SB_DEAD_LOG = -110.0
SB_LANES = 128


def _sb_block(qh, k, v, diag, suffix_ones, acc_scr, carry_scr):
    tq = qh[0].shape[0]
    tk = k.shape[1]
    kb = k.astype(BF16)
    z = jnp.concatenate(
        [jnp.dot(qh[h], kb[SB_HEAD * h:SB_HEAD * (h + 1)], preferred_element_type=F32)
         for h in range(SB_HEADS)], axis=0)
    lp = _log_sigmoid(z)
    lm = lp - z
    if diag:
        row = lax.broadcasted_iota(jnp.int32, z.shape, 0) & (tq - 1)
        mask = row > lax.broadcasted_iota(jnp.int32, z.shape, 1)
        lm = jnp.where(mask, lm, 0.0)
    sums = _mm_split_lhs(lm, suffix_ones)
    carry = carry_scr[...]
    wgt = jnp.exp(lp + carry[:, :tk] + sums[:, :tk])
    if diag:
        wgt = jnp.where(mask, wgt, 0.0)
    wgt = wgt.astype(BF16)
    vb = v.astype(BF16)
    for h in range(SB_HEADS):
        acc_scr[h] += lax.dot_general(wgt[tq * h:tq * (h + 1)], vb[SB_HEAD * h:SB_HEAD * (h + 1)],
                                      (((1,), (1,)), ((), ())), preferred_element_type=F32)
    carry = carry + sums[:, tk:]
    carry_scr[...] = carry
    return jnp.max(carry)


def _suffix_ones(tk):
    return jnp.concatenate([_tri(tk, strict=True).astype(BF16), jnp.ones((tk, SB_LANES), BF16)],
                           axis=1)


def _sb_heads(q):
    return [q[:, SB_HEAD * h:SB_HEAD * (h + 1)].astype(BF16) for h in range(SB_HEADS)]


def _sb_self_kernel(q_ref, k_ref, v_ref, o_ref, acc_scr, carry_scr, *, tq):
    i = pl.program_id(1)
    acc_scr[...] = jnp.zeros_like(acc_scr)
    carry_scr[...] = jnp.zeros_like(carry_scr)
    qh = _sb_heads(q_ref[...])
    ones = _suffix_ones(tq)

    def keys(j):
        return pl.ds(pl.multiple_of(j * tq, tq), tq)

    live = _sb_block(qh, k_ref[0, :, keys(i)], v_ref[0, :, keys(i)], True, ones, acc_scr, carry_scr)

    def body(state):
        j, _ = state
        m = _sb_block(qh, k_ref[0, :, keys(j)], v_ref[0, :, keys(j)], False, ones, acc_scr, carry_scr)
        return j - 1, m

    lax.while_loop(lambda s: jnp.logical_and(s[0] >= 0, s[1] > SB_DEAD_LOG), body, (i - 1, live))
    o_ref[...] = jnp.concatenate([acc_scr[h] for h in range(SB_HEADS)], axis=-1)


def _sb_self(q2d, k_t, v_t, tq):
    bt, _, t = k_t.shape
    nq = t // tq
    assert tq & (tq - 1) == 0
    kern = functools.partial(_sb_self_kernel, tq=tq)
    seq = pl.BlockSpec((1, WIDTH, t), lambda b, i: (b, 0, 0))
    blk = pl.BlockSpec((tq, WIDTH), lambda b, i: (b * nq + i, 0))
    return pl.pallas_call(
        kern,
        grid=(bt, nq),
        in_specs=[blk, seq, seq],
        out_specs=blk,
        out_shape=jax.ShapeDtypeStruct((bt * t, WIDTH), F32),
        scratch_shapes=[pltpu.VMEM((SB_HEADS, tq, SB_HEAD), F32),
                        pltpu.VMEM((SB_HEADS * tq, SB_LANES), F32)],
        compiler_params=_params("parallel", "arbitrary"),
        name="sb_self",
    )(q2d, k_t, v_t)


def _sb_past_kernel(q_ref, k_ref, v_ref, pk_ref, pv_ref, o_ref, acc_scr, carry_scr, *, t, tkp, sub):
    j = pl.program_id(1)
    qh = _sb_heads(q_ref[...])

    @pl.when(j == 0)
    def _():
        acc_scr[...] = jnp.zeros_like(acc_scr)
        carry_scr[...] = jnp.zeros_like(carry_scr)
        _sb_block(qh, k_ref[0], v_ref[0], True, _suffix_ones(t), acc_scr, carry_scr)

    @pl.when(j > 0)
    def _():
        ones = _suffix_ones(sub)

        def body(state):
            s, _ = state
            keys = pl.ds(pl.multiple_of(s * sub, sub), sub)
            m = _sb_block(qh, pk_ref[0, 0, :, keys], pv_ref[0, 0, :, keys], False, ones,
                          acc_scr, carry_scr)
            return s - 1, m

        lax.while_loop(lambda s: jnp.logical_and(s[0] >= 0, s[1] > SB_DEAD_LOG), body,
                       (tkp // sub - 1, jnp.max(carry_scr[...])))

    @pl.when(j == pl.num_programs(1) - 1)
    def _():
        o_ref[...] = jnp.concatenate([acc_scr[h] for h in range(SB_HEADS)], axis=-1)


def _sb_past(q2d, k_t, v_t, past_k_t, past_v_t, layer, tkp, sub):
    bt, _, t = k_t.shape
    np_ = past_k_t.shape[3] // tkp
    assert t & (t - 1) == 0
    kern = functools.partial(_sb_past_kernel, t=t, tkp=tkp, sub=sub)
    qblk = pl.BlockSpec((t, WIDTH), lambda b, j: (b, 0))
    new = pl.BlockSpec((1, WIDTH, t), lambda b, j: (b, 0, 0))
    past = pl.BlockSpec((1, 1, WIDTH, tkp), lambda b, j: (layer, b, 0, np_ - jnp.maximum(j, 1)))
    return pl.pallas_call(
        kern,
        grid=(bt, np_ + 1),
        in_specs=[qblk, new, new, past, past],
        out_specs=qblk,
        out_shape=jax.ShapeDtypeStruct((bt * t, WIDTH), F32),
        scratch_shapes=[pltpu.VMEM((SB_HEADS, t, SB_HEAD), F32),
                        pltpu.VMEM((SB_HEADS * t, SB_LANES), F32)],
        compiler_params=_params("parallel", "arbitrary"),
        name="sb_past",
    )(q2d, k_t, v_t, past_k_t, past_v_t)


def _merge_ffn_kernel(x_ref, oa_ref, ob_ref, oc_ref, od_ref, gate_ref, wb_ref, wout_ref, nffn_ref,
                      wg_ref, wu_ref, wd_ref, y_ref):
    merged = None
    for n, o_ref in enumerate((oa_ref, ob_ref, oc_ref, od_ref)):
        term = gate_ref[:, n * D_MODEL:(n + 1) * D_MODEL].astype(F32) * _mm(o_ref[...], wb_ref[n])
        merged = term if merged is None else merged + term
    x = x_ref[...] + _mm(merged, wout_ref[...])
    h = x * lax.rsqrt(jnp.mean(x * x, axis=-1, keepdims=True) + RMS_EPS) * nffn_ref[...]
    hb = h.astype(BF16)
    gate = jnp.dot(hb, wg_ref[...], preferred_element_type=F32)
    up = jnp.dot(hb, wu_ref[...], preferred_element_type=F32)
    y_ref[...] = x + _mm(gate * _sigmoid(gate) * up, wd_ref[...])


def _merge_ffn(x2d, oa, ob, oc, od, gates, p, tm, nt):
    n = x2d.shape[0]
    row = lambda width: pl.BlockSpec((tm, width), lambda i: (i, 0))
    oa_spec = row(WIDTH) if nt is None else _time_major_spec(tm, nt)
    return pl.pallas_call(
        _merge_ffn_kernel,
        grid=(n // tm,),
        in_specs=[row(D_MODEL), oa_spec, row(WIDTH), row(WIDTH), row(WIDTH), row(GATE_COLS),
                  _const_spec((N_BRANCH, WIDTH, D_MODEL)), _const_spec((D_MODEL, D_MODEL)),
                  _const_spec((1, D_MODEL)), _const_spec((D_MODEL, D_FF)),
                  _const_spec((D_MODEL, D_FF)), _const_spec((D_FF, D_MODEL))],
        out_specs=row(D_MODEL),
        out_shape=jax.ShapeDtypeStruct((n, D_MODEL), F32),
        compiler_params=_params("parallel"),
        name="merge_ffn",
    )(x2d, oa, ob, oc, od, gates, p["w_branch"], p["w_out"], p["norm_ffn"], p["w_ffn_gate"],
      p["w_ffn_up"], p["w_ffn_down"])


def _prep_layer(w):
    p = {}
    w_in = w["w_in"]
    o_gla = WIDTH
    o_alat = o_gla + 2 * GLA_QK + 2 * WIDTH
    o_rwkv = o_alat + GLA_RANK
    o_sb = o_rwkv + RWKV_COLS
    o_sbk = o_sb + WIDTH
    o_gate = o_sb + SB_COLS
    pad = jnp.zeros((D_MODEL, 128 - GLA_RANK), w_in.dtype)
    p["w_in"] = jnp.concatenate(
        [w_in[:, :o_rwkv], pad, w_in[:, o_rwkv:o_sbk], w_in[:, o_gate:]], axis=1).astype(BF16)
    p["w_kv_t"] = w_in[:, o_sbk:o_gate].T.astype(BF16)
    p["norm_mix"] = w["norm_mix"].reshape(1, D_MODEL)

    lam = lax.complex(w["s5_a_re"], w["s5_a_im"])
    dt = jnp.exp(w["s5_log_dt"])[:, None]
    a_bar = jnp.exp(lam * dt)
    b_bar = ((a_bar - 1.0) / lam)[..., None] * lax.complex(w["s5_b_re"], w["s5_b_im"])
    eye = jnp.eye(S5_GROUPS, dtype=F32)

    def in_map(m):
        return jnp.einsum("gpc,gh->gchp", m, eye).reshape(WIDTH, S5_FLAT)

    def out_map(m):
        return jnp.einsum("gcp,gh->gphc", m, eye).reshape(S5_FLAT, WIDTH)

    p["s5_bbig"] = jnp.concatenate([in_map(b_bar.real), in_map(b_bar.imag)], axis=1).astype(BF16)
    p["s5_cbig"] = jnp.concatenate([out_map(w["s5_c_re"]), out_map(-w["s5_c_im"])], axis=0).astype(BF16)
    p["s5_ar"] = a_bar.real.reshape(1, S5_FLAT)
    p["s5_ai"] = a_bar.imag.reshape(1, S5_FLAT)
    p["s5_d"] = w["s5_d"].reshape(1, WIDTH)
    p["s5_w_glu"] = w["s5_w_glu"].astype(BF16)
    p["s5_b_glu"] = w["s5_b_glu"].reshape(1, WIDTH)

    p["gla_w_alpha"] = jnp.concatenate(
        [w["gla_w_alpha"], jnp.zeros((128 - GLA_RANK, GLA_QK), F32)], axis=0).astype(BF16)
    p["gla_b_alpha"] = w["gla_b_alpha"].reshape(1, GLA_QK)
    p["gla_norm"] = w["gla_norm"].reshape(1, WIDTH)

    for name in ("rwkv_w0", "rwkv_a0", "rwkv_k_k", "rwkv_k_a", "rwkv_r_k", "rwkv_ln_w", "rwkv_ln_b"):
        p[name] = w[name].reshape(1, WIDTH)
    p["rwkv_mu"] = w["rwkv_mu"].reshape(1, RWKV_COLS)
    for name in ("rwkv_w2", "rwkv_a2", "rwkv_g2"):
        p[name] = w[name].astype(BF16)
    head = jnp.arange(WIDTH) // RWKV_HEAD
    p["head_ones"] = (head[:, None] == head[None, :]).astype(BF16)

    p["sb_q_norm"] = jnp.tile(w["sb_q_norm"] * (SB_HEAD ** -0.5), SB_HEADS).reshape(1, WIDTH)
    p["sb_k_norm"] = jnp.tile(w["sb_k_norm"], SB_HEADS).reshape(WIDTH, 1)

    p["w_branch"] = w["w_branch"].astype(BF16)
    p["w_out"] = w["w_out"].astype(BF16)
    p["norm_ffn"] = w["norm_ffn"].reshape(1, D_MODEL)
    for name in ("w_ffn_gate", "w_ffn_up", "w_ffn_down"):
        p[name] = w[name].astype(BF16)
    return p


def _pick(n, prefs):
    for c in prefs:
        if n % c == 0:
            return c
    return n


def _layer(x, past_k_t, past_v_t, layer, s5_0, gla_0, rwkv_0, shift_0, p):
    bt, t, _ = x.shape
    n = bt * t
    x2d = x.reshape(n, D_MODEL)
    tm = _pick(n, (256, 128, 64, 32, 16, 8))
    nt = t // tm if t % tm == 0 else None
    assert nt is not None or tm % t == 0
    u, c_gla, c_rwkv, q, k_t, v_t, gates = _inproj(x2d, p, bt, t, tm, nt)

    if nt is None:
        u = u.reshape(bt, t, WIDTH).transpose(1, 0, 2)
    x0 = jnp.concatenate([s5_0[..., 0].reshape(bt, S5_FLAT), s5_0[..., 1].reshape(bt, S5_FLAT)], axis=1)
    lc = _pick(t, (64, 32, 16, 8))
    o_a, xfin = _s5(u.reshape(n, WIDTH), x0, p, bt, lc)
    if nt is None:
        o_a = o_a.reshape(t, bt, WIDTH).transpose(1, 0, 2).reshape(n, WIDTH)
    else:
        o_a = o_a.reshape(t, bt * WIDTH)
    s5_new = jnp.stack([xfin[:, :S5_FLAT].reshape(bt, S5_GROUPS, S5_STATE),
                        xfin[:, S5_FLAT:].reshape(bt, S5_GROUPS, S5_STATE)], axis=-1)

    chunk = _pick(t, (64, 32, 16, 8))
    nseq = _pick(bt, (RWKV_ROWS // chunk, 4, 2, 1))
    o_b, gla_t = _gla(c_gla.reshape(bt, t, GLA_COLS_PAD), jnp.swapaxes(gla_0, 2, 3), p, nseq, chunk)
    gla_new = jnp.swapaxes(gla_t, 2, 3)

    o_c, shift_new, rwkv_new = _rwkv(c_rwkv.reshape(bt, t, RWKV_COLS),
                                     shift_0.reshape(bt, 1, RWKV_COLS), rwkv_0, p, nseq, chunk)

    if past_k_t is None:
        o_d = _sb_self(q, k_t, v_t, _pick(t, (128, 64, 32, 16, 8)))
    else:
        tkp = _pick(past_k_t.shape[3], (512, 256, 128, 64, 32, 16, 8))
        o_d = _sb_past(q, k_t, v_t, past_k_t, past_v_t, layer, tkp, min(tkp, SB_LANES))

    y = _merge_ffn(x2d, o_a, o_b.reshape(n, WIDTH), o_c.reshape(n, WIDTH), o_d, gates, p, tm, nt)

    def cache_rows(a_t):
        return a_t.reshape(bt, SB_HEADS, SB_HEAD, t).transpose(0, 3, 1, 2)

    states = (cache_rows(k_t), cache_rows(v_t), s5_new, gla_new, rwkv_new,
              shift_new.reshape(bt, RWKV_COLS))
    return y.reshape(bt, t, D_MODEL), states


def _keys_on_lanes(cache):
    d, b, pl_, _, _ = cache.shape
    return cache.transpose(0, 1, 3, 4, 2).reshape(d, b, WIDTH, pl_)


def _trunk(x, past_k, past_v, s5_0, gla_0, rwkv_0, shift_0, layers):
    past_k_t = None if past_k is None else _keys_on_lanes(past_k)
    past_v_t = None if past_v is None else _keys_on_lanes(past_v)
    per_layer = []
    for l, p in enumerate(layers):
        x, st = _layer(x, past_k_t, past_v_t, l, s5_0[l], gla_0[l], rwkv_0[l], shift_0[l], p)
        per_layer.append(st)
    return x, [jnp.stack([st[i] for st in per_layer]) for i in range(6)]


def kernel(x_prompt, x_sample, cache_sb_k, cache_sb_v, state_s5, state_gla, state_rwkv, state_rwkv_shift, norm_mix, w_in, s5_a_re, s5_a_im, s5_log_dt, s5_b_re, s5_b_im, s5_c_re, s5_c_im, s5_d, s5_w_glu, s5_b_glu, gla_w_alpha, gla_b_alpha, gla_norm, rwkv_mu, rwkv_w0, rwkv_w2, rwkv_a0, rwkv_a2, rwkv_g2, rwkv_k_k, rwkv_k_a, rwkv_r_k, rwkv_ln_w, rwkv_ln_b, sb_q_norm, sb_k_norm, w_branch, w_out, norm_ffn, w_ffn_gate, w_ffn_up, w_ffn_down):
    weights = dict(norm_mix=norm_mix, w_in=w_in, s5_a_re=s5_a_re, s5_a_im=s5_a_im, s5_log_dt=s5_log_dt,
                   s5_b_re=s5_b_re, s5_b_im=s5_b_im, s5_c_re=s5_c_re, s5_c_im=s5_c_im, s5_d=s5_d,
                   s5_w_glu=s5_w_glu, s5_b_glu=s5_b_glu, gla_w_alpha=gla_w_alpha, gla_b_alpha=gla_b_alpha,
                   gla_norm=gla_norm, rwkv_mu=rwkv_mu, rwkv_w0=rwkv_w0, rwkv_w2=rwkv_w2, rwkv_a0=rwkv_a0,
                   rwkv_a2=rwkv_a2, rwkv_g2=rwkv_g2, rwkv_k_k=rwkv_k_k, rwkv_k_a=rwkv_k_a, rwkv_r_k=rwkv_r_k,
                   rwkv_ln_w=rwkv_ln_w, rwkv_ln_b=rwkv_ln_b, sb_q_norm=sb_q_norm, sb_k_norm=sb_k_norm,
                   w_branch=w_branch, w_out=w_out, norm_ffn=norm_ffn, w_ffn_gate=w_ffn_gate,
                   w_ffn_up=w_ffn_up, w_ffn_down=w_ffn_down)
    depth = w_in.shape[0]
    layers = [_prep_layer({name: arr[l] for name, arr in weights.items()}) for l in range(depth)]

    bp = x_prompt.shape[0]
    y_prompt, p_states = _trunk(
        x_prompt, None, None,
        jnp.zeros((depth, bp, S5_GROUPS, S5_STATE, 2), F32),
        jnp.zeros((depth, bp, GLA_HEADS, GLA_DK, GLA_DV), F32),
        jnp.zeros((depth, bp, RWKV_HEADS, RWKV_HEAD, RWKV_HEAD), F32),
        jnp.zeros((depth, bp, RWKV_COLS), F32), layers)
    y_sample, s_states = _trunk(x_sample, cache_sb_k, cache_sb_v, state_s5, state_gla, state_rwkv,
                                state_rwkv_shift, layers)
    return (y_prompt, y_sample, *p_states, *s_states)
```

```python
import functools
import math

import jax
import jax.numpy as jnp
from jax import lax
from jax.experimental import pallas as pl
from jax.experimental.pallas import tpu as pltpu

F32 = jnp.float32
BF16 = jnp.bfloat16

D_MODEL = 1024
WIDTH = 256
N_BRANCH = 4
S5_GROUPS, S5_GROUP, S5_STATE = 16, 16, 64
S5_FLAT = S5_GROUPS * S5_STATE
GLA_HEADS, GLA_DK, GLA_DV, GLA_RANK, GLA_TAU = 4, 32, 64, 16, 16.0
GLA_QK = GLA_HEADS * GLA_DK
GLA_COLS_PAD = 2 * GLA_QK + 2 * WIDTH + 128
RWKV_HEADS, RWKV_HEAD = 4, 64
RWKV_COLS = 1024
RWKV_GN_EPS = 64e-5
RWKV_ROWS = 256
SB_HEADS, SB_HEAD = 4, 64
SB_COLS = 3 * WIDTH
GATE_COLS = N_BRANCH * D_MODEL
D_FF = 2816
RMS_EPS = 1e-6

_C_S5 = 0
_C_GLA = _C_S5 + WIDTH
_C_RWKV = _C_GLA + GLA_COLS_PAD
_C_SBQ = _C_RWKV + RWKV_COLS
_C_GATE = _C_SBQ + WIDTH
_C_END = _C_GATE + GATE_COLS

VMEM_LIMIT = 56 * 1024 * 1024


def _params(*sem):
    return pltpu.CompilerParams(dimension_semantics=sem, vmem_limit_bytes=VMEM_LIMIT)


def _const_spec(shape):
    nd = len(shape)
    return pl.BlockSpec(shape, lambda *_: (0,) * nd, pipeline_mode=pl.Buffered(1))


def _mm(a, b):
    return jnp.dot(a.astype(BF16), b.astype(BF16), preferred_element_type=F32)


def _mm_nt(a, b):
    return lax.dot_general(a.astype(BF16), b.astype(BF16), (((1,), (1,)), ((), ())),
                           preferred_element_type=F32)


def _mm_tn(a, b):
    return lax.dot_general(a.astype(BF16), b.astype(BF16), (((0,), (0,)), ((), ())),
                           preferred_element_type=F32)


def _split(a):
    hi = a.astype(BF16)
    lo = (a - hi.astype(F32)).astype(BF16)
    return hi, lo


def _mm_split_lhs(a, b01):
    hi, lo = _split(a)
    return (jnp.dot(hi, b01, preferred_element_type=F32)
            + jnp.dot(lo, b01, preferred_element_type=F32))


def _mm_split_rhs(a01, b):
    hi, lo = _split(b)
    return (jnp.dot(a01, hi, preferred_element_type=F32)
            + jnp.dot(a01, lo, preferred_element_type=F32))


def _log_sigmoid(z):
    return jnp.minimum(z, 0.0) - jnp.log1p(jnp.exp(-jnp.abs(z)))


def _sigmoid(z):
    return 1.0 / (1.0 + jnp.exp(-z))


def _tri(n, strict=False):
    r = lax.broadcasted_iota(jnp.int32, (n, n), 0)
    c = lax.broadcasted_iota(jnp.int32, (n, n), 1)
    return (r > c) if strict else (r >= c)


def _inproj_kernel(x_ref, g_ref, w_ref, wkv_ref, qn_ref, kn_ref, ones_ref,
                   s5_ref, gla_ref, rwkv_ref, q_ref, kt_ref, vt_ref, gate_ref, *, nseq):
    x = x_ref[...]
    h = x * lax.rsqrt(jnp.mean(x * x, axis=-1, keepdims=True) + RMS_EPS) * g_ref[...]
    hb = h.astype(BF16)

    def mm(lo, hi):
        return jnp.dot(hb, w_ref[:, lo:hi], preferred_element_type=F32)

    s5_ref[...] = mm(_C_S5, _C_GLA)
    gla_ref[...] = mm(_C_GLA, _C_RWKV)
    rwkv_ref[...] = mm(_C_RWKV, _C_SBQ)
    for n in range(N_BRANCH):
        lo = _C_GATE + n * D_MODEL
        gate_ref[:, n * D_MODEL:(n + 1) * D_MODEL] = _sigmoid(mm(lo, lo + D_MODEL)).astype(BF16)

    q = mm(_C_SBQ, _C_GATE)
    q_ms = _mm_split_lhs(q * q, ones_ref[...]) * (1.0 / SB_HEAD)
    q_ref[...] = (q * lax.rsqrt(q_ms + RMS_EPS) * qn_ref[...]).astype(BF16)

    kv_t = lax.dot_general(wkv_ref[...], hb, (((1,), (1,)), ((), ())), preferred_element_type=F32)
    k_t = []
    for hd in range(SB_HEADS):
        kh = kv_t[SB_HEAD * hd:SB_HEAD * (hd + 1)]
        k_t.append(kh * lax.rsqrt(jnp.mean(kh * kh, axis=0, keepdims=True) + RMS_EPS))
    k_t = jnp.concatenate(k_t, axis=0) * kn_ref[...]
    v_t = kv_t[WIDTH:2 * WIDTH]
    t = kt_ref.shape[2]
    for s in range(nseq):
        kt_ref[s] = k_t[:, t * s:t * (s + 1)]
        vt_ref[s] = v_t[:, t * s:t * (s + 1)]


def _time_major_spec(tm, nt):
    return pl.BlockSpec((tm, WIDTH), lambda i: (i % nt, i // nt))


def _inproj(x2d, p, bt, t, tm, nt):
    n = x2d.shape[0]
    row = lambda width: pl.BlockSpec((tm, width), lambda i: (i, 0))
    if nt is None:
        nseq = tm // t
        s5_spec, s5_shape = row(WIDTH), (n, WIDTH)
        kv_spec = pl.BlockSpec((nseq, WIDTH, t), lambda i: (i, 0, 0))
    else:
        nseq = 1
        s5_spec, s5_shape = _time_major_spec(tm, nt), (t, bt * WIDTH)
        kv_spec = pl.BlockSpec((1, WIDTH, tm), lambda i: (i // nt, 0, i % nt))
    kv_shape = jax.ShapeDtypeStruct((bt, WIDTH, t), F32)
    return pl.pallas_call(
        functools.partial(_inproj_kernel, nseq=nseq),
        grid=(n // tm,),
        in_specs=[row(D_MODEL), _const_spec((1, D_MODEL)), _const_spec((D_MODEL, _C_END)),
                  _const_spec((2 * WIDTH, D_MODEL)), _const_spec((1, WIDTH)),
                  _const_spec((WIDTH, 1)), _const_spec((WIDTH, WIDTH))],
        out_specs=(s5_spec, row(GLA_COLS_PAD), row(RWKV_COLS), row(WIDTH), kv_spec, kv_spec,
                   row(GATE_COLS)),
        out_shape=(jax.ShapeDtypeStruct(s5_shape, F32),
                   jax.ShapeDtypeStruct((n, GLA_COLS_PAD), F32),
                   jax.ShapeDtypeStruct((n, RWKV_COLS), F32),
                   jax.ShapeDtypeStruct((n, WIDTH), BF16), kv_shape, kv_shape,
                   jax.ShapeDtypeStruct((n, GATE_COLS), BF16)),
        compiler_params=_params("parallel"),
        name="inproj",
    )(x2d, p["norm_mix"], p["w_in"], p["w_kv_t"], p["sb_q_norm"], p["sb_k_norm"], p["head_ones"])


def _s5_kernel(u_ref, x0_ref, bbig_ref, cbig_ref, ar_ref, ai_ref, d_ref, wglu_ref, bglu_ref,
               o_ref, xfin_ref, bu_scr, x_scr, *, nb, lc):
    @pl.when(pl.program_id(0) == 0)
    def _():
        x_scr[...] = x0_ref[...]

    u = u_ref[...]
    bu_scr[...] = _mm(u, bbig_ref[...])
    ar = jnp.broadcast_to(ar_ref[...], (nb, S5_FLAT))
    ai = jnp.broadcast_to(ai_ref[...], (nb, S5_FLAT))

    def step(t, carry):
        xr, xi = carry
        rows = pl.ds(pl.multiple_of(t * nb, nb), nb)
        nxr = ar * xr - ai * xi + bu_scr[rows, 0:S5_FLAT]
        nxi = ar * xi + ai * xr + bu_scr[rows, S5_FLAT:2 * S5_FLAT]
        bu_scr[rows, 0:S5_FLAT] = nxr
        bu_scr[rows, S5_FLAT:2 * S5_FLAT] = nxi
        return nxr, nxi

    xr, xi = lax.fori_loop(0, lc, step, (x_scr[:, 0:S5_FLAT], x_scr[:, S5_FLAT:2 * S5_FLAT]))
    x_scr[:, 0:S5_FLAT] = xr
    x_scr[:, S5_FLAT:2 * S5_FLAT] = xi
    xfin_ref[...] = x_scr[...]

    y = _mm(bu_scr[...], cbig_ref[...]) + d_ref[...] * u
    y = jax.nn.gelu(y)
    o_ref[...] = y * _sigmoid(_mm(y, wglu_ref[...]) + bglu_ref[...])


def _s5(u_tm, x0, p, nb, lc):
    rows = u_tm.shape[0]
    blk = lc * nb
    kern = functools.partial(_s5_kernel, nb=nb, lc=lc)
    return pl.pallas_call(
        kern,
        grid=(rows // blk,),
        in_specs=[pl.BlockSpec((blk, WIDTH), lambda c: (c, 0)),
                  _const_spec((nb, 2 * S5_FLAT)),
                  _const_spec((WIDTH, 2 * S5_FLAT)), _const_spec((2 * S5_FLAT, WIDTH)),
                  _const_spec((1, S5_FLAT)), _const_spec((1, S5_FLAT)), _const_spec((1, WIDTH)),
                  _const_spec((WIDTH, WIDTH)), _const_spec((1, WIDTH))],
        out_specs=(pl.BlockSpec((blk, WIDTH), lambda c: (c, 0)),
                   pl.BlockSpec((nb, 2 * S5_FLAT), lambda c: (0, 0))),
        out_shape=(jax.ShapeDtypeStruct((rows, WIDTH), F32),
                   jax.ShapeDtypeStruct((nb, 2 * S5_FLAT), F32)),
        scratch_shapes=[pltpu.VMEM((blk, 2 * S5_FLAT), F32), pltpu.VMEM((nb, 2 * S5_FLAT), F32)],
        compiler_params=_params("arbitrary"),
        name="s5",
    )(u_tm, x0, p["s5_bbig"], p["s5_cbig"], p["s5_ar"], p["s5_ai"], p["s5_d"], p["s5_w_glu"],
      p["s5_b_glu"])


def _gla_kernel(z_ref, s0_ref, walpha_ref, balpha_ref, gnorm_ref, o_ref, sfin_ref, s_scr, *, nb, L):
    R = nb * L
    DK = GLA_DK

    @pl.when(pl.program_id(1) == 0)
    def _():
        for i in range(nb):
            for h in range(GLA_HEADS):
                s_scr[h, :, DK * i:DK * (i + 1)] = s0_ref[i, h]

    z = z_ref[...].reshape(R, GLA_COLS_PAD)
    q = z[:, 0:GLA_QK] * (GLA_DK ** -0.5)
    k = z[:, GLA_QK:2 * GLA_QK]
    v = z[:, 2 * GLA_QK:2 * GLA_QK + WIDTH]
    g = z[:, 2 * GLA_QK + WIDTH:2 * GLA_QK + 2 * WIDTH]
    alat = z[:, 2 * GLA_QK + 2 * WIDTH:]
    log_a = _log_sigmoid(_mm(alat, walpha_ref[...]) + balpha_ref[...]) * (1.0 / GLA_TAU)
    rr = lax.broadcasted_iota(jnp.int32, (R, R), 0)
    cc = lax.broadcasted_iota(jnp.int32, (R, R), 1)
    tri = jnp.logical_and((rr & -L) == (cc & -L), rr >= cc)
    own = ((lax.broadcasted_iota(jnp.int32, (R, nb * DK), 0) & -L) * DK
           == (lax.broadcasted_iota(jnp.int32, (R, nb * DK), 1) & -DK) * L)
    b = _mm_split_rhs(tri.astype(BF16), log_a)
    b_last = [b[L * (i + 1) - 1:L * (i + 1), :] for i in range(nb)]
    b_end = jnp.concatenate([jnp.broadcast_to(be, (L, GLA_QK)) for be in b_last], axis=0)
    qt = q * jnp.exp(b)
    kt = k * jnp.exp(-b)
    kd = k * jnp.exp(b_end - b)

    def spread(x):
        return jnp.where(own, jnp.concatenate([x] * nb, axis=1), 0.0).astype(BF16)

    heads = range(GLA_HEADS)
    ksl = [slice(DK * h, DK * (h + 1)) for h in heads]
    vsl = [slice(GLA_DV * h, GLA_DV * (h + 1)) for h in heads]
    st = [s_scr[h] for h in heads]
    attn = [jnp.where(tri, _mm_nt(qt[:, ks], kt[:, ks]), 0.0) for ks in ksl]
    from_s = [_mm_nt(spread(qt[:, ks]), st[h]) for h, ks in zip(heads, ksl)]
    o_h = [_mm(attn[h], v[:, vsl[h]]) + from_s[h] for h in heads]
    for h, ks in zip(heads, ksl):
        e_all = jnp.concatenate([jnp.exp(be[:, ks]) for be in b_last], axis=1)
        s_scr[h] = st[h] * e_all + _mm_tn(v[:, vsl[h]], spread(kd[:, ks]))
    outs = [o_h[h] * lax.rsqrt(jnp.mean(o_h[h] * o_h[h], axis=-1, keepdims=True) + RMS_EPS)
            * gnorm_ref[:, vsl[h]] for h in heads]
    o_ref[...] = (jnp.concatenate(outs, axis=-1) * (g * _sigmoid(g))).reshape(nb, L, WIDTH)

    @pl.when(pl.program_id(1) == pl.num_programs(1) - 1)
    def _():
        for i in range(nb):
            for h in range(GLA_HEADS):
                sfin_ref[i, h] = s_scr[h, :, DK * i:DK * (i + 1)]


def _gla(cols, s0t, p, nb, L):
    bt, t, _ = cols.shape
    assert bt % nb == 0 and t % L == 0 and L & (L - 1) == 0
    kern = functools.partial(_gla_kernel, nb=nb, L=L)
    state = pl.BlockSpec((nb, GLA_HEADS, GLA_DV, GLA_DK), lambda b, n: (b, 0, 0, 0))
    return pl.pallas_call(
        kern,
        grid=(bt // nb, t // L),
        in_specs=[pl.BlockSpec((nb, L, GLA_COLS_PAD), lambda b, n: (b, n, 0)), state,
                  _const_spec((128, GLA_QK)), _const_spec((1, GLA_QK)), _const_spec((1, WIDTH))],
        out_specs=(pl.BlockSpec((nb, L, WIDTH), lambda b, n: (b, n, 0)), state),
        out_shape=(jax.ShapeDtypeStruct((bt, t, WIDTH), F32),
                   jax.ShapeDtypeStruct((bt, GLA_HEADS, GLA_DV, GLA_DK), F32)),
        scratch_shapes=[pltpu.VMEM((GLA_HEADS, GLA_DV, nb * GLA_DK), F32)],
        compiler_params=_params("parallel", "arbitrary"),
        name="gla",
    )(cols, s0t, p["gla_w_alpha"], p["gla_b_alpha"], p["gla_norm"])


def _rwkv_kernel(z_ref, shift0_ref, s0_ref, mu_ref, w0_ref, w2_ref, a0_ref, a2_ref, g2_ref, kk_ref,
                 ka_ref, rk_ref, lnw_ref, lnb_ref, bd_ref, o_ref, shiftfin_ref, sfin_ref,
                 s_scr, prev_scr, *, nb, L):
    R = nb * L
    HD = RWKV_HEAD

    @pl.when(pl.program_id(1) == 0)
    def _():
        for i in range(nb):
            for h in range(RWKV_HEADS):
                s_scr[h, :, HD * i:HD * (i + 1)] = s0_ref[i, h]
        prev_scr[...] = shift0_ref[...]

    z = z_ref[...].reshape(R, RWKV_COLS)
    first = (lax.broadcasted_iota(jnp.int32, (R, RWKV_COLS), 0) & (L - 1)) == 0
    carried = jnp.concatenate(
        [jnp.broadcast_to(prev_scr[i], (L, RWKV_COLS)) for i in range(nb)], axis=0)
    prev = jnp.where(first, carried, pltpu.roll(z, 1, axis=0))
    for i in range(nb):
        last = z[L * (i + 1) - 1:L * (i + 1), :]
        prev_scr[i] = last
        shiftfin_ref[i] = last
    zs = z + (prev - z) * mu_ref[...]
    r = zs[:, 0:256]
    k = zs[:, 256:512]
    v = zs[:, 512:768]
    w_lat = zs[:, 768:832]
    a_lat = zs[:, 832:896]
    g_lat = zs[:, 896:1024]
    w = -jax.nn.softplus(-(w0_ref[...] + _mm(jnp.tanh(w_lat), w2_ref[...]))) - 0.5
    log_w = -jnp.exp(w)
    a = _sigmoid(a0_ref[...] + _mm(a_lat, a2_ref[...]))
    g = _mm(_sigmoid(g_lat), g2_ref[...])
    kk = k * kk_ref[...]
    kk = kk / jnp.maximum(jnp.sqrt(_mm_split_lhs(kk * kk, bd_ref[...])), 1e-12)
    k = k * (1.0 + (a - 1.0) * ka_ref[...])
    vec_a = -kk
    vec_b = kk * a

    rr = lax.broadcasted_iota(jnp.int32, (R, R), 0)
    cc = lax.broadcasted_iota(jnp.int32, (R, R), 1)
    same = (rr & -L) == (cc & -L)
    tri = jnp.logical_and(same, rr >= cc)
    stri = jnp.logical_and(same, rr > cc)
    own = ((lax.broadcasted_iota(jnp.int32, (R, nb * HD), 0) & -L) * HD
           == (lax.broadcasted_iota(jnp.int32, (R, nb * HD), 1) & -HD) * L)

    cum = _mm_split_rhs(tri.astype(BF16), log_w)
    p_t = jnp.exp(cum)
    inv_p = jnp.exp(-cum)
    p_last = [p_t[L * (i + 1) - 1:L * (i + 1), :] for i in range(nb)]
    p_end = jnp.concatenate([jnp.broadcast_to(pe, (L, WIDTH)) for pe in p_last], axis=0)
    a_til = vec_a * jnp.exp(cum - log_w)
    b_til = vec_b * inv_p
    k_til = k * inv_p
    r_til = r * p_t
    b_end = b_til * p_end
    k_end = k_til * p_end

    def spread(x):
        return jnp.where(own, jnp.concatenate([x] * nb, axis=1), 0.0).astype(BF16)

    heads = range(RWKV_HEADS)
    hsl = [slice(HD * h, HD * (h + 1)) for h in heads]
    f32dot = functools.partial(jnp.dot, preferred_element_type=F32)
    vb = [v[:, hs].astype(BF16) for hs in hsl]
    gram = [_mm_nt(jnp.concatenate([a_til[:, hs], r_til[:, hs]], axis=0),
                   jnp.concatenate([b_til[:, hs], k_til[:, hs]], axis=0)) for hs in hsl]
    from_s = [_mm_nt(jnp.concatenate([spread(a_til[:, hs]), spread(r_til[:, hs])], axis=0), s_scr[h])
              for h, hs in zip(heads, hsl)]
    pw = [jnp.where(stri, gm[:R, :R], 0.0).astype(BF16) for gm in gram]
    a_ak = [jnp.where(stri, gm[:R, R:], 0.0).astype(BF16) for gm in gram]
    a_r = [jnp.concatenate([jnp.where(tri, gm[R:, :R], 0.0).astype(BF16),
                            jnp.where(tri, gm[R:, R:], 0.0).astype(BF16)], axis=1) for gm in gram]
    sa = [from_s[h][:R] + f32dot(a_ak[h], vb[h]) for h in heads]
    sa = [sa[h] + f32dot(pw[h], sa[h].astype(BF16)) for h in heads]
    for _ in range(int(math.log2(L)) - 1):
        pw = [f32dot(pw[h], pw[h]).astype(BF16) for h in heads]
        sa = [sa[h] + f32dot(pw[h], sa[h].astype(BF16)) for h in heads]
    sa_v = [jnp.concatenate([sa[h].astype(BF16), vb[h]], axis=0) for h in heads]
    y = [from_s[h][R:] + f32dot(a_r[h], sa_v[h]) for h in heads]
    for h, hs in zip(heads, hsl):
        p_all = jnp.concatenate([pe[:, hs] for pe in p_last], axis=1)
        s_scr[h] = s_scr[h] * p_all + _mm_tn(
            sa_v[h], jnp.concatenate([spread(b_end[:, hs]), spread(k_end[:, hs])], axis=0))
    outs = []
    for h, hs in zip(heads, hsl):
        mean = jnp.mean(y[h], axis=-1, keepdims=True)
        var = jnp.mean(jnp.square(y[h] - mean), axis=-1, keepdims=True)
        yn = (y[h] - mean) * lax.rsqrt(var + RWKV_GN_EPS) * lnw_ref[:, hs] + lnb_ref[:, hs]
        bonus = jnp.sum(r[:, hs] * k[:, hs] * rk_ref[:, hs], axis=-1, keepdims=True)
        outs.append(yn + bonus * v[:, hs])
    o_ref[...] = (jnp.concatenate(outs, axis=-1) * g).reshape(nb, L, WIDTH)

    @pl.when(pl.program_id(1) == pl.num_programs(1) - 1)
    def _():
        for i in range(nb):
            for h in range(RWKV_HEADS):
                sfin_ref[i, h] = s_scr[h, :, HD * i:HD * (i + 1)]


def _rwkv(cols, shift0, s0, p, nb, L):
    bt, t, _ = cols.shape
    assert bt % nb == 0 and t % L == 0 and L & (L - 1) == 0
    kern = functools.partial(_rwkv_kernel, nb=nb, L=L)
    state = pl.BlockSpec((nb, RWKV_HEADS, RWKV_HEAD, RWKV_HEAD), lambda b, n: (b, 0, 0, 0))
    shift = pl.BlockSpec((nb, 1, RWKV_COLS), lambda b, n: (b, 0, 0))
    vec = _const_spec((1, WIDTH))
    return pl.pallas_call(
        kern,
        grid=(bt // nb, t // L),
        in_specs=[pl.BlockSpec((nb, L, RWKV_COLS), lambda b, n: (b, n, 0)), shift, state,
                  _const_spec((1, RWKV_COLS)), vec, _const_spec((64, WIDTH)), vec,
                  _const_spec((64, WIDTH)), _const_spec((128, WIDTH)), vec, vec, vec, vec, vec,
                  _const_spec((WIDTH, WIDTH))],
        out_specs=(pl.BlockSpec((nb, L, WIDTH), lambda b, n: (b, n, 0)), shift, state),
        out_shape=(jax.ShapeDtypeStruct((bt, t, WIDTH), F32),
                   jax.ShapeDtypeStruct((bt, 1, RWKV_COLS), F32),
                   jax.ShapeDtypeStruct((bt, RWKV_HEADS, RWKV_HEAD, RWKV_HEAD), F32)),
        scratch_shapes=[pltpu.VMEM((RWKV_HEADS, RWKV_HEAD, nb * RWKV_HEAD), F32),
                        pltpu.VMEM((nb, 1, RWKV_COLS), F32)],
        compiler_params=_params("parallel", "arbitrary"),
        name="rwkv",
    )(cols, shift0, s0, p["rwkv_mu"], p["rwkv_w0"], p["rwkv_w2"], p["rwkv_a0"], p["rwkv_a2"],
      p["rwkv_g2"], p["rwkv_k_k"], p["rwkv_k_a"], p["rwkv_r_k"], p["rwkv_ln_w"], p["rwkv_ln_b"],
      p["head_ones"])


SB_DEAD_LOG = -110.0
SB_LANES = 128


def _sb_block(qh, k, v, diag, suffix_ones, acc_scr, carry_scr):
    tq = qh[0].shape[0]
    tk = k.shape[1]
    kb = k.astype(BF16)
    z = jnp.concatenate(
        [jnp.dot(qh[h], kb[SB_HEAD * h:SB_HEAD * (h + 1)], preferred_element_type=F32)
         for h in range(SB_HEADS)], axis=0)
    lp = _log_sigmoid(z)
    lm = lp - z
    if diag:
        row = lax.broadcasted_iota(jnp.int32, z.shape, 0) & (tq - 1)
        mask = row > lax.broadcasted_iota(jnp.int32, z.shape, 1)
        lm = jnp.where(mask, lm, 0.0)
    sums = _mm_split_lhs(lm, suffix_ones)
    carry = carry_scr[...]
    if tk <= SB_LANES:
        carry_keys = carry[:, :tk]
    else:
        carry_keys = jnp.concatenate([carry] * (tk // SB_LANES), axis=1)
    wgt = jnp.exp(lp + carry_keys + sums[:, :tk])
    if diag:
        wgt = jnp.where(mask, wgt, 0.0)
    wgt = wgt.astype(BF16)
    vb = v.astype(BF16)
    for h in range(SB_HEADS):
        acc_scr[h] += lax.dot_general(wgt[tq * h:tq * (h + 1)], vb[SB_HEAD * h:SB_HEAD * (h + 1)],
                                      (((1,), (1,)), ((), ())), preferred_element_type=F32)
    carry = carry + sums[:, tk:]
    carry_scr[...] = carry
    return jnp.max(carry)


def _suffix_ones(tk):
    return jnp.concatenate([_tri(tk, strict=True).astype(BF16), jnp.ones((tk, SB_LANES), BF16)],
                           axis=1)


def _sb_heads(q):
    return [q[:, SB_HEAD * h:SB_HEAD * (h + 1)].astype(BF16) for h in range(SB_HEADS)]


def _sb_self_kernel(q_ref, k_ref, v_ref, o_ref, acc_scr, carry_scr, *, tq):
    i = pl.program_id(1)
    acc_scr[...] = jnp.zeros_like(acc_scr)
    carry_scr[...] = jnp.zeros_like(carry_scr)
    qh = _sb_heads(q_ref[...])
    ones = _suffix_ones(tq)

    def keys(j):
        return pl.ds(pl.multiple_of(j * tq, tq), tq)

    live = _sb_block(qh, k_ref[0, :, keys(i)], v_ref[0, :, keys(i)], True, ones, acc_scr, carry_scr)

    def body(state):
        j, _ = state
        m = _sb_block(qh, k_ref[0, :, keys(j)], v_ref[0, :, keys(j)], False, ones, acc_scr, carry_scr)
        return j - 1, m

    lax.while_loop(lambda s: jnp.logical_and(s[0] >= 0, s[1] > SB_DEAD_LOG), body, (i - 1, live))
    o_ref[...] = jnp.concatenate([acc_scr[h] for h in range(SB_HEADS)], axis=-1)


def _sb_self(q2d, k_t, v_t, tq):
    bt, _, t = k_t.shape
    nq = t // tq
    assert tq & (tq - 1) == 0
    kern = functools.partial(_sb_self_kernel, tq=tq)
    seq = pl.BlockSpec((1, WIDTH, t), lambda b, i: (b, 0, 0))
    blk = pl.BlockSpec((tq, WIDTH), lambda b, i: (b * nq + i, 0))
    return pl.pallas_call(
        kern,
        grid=(bt, nq),
        in_specs=[blk, seq, seq],
        out_specs=blk,
        out_shape=jax.ShapeDtypeStruct((bt * t, WIDTH), F32),
        scratch_shapes=[pltpu.VMEM((SB_HEADS, tq, SB_HEAD), F32),
                        pltpu.VMEM((SB_HEADS * tq, SB_LANES), F32)],
        compiler_params=_params("parallel", "arbitrary"),
        name="sb_self",
    )(q2d, k_t, v_t)


SB_OLDER_SLOT = 2


def _sb_past_kernel(q_ref, k_ref, v_ref, pk_hbm, pv_hbm, o_ref, kbuf, vbuf, sem, acc_scr, carry_scr,
                    *, layer, t, tkp, np_):
    b = pl.program_id(0)
    slot = b % 2
    qh = _sb_heads(q_ref[...])

    def fetch(seq, blk, dst):
        keys = pl.ds(pl.multiple_of(blk * tkp, tkp), tkp)
        return (pltpu.make_async_copy(pk_hbm.at[layer, seq, :, keys], kbuf.at[dst], sem.at[0, dst]),
                pltpu.make_async_copy(pv_hbm.at[layer, seq, :, keys], vbuf.at[dst], sem.at[1, dst]))

    @pl.when(b == 0)
    def _():
        for cp in fetch(0, np_ - 1, 0):
            cp.start()

    @pl.when(b + 1 < pl.num_programs(0))
    def _():
        for cp in fetch(b + 1, np_ - 1, 1 - slot):
            cp.start()

    acc_scr[...] = jnp.zeros_like(acc_scr)
    carry_scr[...] = jnp.zeros_like(carry_scr)
    _sb_block(qh, k_ref[0], v_ref[0], True, _suffix_ones(t), acc_scr, carry_scr)

    ones = _suffix_ones(tkp)
    for cp in fetch(b, np_ - 1, slot):
        cp.wait()
    live = _sb_block(qh, kbuf[slot], vbuf[slot], False, ones, acc_scr, carry_scr)

    def body(state):
        j, _ = state
        copies = fetch(b, j, SB_OLDER_SLOT)
        for cp in copies:
            cp.start()
        for cp in copies:
            cp.wait()
        m = _sb_block(qh, kbuf[SB_OLDER_SLOT], vbuf[SB_OLDER_SLOT], False, ones, acc_scr, carry_scr)
        return j - 1, m

    lax.while_loop(lambda s: jnp.logical_and(s[0] >= 0, s[1] > SB_DEAD_LOG), body, (np_ - 2, live))
    o_ref[...] = jnp.concatenate([acc_scr[h] for h in range(SB_HEADS)], axis=-1)


def _sb_past(q2d, k_t, v_t, past_k_t, past_v_t, layer, tkp):
    bt, _, t = k_t.shape
    np_ = past_k_t.shape[3] // tkp
    assert t & (t - 1) == 0 and past_k_t.shape[3] % tkp == 0
    kern = functools.partial(_sb_past_kernel, layer=layer, t=t, tkp=tkp, np_=np_)
    qblk = pl.BlockSpec((t, WIDTH), lambda b: (b, 0))
    new = pl.BlockSpec((1, WIDTH, t), lambda b: (b, 0, 0))
    hbm = pl.BlockSpec(memory_space=pl.ANY)
    return pl.pallas_call(
        kern,
        grid=(bt,),
        in_specs=[qblk, new, new, hbm, hbm],
        out_specs=qblk,
        out_shape=jax.ShapeDtypeStruct((bt * t, WIDTH), F32),
        scratch_shapes=[pltpu.VMEM((3, WIDTH, tkp), F32), pltpu.VMEM((3, WIDTH, tkp), F32),
                        pltpu.SemaphoreType.DMA((2, 3)),
                        pltpu.VMEM((SB_HEADS, t, SB_HEAD), F32),
                        pltpu.VMEM((SB_HEADS * t, SB_LANES), F32)],
        compiler_params=_params("arbitrary"),
        name="sb_past",
    )(q2d, k_t, v_t, past_k_t, past_v_t)


def _merge_ffn_kernel(x_ref, oa_ref, ob_ref, oc_ref, od_ref, gate_ref, wb_ref, wout_ref, nffn_ref,
                      wg_ref, wu_ref, wd_ref, y_ref):
    merged = None
    for n, o_ref in enumerate((oa_ref, ob_ref, oc_ref, od_ref)):
        term = gate_ref[:, n * D_MODEL:(n + 1) * D_MODEL].astype(F32) * _mm(o_ref[...], wb_ref[n])
        merged = term if merged is None else merged + term
    x = x_ref[...] + _mm(merged, wout_ref[...])
    h = x * lax.rsqrt(jnp.mean(x * x, axis=-1, keepdims=True) + RMS_EPS) * nffn_ref[...]
    hb = h.astype(BF16)
    gate = jnp.dot(hb, wg_ref[...], preferred_element_type=F32)
    up = jnp.dot(hb, wu_ref[...], preferred_element_type=F32)
    y_ref[...] = x + _mm(gate * _sigmoid(gate) * up, wd_ref[...])


def _merge_ffn(x2d, oa, ob, oc, od, gates, p, tm, nt):
    n = x2d.shape[0]
    row = lambda width: pl.BlockSpec((tm, width), lambda i: (i, 0))
    oa_spec = row(WIDTH) if nt is None else _time_major_spec(tm, nt)
    return pl.pallas_call(
        _merge_ffn_kernel,
        grid=(n // tm,),
        in_specs=[row(D_MODEL), oa_spec, row(WIDTH), row(WIDTH), row(WIDTH), row(GATE_COLS),
                  _const_spec((N_BRANCH, WIDTH, D_MODEL)), _const_spec((D_MODEL, D_MODEL)),
                  _const_spec((1, D_MODEL)), _const_spec((D_MODEL, D_FF)),
                  _const_spec((D_MODEL, D_FF)), _const_spec((D_FF, D_MODEL))],
        out_specs=row(D_MODEL),
        out_shape=jax.ShapeDtypeStruct((n, D_MODEL), F32),
        compiler_params=_params("parallel"),
        name="merge_ffn",
    )(x2d, oa, ob, oc, od, gates, p["w_branch"], p["w_out"], p["norm_ffn"], p["w_ffn_gate"],
      p["w_ffn_up"], p["w_ffn_down"])


def _prep_layer(w):
    p = {}
    w_in = w["w_in"]
    o_gla = WIDTH
    o_alat = o_gla + 2 * GLA_QK + 2 * WIDTH
    o_rwkv = o_alat + GLA_RANK
    o_sb = o_rwkv + RWKV_COLS
    o_sbk = o_sb + WIDTH
    o_gate = o_sb + SB_COLS
    pad = jnp.zeros((D_MODEL, 128 - GLA_RANK), w_in.dtype)
    p["w_in"] = jnp.concatenate(
        [w_in[:, :o_rwkv], pad, w_in[:, o_rwkv:o_sbk], w_in[:, o_gate:]], axis=1).astype(BF16)
    p["w_kv_t"] = w_in[:, o_sbk:o_gate].T.astype(BF16)
    p["norm_mix"] = w["norm_mix"].reshape(1, D_MODEL)

    lam = lax.complex(w["s5_a_re"], w["s5_a_im"])
    dt = jnp.exp(w["s5_log_dt"])[:, None]
    a_bar = jnp.exp(lam * dt)
    b_bar = ((a_bar - 1.0) / lam)[..., None] * lax.complex(w["s5_b_re"], w["s5_b_im"])
    eye = jnp.eye(S5_GROUPS, dtype=F32)

    def in_map(m):
        return jnp.einsum("gpc,gh->gchp", m, eye).reshape(WIDTH, S5_FLAT)

    def out_map(m):
        return jnp.einsum("gcp,gh->gphc", m, eye).reshape(S5_FLAT, WIDTH)

    p["s5_bbig"] = jnp.concatenate([in_map(b_bar.real), in_map(b_bar.imag)], axis=1).astype(BF16)
    p["s5_cbig"] = jnp.concatenate([out_map(w["s5_c_re"]), out_map(-w["s5_c_im"])], axis=0).astype(BF16)
    p["s5_ar"] = a_bar.real.reshape(1, S5_FLAT)
    p["s5_ai"] = a_bar.imag.reshape(1, S5_FLAT)
    p["s5_d"] = w["s5_d"].reshape(1, WIDTH)
    p["s5_w_glu"] = w["s5_w_glu"].astype(BF16)
    p["s5_b_glu"] = w["s5_b_glu"].reshape(1, WIDTH)

    p["gla_w_alpha"] = jnp.concatenate(
        [w["gla_w_alpha"], jnp.zeros((128 - GLA_RANK, GLA_QK), F32)], axis=0).astype(BF16)
    p["gla_b_alpha"] = w["gla_b_alpha"].reshape(1, GLA_QK)
    p["gla_norm"] = w["gla_norm"].reshape(1, WIDTH)

    for name in ("rwkv_w0", "rwkv_a0", "rwkv_k_k", "rwkv_k_a", "rwkv_r_k", "rwkv_ln_w", "rwkv_ln_b"):
        p[name] = w[name].reshape(1, WIDTH)
    p["rwkv_mu"] = w["rwkv_mu"].reshape(1, RWKV_COLS)
    for name in ("rwkv_w2", "rwkv_a2", "rwkv_g2"):
        p[name] = w[name].astype(BF16)
    head = jnp.arange(WIDTH) // RWKV_HEAD
    p["head_ones"] = (head[:, None] == head[None, :]).astype(BF16)

    p["sb_q_norm"] = jnp.tile(w["sb_q_norm"] * (SB_HEAD ** -0.5), SB_HEADS).reshape(1, WIDTH)
    p["sb_k_norm"] = jnp.tile(w["sb_k_norm"], SB_HEADS).reshape(WIDTH, 1)

    p["w_branch"] = w["w_branch"].astype(BF16)
    p["w_out"] = w["w_out"].astype(BF16)
    p["norm_ffn"] = w["norm_ffn"].reshape(1, D_MODEL)
    for name in ("w_ffn_gate", "w_ffn_up", "w_ffn_down"):
        p[name] = w[name].astype(BF16)
    return p


def _pick(n, prefs):
    for c in prefs:
        if n % c == 0:
            return c
    return n


def _layer(x, past_k_t, past_v_t, layer, s5_0, gla_0, rwkv_0, shift_0, p):
    bt, t, _ = x.shape
    n = bt * t
    x2d = x.reshape(n, D_MODEL)
    tm = _pick(n, (256, 128, 64, 32, 16, 8))
    nt = t // tm if t % tm == 0 else None
    assert nt is not None or tm % t == 0
    u, c_gla, c_rwkv, q, k_t, v_t, gates = _inproj(x2d, p, bt, t, tm, nt)

    if nt is None:
        u = u.reshape(bt, t, WIDTH).transpose(1, 0, 2)
    x0 = jnp.concatenate([s5_0[..., 0].reshape(bt, S5_FLAT), s5_0[..., 1].reshape(bt, S5_FLAT)], axis=1)
    lc = _pick(t, (64, 32, 16, 8))
    o_a, xfin = _s5(u.reshape(n, WIDTH), x0, p, bt, lc)
    if nt is None:
        o_a = o_a.reshape(t, bt, WIDTH).transpose(1, 0, 2).reshape(n, WIDTH)
    else:
        o_a = o_a.reshape(t, bt * WIDTH)
    s5_new = jnp.stack([xfin[:, :S5_FLAT].reshape(bt, S5_GROUPS, S5_STATE),
                        xfin[:, S5_FLAT:].reshape(bt, S5_GROUPS, S5_STATE)], axis=-1)

    chunk = _pick(t, (64, 32, 16, 8))
    nseq = _pick(bt, (RWKV_ROWS // chunk, 4, 2, 1))
    o_b, gla_t = _gla(c_gla.reshape(bt, t, GLA_COLS_PAD), jnp.swapaxes(gla_0, 2, 3), p, nseq, chunk)
    gla_new = jnp.swapaxes(gla_t, 2, 3)

    o_c, shift_new, rwkv_new = _rwkv(c_rwkv.reshape(bt, t, RWKV_COLS),
                                     shift_0.reshape(bt, 1, RWKV_COLS), rwkv_0, p, nseq, chunk)

    if past_k_t is None:
        o_d = _sb_self(q, k_t, v_t, _pick(t, (256, 128, 64, 32, 16, 8)))
    else:
        tkp = _pick(past_k_t.shape[3], (256, 128))
        o_d = _sb_past(q, k_t, v_t, past_k_t, past_v_t, layer, tkp)

    y = _merge_ffn(x2d, o_a, o_b.reshape(n, WIDTH), o_c.reshape(n, WIDTH), o_d, gates, p, tm, nt)

    def cache_rows(a_t):
        return a_t.reshape(bt, SB_HEADS, SB_HEAD, t).transpose(0, 3, 1, 2)

    states = (cache_rows(k_t), cache_rows(v_t), s5_new, gla_new, rwkv_new,
              shift_new.reshape(bt, RWKV_COLS))
    return y.reshape(bt, t, D_MODEL), states


def _keys_on_lanes(cache):
    d, b, pl_, _, _ = cache.shape
    return cache.transpose(0, 1, 3, 4, 2).reshape(d, b, WIDTH, pl_)


def _trunk(x, past_k, past_v, s5_0, gla_0, rwkv_0, shift_0, layers):
    past_k_t = None if past_k is None else _keys_on_lanes(past_k)
    past_v_t = None if past_v is None else _keys_on_lanes(past_v)
    per_layer = []
    for l, p in enumerate(layers):
        x, st = _layer(x, past_k_t, past_v_t, l, s5_0[l], gla_0[l], rwkv_0[l], shift_0[l], p)
        per_layer.append(st)
    return x, [jnp.stack([st[i] for st in per_layer]) for i in range(6)]


def kernel(x_prompt, x_sample, cache_sb_k, cache_sb_v, state_s5, state_gla, state_rwkv, state_rwkv_shift, norm_mix, w_in, s5_a_re, s5_a_im, s5_log_dt, s5_b_re, s5_b_im, s5_c_re, s5_c_im, s5_d, s5_w_glu, s5_b_glu, gla_w_alpha, gla_b_alpha, gla_norm, rwkv_mu, rwkv_w0, rwkv_w2, rwkv_a0, rwkv_a2, rwkv_g2, rwkv_k_k, rwkv_k_a, rwkv_r_k, rwkv_ln_w, rwkv_ln_b, sb_q_norm, sb_k_norm, w_branch, w_out, norm_ffn, w_ffn_gate, w_ffn_up, w_ffn_down):
    weights = dict(norm_mix=norm_mix, w_in=w_in, s5_a_re=s5_a_re, s5_a_im=s5_a_im, s5_log_dt=s5_log_dt,
                   s5_b_re=s5_b_re, s5_b_im=s5_b_im, s5_c_re=s5_c_re, s5_c_im=s5_c_im, s5_d=s5_d,
                   s5_w_glu=s5_w_glu, s5_b_glu=s5_b_glu, gla_w_alpha=gla_w_alpha, gla_b_alpha=gla_b_alpha,
                   gla_norm=gla_norm, rwkv_mu=rwkv_mu, rwkv_w0=rwkv_w0, rwkv_w2=rwkv_w2, rwkv_a0=rwkv_a0,
                   rwkv_a2=rwkv_a2, rwkv_g2=rwkv_g2, rwkv_k_k=rwkv_k_k, rwkv_k_a=rwkv_k_a, rwkv_r_k=rwkv_r_k,
                   rwkv_ln_w=rwkv_ln_w, rwkv_ln_b=rwkv_ln_b, sb_q_norm=sb_q_norm, sb_k_norm=sb_k_norm,
                   w_branch=w_branch, w_out=w_out, norm_ffn=norm_ffn, w_ffn_gate=w_ffn_gate,
                   w_ffn_up=w_ffn_up, w_ffn_down=w_ffn_down)
    depth = w_in.shape[0]
    layers = [_prep_layer({name: arr[l] for name, arr in weights.items()}) for l in range(depth)]

    bp = x_prompt.shape[0]
    y_prompt, p_states = _trunk(
        x_prompt, None, None,
        jnp.zeros((depth, bp, S5_GROUPS, S5_STATE, 2), F32),
        jnp.zeros((depth, bp, GLA_HEADS, GLA_DK, GLA_DV), F32),
        jnp.zeros((depth, bp, RWKV_HEADS, RWKV_HEAD, RWKV_HEAD), F32),
        jnp.zeros((depth, bp, RWKV_COLS), F32), layers)
    y_sample, s_states = _trunk(x_sample, cache_sb_k, cache_sb_v, state_s5, state_gla, state_rwkv,
                                state_rwkv_shift, layers)
    return (y_prompt, y_sample, *p_states, *s_states)
```

```python
import functools
import math

import jax
import jax.numpy as jnp
from jax import lax
from jax.experimental import pallas as pl
from jax.experimental.pallas import tpu as pltpu

F32 = jnp.float32
BF16 = jnp.bfloat16

D_MODEL = 1024
WIDTH = 256
N_BRANCH = 4
S5_GROUPS, S5_GROUP, S5_STATE = 16, 16, 64
S5_FLAT = S5_GROUPS * S5_STATE
GLA_HEADS, GLA_DK, GLA_DV, GLA_RANK, GLA_TAU = 4, 32, 64, 16, 16.0
GLA_QK = GLA_HEADS * GLA_DK
GLA_COLS_PAD = 2 * GLA_QK + 2 * WIDTH + 128
RWKV_HEADS, RWKV_HEAD = 4, 64
RWKV_COLS = 1024
RWKV_GN_EPS = 64e-5
RWKV_ROWS = 256
RWKV_GROUPS = 2
SB_HEADS, SB_HEAD = 4, 64
SB_COLS = 3 * WIDTH
GATE_COLS = N_BRANCH * D_MODEL
D_FF = 2816
RMS_EPS = 1e-6

_C_S5 = 0
_C_GLA = _C_S5 + WIDTH
_C_RWKV = _C_GLA + GLA_COLS_PAD
_C_SBQ = _C_RWKV + RWKV_COLS
_C_GATE = _C_SBQ + WIDTH
_C_END = _C_GATE + GATE_COLS

VMEM_LIMIT = 56 * 1024 * 1024
INPROJ_ROWS = 512


def _params(*sem):
    return pltpu.CompilerParams(dimension_semantics=sem, vmem_limit_bytes=VMEM_LIMIT)


def _const_spec(shape):
    nd = len(shape)
    return pl.BlockSpec(shape, lambda *_: (0,) * nd, pipeline_mode=pl.Buffered(1))


def _mm(a, b):
    return jnp.dot(a.astype(BF16), b.astype(BF16), preferred_element_type=F32)


def _mm_nt(a, b):
    return lax.dot_general(a.astype(BF16), b.astype(BF16), (((1,), (1,)), ((), ())),
                           preferred_element_type=F32)


def _mm_tn(a, b):
    return lax.dot_general(a.astype(BF16), b.astype(BF16), (((0,), (0,)), ((), ())),
                           preferred_element_type=F32)


def _split(a):
    bits = lax.bitcast_convert_type(a, jnp.uint32) & jnp.uint32(0xFFFF0000)
    hi = lax.bitcast_convert_type(bits, F32)
    return hi.astype(BF16), (a - hi).astype(BF16)


def _mm_split_lhs(a, b01):
    hi, lo = _split(a)
    return (jnp.dot(hi, b01, preferred_element_type=F32)
            + jnp.dot(lo, b01, preferred_element_type=F32))


def _mm_split_rhs(a01, b):
    hi, lo = _split(b)
    return (jnp.dot(a01, hi, preferred_element_type=F32)
            + jnp.dot(a01, lo, preferred_element_type=F32))


def _log_sigmoid(z):
    return jnp.minimum(z, 0.0) - jnp.log1p(jnp.exp(-jnp.abs(z)))


def _sigmoid(z):
    return 1.0 / (1.0 + jnp.exp(-z))


def _tri(n, strict=False):
    r = lax.broadcasted_iota(jnp.int32, (n, n), 0)
    c = lax.broadcasted_iota(jnp.int32, (n, n), 1)
    return (r > c) if strict else (r >= c)


def _inproj_kernel(x_ref, g_ref, w_ref, wkv_ref, qn_ref, kn_ref, ones_ref,
                   s5_ref, gla_ref, rwkv_ref, q_ref, kt_ref, vt_ref, gate_ref, *, nseq):
    x = x_ref[...]
    h = x * lax.rsqrt(jnp.mean(x * x, axis=-1, keepdims=True) + RMS_EPS) * g_ref[...]
    hb = h.astype(BF16)

    def mm(lo, hi):
        return jnp.dot(hb, w_ref[:, lo:hi], preferred_element_type=F32)

    s5_ref[...] = mm(_C_S5, _C_GLA)
    gla_ref[...] = mm(_C_GLA, _C_RWKV)
    rwkv_ref[...] = mm(_C_RWKV, _C_SBQ)
    for n in range(N_BRANCH):
        lo = _C_GATE + n * D_MODEL
        gate_ref[:, n * D_MODEL:(n + 1) * D_MODEL] = _sigmoid(mm(lo, lo + D_MODEL)).astype(BF16)

    q = mm(_C_SBQ, _C_GATE)
    q_ms = _mm_split_lhs(q * q, ones_ref[...]) * (1.0 / SB_HEAD)
    q_ref[...] = (q * lax.rsqrt(q_ms + RMS_EPS) * qn_ref[...]).astype(BF16)

    kv_t = lax.dot_general(wkv_ref[...], hb, (((1,), (1,)), ((), ())), preferred_element_type=F32)
    k_t = []
    for hd in range(SB_HEADS):
        kh = kv_t[SB_HEAD * hd:SB_HEAD * (hd + 1)]
        k_t.append(kh * lax.rsqrt(jnp.mean(kh * kh, axis=0, keepdims=True) + RMS_EPS))
    k_t = jnp.concatenate(k_t, axis=0) * kn_ref[...]
    v_t = kv_t[WIDTH:2 * WIDTH]
    t = kt_ref.shape[2]
    for s in range(nseq):
        kt_ref[s] = k_t[:, t * s:t * (s + 1)]
        vt_ref[s] = v_t[:, t * s:t * (s + 1)]


def _time_major_spec(tm, nt):
    return pl.BlockSpec((tm, WIDTH), lambda i: (i % nt, i // nt))


def _inproj(x2d, p, bt, t, tm, nt):
    n = x2d.shape[0]
    row = lambda width: pl.BlockSpec((tm, width), lambda i: (i, 0))
    if nt is None:
        nseq = tm // t
        s5_spec, s5_shape = row(WIDTH), (n, WIDTH)
        kv_spec = pl.BlockSpec((nseq, WIDTH, t), lambda i: (i, 0, 0))
    else:
        nseq = 1
        s5_spec, s5_shape = _time_major_spec(tm, nt), (t, bt * WIDTH)
        kv_spec = pl.BlockSpec((1, WIDTH, tm), lambda i: (i // nt, 0, i % nt))
    kv_shape = jax.ShapeDtypeStruct((bt, WIDTH, t), F32)
    return pl.pallas_call(
        functools.partial(_inproj_kernel, nseq=nseq),
        grid=(n // tm,),
        in_specs=[row(D_MODEL), _const_spec((1, D_MODEL)), _const_spec((D_MODEL, _C_END)),
                  _const_spec((2 * WIDTH, D_MODEL)), _const_spec((1, WIDTH)),
                  _const_spec((WIDTH, 1)), _const_spec((WIDTH, WIDTH))],
        out_specs=(s5_spec, row(GLA_COLS_PAD), row(RWKV_COLS), row(WIDTH), kv_spec, kv_spec,
                   row(GATE_COLS)),
        out_shape=(jax.ShapeDtypeStruct(s5_shape, F32),
                   jax.ShapeDtypeStruct((n, GLA_COLS_PAD), F32),
                   jax.ShapeDtypeStruct((n, RWKV_COLS), F32),
                   jax.ShapeDtypeStruct((n, WIDTH), BF16), kv_shape, kv_shape,
                   jax.ShapeDtypeStruct((n, GATE_COLS), BF16)),
        compiler_params=_params("parallel"),
        name="inproj",
    )(x2d, p["norm_mix"], p["w_in"], p["w_kv_t"], p["sb_q_norm"], p["sb_k_norm"], p["head_ones"])


def _s5_kernel(u_ref, x0_ref, bbig_ref, cbig_ref, ar_ref, ai_ref, d_ref, wglu_ref, bglu_ref,
               o_ref, xfin_ref, bu_scr, x_scr, *, nb, lc):
    @pl.when(pl.program_id(0) == 0)
    def _():
        x_scr[...] = x0_ref[...]

    u = u_ref[...]
    bu_scr[...] = _mm(u, bbig_ref[...])
    ar = jnp.broadcast_to(ar_ref[...], (nb, S5_FLAT))
    ai = jnp.broadcast_to(ai_ref[...], (nb, S5_FLAT))

    def step(t, carry):
        xr, xi = carry
        rows = pl.ds(pl.multiple_of(t * nb, nb), nb)
        nxr = ar * xr - ai * xi + bu_scr[rows, 0:S5_FLAT]
        nxi = ar * xi + ai * xr + bu_scr[rows, S5_FLAT:2 * S5_FLAT]
        bu_scr[rows, 0:S5_FLAT] = nxr
        bu_scr[rows, S5_FLAT:2 * S5_FLAT] = nxi
        return nxr, nxi

    xr, xi = lax.fori_loop(0, lc, step, (x_scr[:, 0:S5_FLAT], x_scr[:, S5_FLAT:2 * S5_FLAT]))
    x_scr[:, 0:S5_FLAT] = xr
    x_scr[:, S5_FLAT:2 * S5_FLAT] = xi
    xfin_ref[...] = x_scr[...]

    y = _mm(bu_scr[...], cbig_ref[...]) + d_ref[...] * u
    y = jax.nn.gelu(y)
    o_ref[...] = y * _sigmoid(_mm(y, wglu_ref[...]) + bglu_ref[...])


def _s5(u_tm, x0, p, nb, lc):
    rows = u_tm.shape[0]
    blk = lc * nb
    kern = functools.partial(_s5_kernel, nb=nb, lc=lc)
    return pl.pallas_call(
        kern,
        grid=(rows // blk,),
        in_specs=[pl.BlockSpec((blk, WIDTH), lambda c: (c, 0)),
                  _const_spec((nb, 2 * S5_FLAT)),
                  _const_spec((WIDTH, 2 * S5_FLAT)), _const_spec((2 * S5_FLAT, WIDTH)),
                  _const_spec((1, S5_FLAT)), _const_spec((1, S5_FLAT)), _const_spec((1, WIDTH)),
                  _const_spec((WIDTH, WIDTH)), _const_spec((1, WIDTH))],
        out_specs=(pl.BlockSpec((blk, WIDTH), lambda c: (c, 0)),
                   pl.BlockSpec((nb, 2 * S5_FLAT), lambda c: (0, 0))),
        out_shape=(jax.ShapeDtypeStruct((rows, WIDTH), F32),
                   jax.ShapeDtypeStruct((nb, 2 * S5_FLAT), F32)),
        scratch_shapes=[pltpu.VMEM((blk, 2 * S5_FLAT), F32), pltpu.VMEM((nb, 2 * S5_FLAT), F32)],
        compiler_params=_params("arbitrary"),
        name="s5",
    )(u_tm, x0, p["s5_bbig"], p["s5_cbig"], p["s5_ar"], p["s5_ai"], p["s5_d"], p["s5_w_glu"],
      p["s5_b_glu"])


def _gla_kernel(z_ref, s0_ref, walpha_ref, balpha_ref, gnorm_ref, o_ref, sfin_ref, s_scr,
                *, ng, nb, L):
    R = nb * L
    DK = GLA_DK
    groups = range(ng)
    heads = range(GLA_HEADS)
    ksl = [slice(DK * h, DK * (h + 1)) for h in heads]
    vsl = [slice(GLA_DV * h, GLA_DV * (h + 1)) for h in heads]

    @pl.when(pl.program_id(1) == 0)
    def _():
        for gi in groups:
            for i in range(nb):
                for h in heads:
                    s_scr[gi, h, :, DK * i:DK * (i + 1)] = s0_ref[gi * nb + i, h]

    rr = lax.broadcasted_iota(jnp.int32, (R, R), 0)
    cc = lax.broadcasted_iota(jnp.int32, (R, R), 1)
    tri = jnp.logical_and((rr & -L) == (cc & -L), rr >= cc)
    tri_ones = tri.astype(BF16)
    own = ((lax.broadcasted_iota(jnp.int32, (R, nb * DK), 0) & -L) * DK
           == (lax.broadcasted_iota(jnp.int32, (R, nb * DK), 1) & -DK) * L)

    def spread(x):
        return jnp.where(own, jnp.concatenate([x] * nb, axis=1), 0.0).astype(BF16)

    def prologue(gi):
        z = z_ref[gi * nb:(gi + 1) * nb].reshape(R, GLA_COLS_PAD)
        q = z[:, 0:GLA_QK] * (GLA_DK ** -0.5)
        k = z[:, GLA_QK:2 * GLA_QK]
        alat = z[:, 2 * GLA_QK + 2 * WIDTH:]
        log_a = _log_sigmoid(_mm(alat, walpha_ref[...]) + balpha_ref[...]) * (1.0 / GLA_TAU)
        b = _mm_split_rhs(tri_ones, log_a)
        b_last = [b[L * (i + 1) - 1:L * (i + 1), :] for i in range(nb)]
        b_end = jnp.concatenate([jnp.broadcast_to(be, (L, GLA_QK)) for be in b_last], axis=0)
        return dict(v=z[:, 2 * GLA_QK:2 * GLA_QK + WIDTH],
                    g=z[:, 2 * GLA_QK + WIDTH:2 * GLA_QK + 2 * WIDTH], b_last=b_last,
                    qt=q * jnp.exp(b), kt=k * jnp.exp(-b), kd=k * jnp.exp(b_end - b))

    pro = [prologue(gi) for gi in groups]
    units = [(gi, h) for gi in groups for h in heads]
    st = [s_scr[gi, h] for gi, h in units]
    attn = [jnp.where(tri, _mm_nt(pro[gi]["qt"][:, ksl[h]], pro[gi]["kt"][:, ksl[h]]), 0.0)
            for gi, h in units]
    from_s = [_mm_nt(spread(pro[gi]["qt"][:, ksl[h]]), st[u]) for u, (gi, h) in enumerate(units)]
    o_h = [_mm(attn[u], pro[gi]["v"][:, vsl[h]]) + from_s[u] for u, (gi, h) in enumerate(units)]
    for u, (gi, h) in enumerate(units):
        e_all = jnp.concatenate([jnp.exp(be[:, ksl[h]]) for be in pro[gi]["b_last"]], axis=1)
        s_scr[gi, h] = st[u] * e_all + _mm_tn(pro[gi]["v"][:, vsl[h]], spread(pro[gi]["kd"][:, ksl[h]]))
    for gi in groups:
        outs = []
        for h in heads:
            oh = o_h[gi * GLA_HEADS + h]
            outs.append(oh * lax.rsqrt(jnp.mean(oh * oh, axis=-1, keepdims=True) + RMS_EPS)
                        * gnorm_ref[:, vsl[h]])
        g = pro[gi]["g"]
        o_ref[gi * nb:(gi + 1) * nb] = (jnp.concatenate(outs, axis=-1)
                                        * (g * _sigmoid(g))).reshape(nb, L, WIDTH)

    @pl.when(pl.program_id(1) == pl.num_programs(1) - 1)
    def _():
        for gi in groups:
            for i in range(nb):
                for h in heads:
                    sfin_ref[gi * nb + i, h] = s_scr[gi, h, :, DK * i:DK * (i + 1)]


def _gla(cols, s0t, p, ng, nb, L):
    bt, t, _ = cols.shape
    ns = ng * nb
    assert bt % ns == 0 and t % L == 0 and L & (L - 1) == 0
    kern = functools.partial(_gla_kernel, ng=ng, nb=nb, L=L)
    state = pl.BlockSpec((ns, GLA_HEADS, GLA_DV, GLA_DK), lambda b, n: (b, 0, 0, 0))
    return pl.pallas_call(
        kern,
        grid=(bt // ns, t // L),
        in_specs=[pl.BlockSpec((ns, L, GLA_COLS_PAD), lambda b, n: (b, n, 0)), state,
                  _const_spec((128, GLA_QK)), _const_spec((1, GLA_QK)), _const_spec((1, WIDTH))],
        out_specs=(pl.BlockSpec((ns, L, WIDTH), lambda b, n: (b, n, 0)), state),
        out_shape=(jax.ShapeDtypeStruct((bt, t, WIDTH), F32),
                   jax.ShapeDtypeStruct((bt, GLA_HEADS, GLA_DV, GLA_DK), F32)),
        scratch_shapes=[pltpu.VMEM((ng, GLA_HEADS, GLA_DV, nb * GLA_DK), F32)],
        compiler_params=_params("parallel", "arbitrary"),
        name="gla",
    )(cols, s0t, p["gla_w_alpha"], p["gla_b_alpha"], p["gla_norm"])


def _rwkv_kernel(z_ref, shift0_ref, s0_ref, mu_ref, w0_ref, w2_ref, a0_ref, a2_ref, g2_ref, kk_ref,
                 ka_ref, rk_ref, lnw_ref, lnb_ref, bd_ref, o_ref, shiftfin_ref, sfin_ref,
                 s_scr, prev_scr, *, ng, nb, L):
    R = nb * L
    HD = RWKV_HEAD
    groups = range(ng)
    heads = range(RWKV_HEADS)
    hsl = [slice(HD * h, HD * (h + 1)) for h in heads]

    @pl.when(pl.program_id(1) == 0)
    def _():
        for gi in groups:
            for i in range(nb):
                for h in heads:
                    s_scr[gi, h, :, HD * i:HD * (i + 1)] = s0_ref[gi * nb + i, h]
        prev_scr[...] = shift0_ref[...]

    rr = lax.broadcasted_iota(jnp.int32, (R, R), 0)
    cc = lax.broadcasted_iota(jnp.int32, (R, R), 1)
    same = (rr & -L) == (cc & -L)
    tri = jnp.logical_and(same, rr >= cc)
    stri = jnp.logical_and(same, rr > cc)
    tri_ones = tri.astype(BF16)
    own = ((lax.broadcasted_iota(jnp.int32, (R, nb * HD), 0) & -L) * HD
           == (lax.broadcasted_iota(jnp.int32, (R, nb * HD), 1) & -HD) * L)
    first = (lax.broadcasted_iota(jnp.int32, (R, RWKV_COLS), 0) & (L - 1)) == 0

    def spread(x):
        return jnp.where(own, jnp.concatenate([x] * nb, axis=1), 0.0).astype(BF16)

    pro = [_rwkv_prologue(gi, z_ref, prev_scr, shiftfin_ref, first, tri_ones, mu_ref, w0_ref, w2_ref,
                          a0_ref, a2_ref, g2_ref, kk_ref, ka_ref, bd_ref, nb, L) for gi in groups]

    units = [(gi, h) for gi in groups for h in heads]
    f32dot = functools.partial(jnp.dot, preferred_element_type=F32)
    vb = [pro[gi]["v"][:, hsl[h]].astype(BF16) for gi, h in units]
    gram = [_mm_nt(jnp.concatenate([pro[gi]["a_til"][:, hsl[h]], pro[gi]["r_til"][:, hsl[h]]], axis=0),
                   jnp.concatenate([pro[gi]["b_til"][:, hsl[h]], pro[gi]["k_til"][:, hsl[h]]], axis=0))
            for gi, h in units]
    from_s = [_mm_nt(jnp.concatenate([spread(pro[gi]["a_til"][:, hsl[h]]),
                                      spread(pro[gi]["r_til"][:, hsl[h]])], axis=0), s_scr[gi, h])
              for gi, h in units]
    pw = [jnp.where(stri, gm[:R, :R], 0.0).astype(BF16) for gm in gram]
    a_ak = [jnp.where(stri, gm[:R, R:], 0.0).astype(BF16) for gm in gram]
    a_r = [jnp.concatenate([jnp.where(tri, gm[R:, :R], 0.0).astype(BF16),
                            jnp.where(tri, gm[R:, R:], 0.0).astype(BF16)], axis=1) for gm in gram]
    n_units = range(len(units))
    sa = [from_s[u][:R] + f32dot(a_ak[u], vb[u]) for u in n_units]
    sa = [sa[u] + f32dot(pw[u], sa[u].astype(BF16)) for u in n_units]
    for _ in range(int(math.log2(L)) - 1):
        pw = [f32dot(pw[u], pw[u]).astype(BF16) for u in n_units]
        sa = [sa[u] + f32dot(pw[u], sa[u].astype(BF16)) for u in n_units]
    sa_v = [jnp.concatenate([sa[u].astype(BF16), vb[u]], axis=0) for u in n_units]
    y = [from_s[u][R:] + f32dot(a_r[u], sa_v[u]) for u in n_units]
    for u, (gi, h) in enumerate(units):
        p_all = jnp.concatenate([pe[:, hsl[h]] for pe in pro[gi]["p_last"]], axis=1)
        s_scr[gi, h] = s_scr[gi, h] * p_all + _mm_tn(
            sa_v[u], jnp.concatenate([spread(pro[gi]["b_end"][:, hsl[h]]),
                                      spread(pro[gi]["k_end"][:, hsl[h]])], axis=0))
    for gi in groups:
        outs = []
        for h in heads:
            yh = y[gi * RWKV_HEADS + h]
            hs = hsl[h]
            mean = jnp.mean(yh, axis=-1, keepdims=True)
            var = jnp.mean(jnp.square(yh - mean), axis=-1, keepdims=True)
            yn = (yh - mean) * lax.rsqrt(var + RWKV_GN_EPS) * lnw_ref[:, hs] + lnb_ref[:, hs]
            bonus = jnp.sum(pro[gi]["r"][:, hs] * pro[gi]["k"][:, hs] * rk_ref[:, hs], axis=-1,
                            keepdims=True)
            outs.append(yn + bonus * pro[gi]["v"][:, hs])
        o_ref[gi * nb:(gi + 1) * nb] = (jnp.concatenate(outs, axis=-1)
                                        * pro[gi]["g"]).reshape(nb, L, WIDTH)

    @pl.when(pl.program_id(1) == pl.num_programs(1) - 1)
    def _():
        for gi in groups:
            for i in range(nb):
                for h in heads:
                    sfin_ref[gi * nb + i, h] = s_scr[gi, h, :, HD * i:HD * (i + 1)]


def _rwkv_prologue(gi, z_ref, prev_scr, shiftfin_ref, first, tri_ones, mu_ref, w0_ref, w2_ref, a0_ref,
                   a2_ref, g2_ref, kk_ref, ka_ref, bd_ref, nb, L):
    R = nb * L
    seqs = range(gi * nb, (gi + 1) * nb)
    z = z_ref[gi * nb:(gi + 1) * nb].reshape(R, RWKV_COLS)
    carried = jnp.concatenate(
        [jnp.broadcast_to(prev_scr[s], (L, RWKV_COLS)) for s in seqs], axis=0)
    prev = jnp.where(first, carried, pltpu.roll(z, 1, axis=0))
    for i, s in enumerate(seqs):
        last = z[L * (i + 1) - 1:L * (i + 1), :]
        prev_scr[s] = last
        shiftfin_ref[s] = last
    zs = z + (prev - z) * mu_ref[...]
    r = zs[:, 0:256]
    k = zs[:, 256:512]
    v = zs[:, 512:768]
    w_lat = zs[:, 768:832]
    a_lat = zs[:, 832:896]
    g_lat = zs[:, 896:1024]
    w = -jax.nn.softplus(-(w0_ref[...] + _mm(jnp.tanh(w_lat), w2_ref[...]))) - 0.5
    log_w = -jnp.exp(w)
    a = _sigmoid(a0_ref[...] + _mm(a_lat, a2_ref[...]))
    g = _mm(_sigmoid(g_lat), g2_ref[...])
    kk = k * kk_ref[...]
    kk = kk / jnp.maximum(jnp.sqrt(_mm_split_lhs(kk * kk, bd_ref[...])), 1e-12)
    k = k * (1.0 + (a - 1.0) * ka_ref[...])
    cum = _mm_split_rhs(tri_ones, log_w)
    p_t = jnp.exp(cum)
    inv_p = jnp.exp(-cum)
    p_last = [p_t[L * (i + 1) - 1:L * (i + 1), :] for i in range(nb)]
    p_end = jnp.concatenate([jnp.broadcast_to(pe, (L, WIDTH)) for pe in p_last], axis=0)
    b_til = kk * a * inv_p
    k_til = k * inv_p
    return dict(r=r, k=k, v=v, g=g, p_last=p_last,
                a_til=-kk * jnp.exp(cum - log_w),
                b_til=b_til, k_til=k_til, r_til=r * p_t,
                b_end=b_til * p_end, k_end=k_til * p_end)


def _rwkv(cols, shift0, s0, p, ng, nb, L):
    bt, t, _ = cols.shape
    ns = ng * nb
    assert bt % ns == 0 and t % L == 0 and L & (L - 1) == 0
    kern = functools.partial(_rwkv_kernel, ng=ng, nb=nb, L=L)
    state = pl.BlockSpec((ns, RWKV_HEADS, RWKV_HEAD, RWKV_HEAD), lambda b, n: (b, 0, 0, 0))
    shift = pl.BlockSpec((ns, 1, RWKV_COLS), lambda b, n: (b, 0, 0))
    vec = _const_spec((1, WIDTH))
    return pl.pallas_call(
        kern,
        grid=(bt // ns, t // L),
        in_specs=[pl.BlockSpec((ns, L, RWKV_COLS), lambda b, n: (b, n, 0)), shift, state,
                  _const_spec((1, RWKV_COLS)), vec, _const_spec((64, WIDTH)), vec,
                  _const_spec((64, WIDTH)), _const_spec((128, WIDTH)), vec, vec, vec, vec, vec,
                  _const_spec((WIDTH, WIDTH))],
        out_specs=(pl.BlockSpec((ns, L, WIDTH), lambda b, n: (b, n, 0)), shift, state),
        out_shape=(jax.ShapeDtypeStruct((bt, t, WIDTH), F32),
                   jax.ShapeDtypeStruct((bt, 1, RWKV_COLS), F32),
                   jax.ShapeDtypeStruct((bt, RWKV_HEADS, RWKV_HEAD, RWKV_HEAD), F32)),
        scratch_shapes=[pltpu.VMEM((ng, RWKV_HEADS, RWKV_HEAD, nb * RWKV_HEAD), F32),
                        pltpu.VMEM((ns, 1, RWKV_COLS), F32)],
        compiler_params=_params("parallel", "arbitrary"),
        name="rwkv",
    )(cols, shift0, s0, p["rwkv_mu"], p["rwkv_w0"], p["rwkv_w2"], p["rwkv_a0"], p["rwkv_a2"],
      p["rwkv_g2"], p["rwkv_k_k"], p["rwkv_k_a"], p["rwkv_r_k"], p["rwkv_ln_w"], p["rwkv_ln_b"],
      p["head_ones"])


SB_DEAD_LOG = -110.0
SB_LANES = 128


def _sb_block(qh, k, v, diag, suffix_ones, acc_scr, carry_scr):
    tq = qh[0].shape[0]
    tk = k.shape[1]
    kb = k.astype(BF16)
    z = jnp.concatenate(
        [jnp.dot(qh[h], kb[SB_HEAD * h:SB_HEAD * (h + 1)], preferred_element_type=F32)
         for h in range(SB_HEADS)], axis=0)
    lm = -(jnp.maximum(z, 0.0) + jnp.log(1.0 + jnp.exp(-jnp.abs(z))))
    lp = lm + z
    if diag:
        row = lax.broadcasted_iota(jnp.int32, z.shape, 0) & (tq - 1)
        mask = row > lax.broadcasted_iota(jnp.int32, z.shape, 1)
        lm = jnp.where(mask, lm, 0.0)
    sums = _mm_split_lhs(lm, suffix_ones)
    carry = carry_scr[...]
    if tk <= SB_LANES:
        carry_keys = carry[:, :tk]
    else:
        carry_keys = jnp.concatenate([carry] * (tk // SB_LANES), axis=1)
    wgt = jnp.exp(lp + carry_keys + sums[:, :tk])
    if diag:
        wgt = jnp.where(mask, wgt, 0.0)
    wgt = wgt.astype(BF16)
    vb = v.astype(BF16)
    for h in range(SB_HEADS):
        acc_scr[h] += lax.dot_general(wgt[tq * h:tq * (h + 1)], vb[SB_HEAD * h:SB_HEAD * (h + 1)],
                                      (((1,), (1,)), ((), ())), preferred_element_type=F32)
    carry = carry + sums[:, tk:]
    carry_scr[...] = carry
    return jnp.max(carry)


def _suffix_ones(tk):
    return jnp.concatenate([_tri(tk, strict=True).astype(BF16), jnp.ones((tk, SB_LANES), BF16)],
                           axis=1)


def _sb_heads(q):
    return [q[:, SB_HEAD * h:SB_HEAD * (h + 1)].astype(BF16) for h in range(SB_HEADS)]


def _sb_self_kernel(q_ref, k_ref, v_ref, o_ref, acc_scr, carry_scr, *, tq):
    i = pl.program_id(1)
    acc_scr[...] = jnp.zeros_like(acc_scr)
    carry_scr[...] = jnp.zeros_like(carry_scr)
    qh = _sb_heads(q_ref[...])
    ones = _suffix_ones(tq)

    def keys(j):
        return pl.ds(pl.multiple_of(j * tq, tq), tq)

    live = _sb_block(qh, k_ref[0, :, keys(i)], v_ref[0, :, keys(i)], True, ones, acc_scr, carry_scr)

    def body(state):
        j, _ = state
        m = _sb_block(qh, k_ref[0, :, keys(j)], v_ref[0, :, keys(j)], False, ones, acc_scr, carry_scr)
        return j - 1, m

    lax.while_loop(lambda s: jnp.logical_and(s[0] >= 0, s[1] > SB_DEAD_LOG), body, (i - 1, live))
    o_ref[...] = jnp.concatenate([acc_scr[h] for h in range(SB_HEADS)], axis=-1)


def _sb_self(q2d, k_t, v_t, tq):
    bt, _, t = k_t.shape
    nq = t // tq
    assert tq & (tq - 1) == 0
    kern = functools.partial(_sb_self_kernel, tq=tq)
    seq = pl.BlockSpec((1, WIDTH, t), lambda b, i: (b, 0, 0))
    blk = pl.BlockSpec((tq, WIDTH), lambda b, i: (b * nq + i, 0))
    return pl.pallas_call(
        kern,
        grid=(bt, nq),
        in_specs=[blk, seq, seq],
        out_specs=blk,
        out_shape=jax.ShapeDtypeStruct((bt * t, WIDTH), F32),
        scratch_shapes=[pltpu.VMEM((SB_HEADS, tq, SB_HEAD), F32),
                        pltpu.VMEM((SB_HEADS * tq, SB_LANES), F32)],
        compiler_params=_params("parallel", "arbitrary"),
        name="sb_self",
    )(q2d, k_t, v_t)


SB_OLDER_SLOT = 2


def _sb_past_kernel(q_ref, k_ref, v_ref, pk_hbm, pv_hbm, o_ref, kbuf, vbuf, sem, acc_scr, carry_scr,
                    *, layer, t, tkp, np_):
    b = pl.program_id(0)
    slot = b % 2
    qh = _sb_heads(q_ref[...])

    def fetch(seq, blk, dst):
        keys = pl.ds(pl.multiple_of(blk * tkp, tkp), tkp)
        return (pltpu.make_async_copy(pk_hbm.at[layer, seq, :, keys], kbuf.at[dst], sem.at[0, dst]),
                pltpu.make_async_copy(pv_hbm.at[layer, seq, :, keys], vbuf.at[dst], sem.at[1, dst]))

    @pl.when(b == 0)
    def _():
        for cp in fetch(0, np_ - 1, 0):
            cp.start()

    @pl.when(b + 1 < pl.num_programs(0))
    def _():
        for cp in fetch(b + 1, np_ - 1, 1 - slot):
            cp.start()

    acc_scr[...] = jnp.zeros_like(acc_scr)
    carry_scr[...] = jnp.zeros_like(carry_scr)
    _sb_block(qh, k_ref[0], v_ref[0], True, _suffix_ones(t), acc_scr, carry_scr)

    ones = _suffix_ones(tkp)
    for cp in fetch(b, np_ - 1, slot):
        cp.wait()
    live = _sb_block(qh, kbuf[slot], vbuf[slot], False, ones, acc_scr, carry_scr)

    def body(state):
        j, _ = state
        copies = fetch(b, j, SB_OLDER_SLOT)
        for cp in copies:
            cp.start()
        for cp in copies:
            cp.wait()
        m = _sb_block(qh, kbuf[SB_OLDER_SLOT], vbuf[SB_OLDER_SLOT], False, ones, acc_scr, carry_scr)
        return j - 1, m

    lax.while_loop(lambda s: jnp.logical_and(s[0] >= 0, s[1] > SB_DEAD_LOG), body, (np_ - 2, live))
    o_ref[...] = jnp.concatenate([acc_scr[h] for h in range(SB_HEADS)], axis=-1)


def _sb_past(q2d, k_t, v_t, past_k_t, past_v_t, layer, tkp):
    bt, _, t = k_t.shape
    np_ = past_k_t.shape[3] // tkp
    assert t & (t - 1) == 0 and past_k_t.shape[3] % tkp == 0
    kern = functools.partial(_sb_past_kernel, layer=layer, t=t, tkp=tkp, np_=np_)
    qblk = pl.BlockSpec((t, WIDTH), lambda b: (b, 0))
    new = pl.BlockSpec((1, WIDTH, t), lambda b: (b, 0, 0))
    hbm = pl.BlockSpec(memory_space=pl.ANY)
    return pl.pallas_call(
        kern,
        grid=(bt,),
        in_specs=[qblk, new, new, hbm, hbm],
        out_specs=qblk,
        out_shape=jax.ShapeDtypeStruct((bt * t, WIDTH), F32),
        scratch_shapes=[pltpu.VMEM((3, WIDTH, tkp), F32), pltpu.VMEM((3, WIDTH, tkp), F32),
                        pltpu.SemaphoreType.DMA((2, 3)),
                        pltpu.VMEM((SB_HEADS, t, SB_HEAD), F32),
                        pltpu.VMEM((SB_HEADS * t, SB_LANES), F32)],
        compiler_params=_params("arbitrary"),
        name="sb_past",
    )(q2d, k_t, v_t, past_k_t, past_v_t)


def _merge_ffn_kernel(x_ref, oa_ref, ob_ref, oc_ref, od_ref, gate_ref, wb_ref, wout_ref, nffn_ref,
                      wg_ref, wu_ref, wd_ref, y_ref):
    merged = None
    for n, o_ref in enumerate((oa_ref, ob_ref, oc_ref, od_ref)):
        term = gate_ref[:, n * D_MODEL:(n + 1) * D_MODEL].astype(F32) * _mm(o_ref[...], wb_ref[n])
        merged = term if merged is None else merged + term
    x = x_ref[...] + _mm(merged, wout_ref[...])
    h = x * lax.rsqrt(jnp.mean(x * x, axis=-1, keepdims=True) + RMS_EPS) * nffn_ref[...]
    hb = h.astype(BF16)
    gate = jnp.dot(hb, wg_ref[...], preferred_element_type=F32)
    up = jnp.dot(hb, wu_ref[...], preferred_element_type=F32)
    y_ref[...] = x + _mm(gate * _sigmoid(gate) * up, wd_ref[...])


def _merge_ffn(x2d, oa, ob, oc, od, gates, p, tm, nt):
    n = x2d.shape[0]
    row = lambda width: pl.BlockSpec((tm, width), lambda i: (i, 0))
    oa_spec = row(WIDTH) if nt is None else _time_major_spec(tm, nt)
    return pl.pallas_call(
        _merge_ffn_kernel,
        grid=(n // tm,),
        in_specs=[row(D_MODEL), oa_spec, row(WIDTH), row(WIDTH), row(WIDTH), row(GATE_COLS),
                  _const_spec((N_BRANCH, WIDTH, D_MODEL)), _const_spec((D_MODEL, D_MODEL)),
                  _const_spec((1, D_MODEL)), _const_spec((D_MODEL, D_FF)),
                  _const_spec((D_MODEL, D_FF)), _const_spec((D_FF, D_MODEL))],
        out_specs=row(D_MODEL),
        out_shape=jax.ShapeDtypeStruct((n, D_MODEL), F32),
        compiler_params=_params("parallel"),
        name="merge_ffn",
    )(x2d, oa, ob, oc, od, gates, p["w_branch"], p["w_out"], p["norm_ffn"], p["w_ffn_gate"],
      p["w_ffn_up"], p["w_ffn_down"])


def _prep_layer(w):
    p = {}
    w_in = w["w_in"]
    o_gla = WIDTH
    o_alat = o_gla + 2 * GLA_QK + 2 * WIDTH
    o_rwkv = o_alat + GLA_RANK
    o_sb = o_rwkv + RWKV_COLS
    o_sbk = o_sb + WIDTH
    o_gate = o_sb + SB_COLS
    pad = jnp.zeros((D_MODEL, 128 - GLA_RANK), w_in.dtype)
    p["w_in"] = jnp.concatenate(
        [w_in[:, :o_rwkv], pad, w_in[:, o_rwkv:o_sbk], w_in[:, o_gate:]], axis=1).astype(BF16)
    p["w_kv_t"] = w_in[:, o_sbk:o_gate].T.astype(BF16)
    p["norm_mix"] = w["norm_mix"].reshape(1, D_MODEL)

    lam = lax.complex(w["s5_a_re"], w["s5_a_im"])
    dt = jnp.exp(w["s5_log_dt"])[:, None]
    a_bar = jnp.exp(lam * dt)
    b_bar = ((a_bar - 1.0) / lam)[..., None] * lax.complex(w["s5_b_re"], w["s5_b_im"])
    eye = jnp.eye(S5_GROUPS, dtype=F32)

    def in_map(m):
        return jnp.einsum("gpc,gh->gchp", m, eye).reshape(WIDTH, S5_FLAT)

    def out_map(m):
        return jnp.einsum("gcp,gh->gphc", m, eye).reshape(S5_FLAT, WIDTH)

    p["s5_bbig"] = jnp.concatenate([in_map(b_bar.real), in_map(b_bar.imag)], axis=1).astype(BF16)
    p["s5_cbig"] = jnp.concatenate([out_map(w["s5_c_re"]), out_map(-w["s5_c_im"])], axis=0).astype(BF16)
    p["s5_ar"] = a_bar.real.reshape(1, S5_FLAT)
    p["s5_ai"] = a_bar.imag.reshape(1, S5_FLAT)
    p["s5_d"] = w["s5_d"].reshape(1, WIDTH)
    p["s5_w_glu"] = w["s5_w_glu"].astype(BF16)
    p["s5_b_glu"] = w["s5_b_glu"].reshape(1, WIDTH)

    p["gla_w_alpha"] = jnp.concatenate(
        [w["gla_w_alpha"], jnp.zeros((128 - GLA_RANK, GLA_QK), F32)], axis=0).astype(BF16)
    p["gla_b_alpha"] = w["gla_b_alpha"].reshape(1, GLA_QK)
    p["gla_norm"] = w["gla_norm"].reshape(1, WIDTH)

    for name in ("rwkv_w0", "rwkv_a0", "rwkv_k_k", "rwkv_k_a", "rwkv_r_k", "rwkv_ln_w", "rwkv_ln_b"):
        p[name] = w[name].reshape(1, WIDTH)
    p["rwkv_mu"] = w["rwkv_mu"].reshape(1, RWKV_COLS)
    for name in ("rwkv_w2", "rwkv_a2", "rwkv_g2"):
        p[name] = w[name].astype(BF16)
    head = jnp.arange(WIDTH) // RWKV_HEAD
    p["head_ones"] = (head[:, None] == head[None, :]).astype(BF16)

    p["sb_q_norm"] = jnp.tile(w["sb_q_norm"] * (SB_HEAD ** -0.5), SB_HEADS).reshape(1, WIDTH)
    p["sb_k_norm"] = jnp.tile(w["sb_k_norm"], SB_HEADS).reshape(WIDTH, 1)

    p["w_branch"] = w["w_branch"].astype(BF16)
    p["w_out"] = w["w_out"].astype(BF16)
    p["norm_ffn"] = w["norm_ffn"].reshape(1, D_MODEL)
    for name in ("w_ffn_gate", "w_ffn_up", "w_ffn_down"):
        p[name] = w[name].astype(BF16)
    return p


def _pick(n, prefs):
    for c in prefs:
        if n % c == 0:
            return c
    return n


def _layer(x, past_k_t, past_v_t, layer, s5_0, gla_0, rwkv_0, shift_0, p):
    bt, t, _ = x.shape
    n = bt * t
    x2d = x.reshape(n, D_MODEL)
    tm = _pick(n, (256, 128, 64, 32, 16, 8))
    nt = t // tm if t % tm == 0 else None
    assert nt is not None or tm % t == 0
    tm_in = INPROJ_ROWS if nt is not None and t % INPROJ_ROWS == 0 else tm
    u, c_gla, c_rwkv, q, k_t, v_t, gates = _inproj(x2d, p, bt, t, tm_in,
                                                   None if nt is None else t // tm_in)

    if nt is None:
        u = u.reshape(bt, t, WIDTH).transpose(1, 0, 2)
    x0 = jnp.concatenate([s5_0[..., 0].reshape(bt, S5_FLAT), s5_0[..., 1].reshape(bt, S5_FLAT)], axis=1)
    lc = _pick(t, (64, 32, 16, 8))
    o_a, xfin = _s5(u.reshape(n, WIDTH), x0, p, bt, lc)
    if nt is None:
        o_a = o_a.reshape(t, bt, WIDTH).transpose(1, 0, 2).reshape(n, WIDTH)
    else:
        o_a = o_a.reshape(t, bt * WIDTH)
    s5_new = jnp.stack([xfin[:, :S5_FLAT].reshape(bt, S5_GROUPS, S5_STATE),
                        xfin[:, S5_FLAT:].reshape(bt, S5_GROUPS, S5_STATE)], axis=-1)

    chunk = _pick(t, (64, 32, 16, 8))
    nseq = _pick(bt, (RWKV_ROWS // chunk, 4, 2, 1))
    ngrp = _pick(bt // nseq, (RWKV_GROUPS, 1))
    o_b, gla_t = _gla(c_gla.reshape(bt, t, GLA_COLS_PAD), jnp.swapaxes(gla_0, 2, 3), p, ngrp, nseq,
                      chunk)
    gla_new = jnp.swapaxes(gla_t, 2, 3)

    o_c, shift_new, rwkv_new = _rwkv(c_rwkv.reshape(bt, t, RWKV_COLS),
                                     shift_0.reshape(bt, 1, RWKV_COLS), rwkv_0, p, ngrp, nseq, chunk)

    if past_k_t is None:
        o_d = _sb_self(q, k_t, v_t, _pick(t, (256, 128, 64, 32, 16, 8)))
    else:
        tkp = _pick(past_k_t.shape[3], (256, 128))
        o_d = _sb_past(q, k_t, v_t, past_k_t, past_v_t, layer, tkp)

    y = _merge_ffn(x2d, o_a, o_b.reshape(n, WIDTH), o_c.reshape(n, WIDTH), o_d, gates, p, tm_in,
                   None if nt is None else t // tm_in)

    def cache_rows(a_t):
        return a_t.reshape(bt, SB_HEADS, SB_HEAD, t).transpose(0, 3, 1, 2)

    states = (cache_rows(k_t), cache_rows(v_t), s5_new, gla_new, rwkv_new,
              shift_new.reshape(bt, RWKV_COLS))
    return y.reshape(bt, t, D_MODEL), states


def _keys_on_lanes(cache):
    d, b, pl_, _, _ = cache.shape
    return cache.transpose(0, 1, 3, 4, 2).reshape(d, b, WIDTH, pl_)


def _trunk(x, past_k, past_v, s5_0, gla_0, rwkv_0, shift_0, layers):
    past_k_t = None if past_k is None else _keys_on_lanes(past_k)
    past_v_t = None if past_v is None else _keys_on_lanes(past_v)
    per_layer = []
    for l, p in enumerate(layers):
        x, st = _layer(x, past_k_t, past_v_t, l, s5_0[l], gla_0[l], rwkv_0[l], shift_0[l], p)
        per_layer.append(st)
    return x, [jnp.stack([st[i] for st in per_layer]) for i in range(6)]


def kernel(x_prompt, x_sample, cache_sb_k, cache_sb_v, state_s5, state_gla, state_rwkv, state_rwkv_shift, norm_mix, w_in, s5_a_re, s5_a_im, s5_log_dt, s5_b_re, s5_b_im, s5_c_re, s5_c_im, s5_d, s5_w_glu, s5_b_glu, gla_w_alpha, gla_b_alpha, gla_norm, rwkv_mu, rwkv_w0, rwkv_w2, rwkv_a0, rwkv_a2, rwkv_g2, rwkv_k_k, rwkv_k_a, rwkv_r_k, rwkv_ln_w, rwkv_ln_b, sb_q_norm, sb_k_norm, w_branch, w_out, norm_ffn, w_ffn_gate, w_ffn_up, w_ffn_down):
    weights = dict(norm_mix=norm_mix, w_in=w_in, s5_a_re=s5_a_re, s5_a_im=s5_a_im, s5_log_dt=s5_log_dt,
                   s5_b_re=s5_b_re, s5_b_im=s5_b_im, s5_c_re=s5_c_re, s5_c_im=s5_c_im, s5_d=s5_d,
                   s5_w_glu=s5_w_glu, s5_b_glu=s5_b_glu, gla_w_alpha=gla_w_alpha, gla_b_alpha=gla_b_alpha,
                   gla_norm=gla_norm, rwkv_mu=rwkv_mu, rwkv_w0=rwkv_w0, rwkv_w2=rwkv_w2, rwkv_a0=rwkv_a0,
                   rwkv_a2=rwkv_a2, rwkv_g2=rwkv_g2, rwkv_k_k=rwkv_k_k, rwkv_k_a=rwkv_k_a, rwkv_r_k=rwkv_r_k,
                   rwkv_ln_w=rwkv_ln_w, rwkv_ln_b=rwkv_ln_b, sb_q_norm=sb_q_norm, sb_k_norm=sb_k_norm,
                   w_branch=w_branch, w_out=w_out, norm_ffn=norm_ffn, w_ffn_gate=w_ffn_gate,
                   w_ffn_up=w_ffn_up, w_ffn_down=w_ffn_down)
    depth = w_in.shape[0]
    layers = [_prep_layer({name: arr[l] for name, arr in weights.items()}) for l in range(depth)]

    bp = x_prompt.shape[0]
    y_prompt, p_states = _trunk(
        x_prompt, None, None,
        jnp.zeros((depth, bp, S5_GROUPS, S5_STATE, 2), F32),
        jnp.zeros((depth, bp, GLA_HEADS, GLA_DK, GLA_DV), F32),
        jnp.zeros((depth, bp, RWKV_HEADS, RWKV_HEAD, RWKV_HEAD), F32),
        jnp.zeros((depth, bp, RWKV_COLS), F32), layers)
    y_sample, s_states = _trunk(x_sample, cache_sb_k, cache_sb_v, state_s5, state_gla, state_rwkv,
                                state_rwkv_shift, layers)
    return (y_prompt, y_sample, *p_states, *s_states)
```

```python
import functools
import math

import jax
import jax.numpy as jnp
from jax import lax
from jax.experimental import pallas as pl
from jax.experimental.pallas import tpu as pltpu

F32 = jnp.float32
BF16 = jnp.bfloat16

D_MODEL = 1024
WIDTH = 256
N_BRANCH = 4
S5_GROUPS, S5_GROUP, S5_STATE = 16, 16, 64
S5_FLAT = S5_GROUPS * S5_STATE
GLA_HEADS, GLA_DK, GLA_DV, GLA_RANK, GLA_TAU = 4, 32, 64, 16, 16.0
GLA_QK = GLA_HEADS * GLA_DK
GLA_COLS_PAD = 2 * GLA_QK + 2 * WIDTH + 128
RWKV_HEADS, RWKV_HEAD = 4, 64
RWKV_COLS = 1024
RWKV_GN_EPS = 64e-5
RWKV_ROWS = 256
RWKV_GROUPS = 2
SB_HEADS, SB_HEAD = 4, 64
SB_COLS = 3 * WIDTH
GATE_COLS = N_BRANCH * D_MODEL
D_FF = 2816
RMS_EPS = 1e-6

_C_S5 = 0
_C_GLA = _C_S5 + WIDTH
_C_RWKV = _C_GLA + GLA_COLS_PAD
_C_SBQ = _C_RWKV + RWKV_COLS
_C_GATE = _C_SBQ + WIDTH
_C_END = _C_GATE + GATE_COLS

VMEM_LIMIT = 56 * 1024 * 1024
INPROJ_ROWS = 512


def _params(*sem):
    return pltpu.CompilerParams(dimension_semantics=sem, vmem_limit_bytes=VMEM_LIMIT)


def _const_spec(shape):
    nd = len(shape)
    return pl.BlockSpec(shape, lambda *_: (0,) * nd, pipeline_mode=pl.Buffered(1))


def _mm(a, b):
    return jnp.dot(a.astype(BF16), b.astype(BF16), preferred_element_type=F32)


def _mm_nt(a, b):
    return lax.dot_general(a.astype(BF16), b.astype(BF16), (((1,), (1,)), ((), ())),
                           preferred_element_type=F32)


def _mm_tn(a, b):
    return lax.dot_general(a.astype(BF16), b.astype(BF16), (((0,), (0,)), ((), ())),
                           preferred_element_type=F32)


def _split(a):
    bits = lax.bitcast_convert_type(a, jnp.uint32) & jnp.uint32(0xFFFF0000)
    hi = lax.bitcast_convert_type(bits, F32)
    return hi.astype(BF16), (a - hi).astype(BF16)


def _mm_split_lhs(a, b01):
    hi, lo = _split(a)
    return (jnp.dot(hi, b01, preferred_element_type=F32)
            + jnp.dot(lo, b01, preferred_element_type=F32))


def _mm_split_rhs(a01, b):
    hi, lo = _split(b)
    return (jnp.dot(a01, hi, preferred_element_type=F32)
            + jnp.dot(a01, lo, preferred_element_type=F32))


def _log_sigmoid(z):
    return jnp.minimum(z, 0.0) - jnp.log1p(jnp.exp(-jnp.abs(z)))


def _sigmoid(z):
    return 1.0 / (1.0 + jnp.exp(-z))


def _tri(n, strict=False):
    r = lax.broadcasted_iota(jnp.int32, (n, n), 0)
    c = lax.broadcasted_iota(jnp.int32, (n, n), 1)
    return (r > c) if strict else (r >= c)


def _inproj_kernel(x_ref, g_ref, w_ref, wkv_ref, qn_ref, kn_ref, ones_ref,
                   s5_ref, gla_ref, rwkv_ref, q_ref, kt_ref, vt_ref, gate_ref, *, nseq):
    x = x_ref[...]
    h = x * lax.rsqrt(jnp.mean(x * x, axis=-1, keepdims=True) + RMS_EPS) * g_ref[...]
    hb = h.astype(BF16)

    def mm(lo, hi):
        return jnp.dot(hb, w_ref[:, lo:hi], preferred_element_type=F32)

    s5_ref[...] = mm(_C_S5, _C_GLA)
    gla_ref[...] = mm(_C_GLA, _C_RWKV)
    rwkv_ref[...] = mm(_C_RWKV, _C_SBQ)
    for n in range(N_BRANCH):
        lo = _C_GATE + n * D_MODEL
        gate_ref[:, n * D_MODEL:(n + 1) * D_MODEL] = _sigmoid(mm(lo, lo + D_MODEL)).astype(BF16)

    q = mm(_C_SBQ, _C_GATE)
    q_ms = _mm_split_lhs(q * q, ones_ref[...]) * (1.0 / SB_HEAD)
    q_ref[...] = (q * lax.rsqrt(q_ms + RMS_EPS) * qn_ref[...]).astype(BF16)

    kv_t = lax.dot_general(wkv_ref[...], hb, (((1,), (1,)), ((), ())), preferred_element_type=F32)
    k_t = []
    for hd in range(SB_HEADS):
        kh = kv_t[SB_HEAD * hd:SB_HEAD * (hd + 1)]
        k_t.append(kh * lax.rsqrt(jnp.mean(kh * kh, axis=0, keepdims=True) + RMS_EPS))
    k_t = jnp.concatenate(k_t, axis=0) * kn_ref[...]
    v_t = kv_t[WIDTH:2 * WIDTH]
    t = kt_ref.shape[2]
    for s in range(nseq):
        kt_ref[s] = k_t[:, t * s:t * (s + 1)]
        vt_ref[s] = v_t[:, t * s:t * (s + 1)]


def _time_major_spec(tm, nt):
    return pl.BlockSpec((tm, WIDTH), lambda i: (i % nt, i // nt))


def _inproj(x2d, p, bt, t, tm, nt):
    n = x2d.shape[0]
    row = lambda width: pl.BlockSpec((tm, width), lambda i: (i, 0))
    if nt is None:
        nseq = tm // t
        s5_spec, s5_shape = row(WIDTH), (n, WIDTH)
        kv_spec = pl.BlockSpec((nseq, WIDTH, t), lambda i: (i, 0, 0))
    else:
        nseq = 1
        s5_spec, s5_shape = _time_major_spec(tm, nt), (t, bt * WIDTH)
        kv_spec = pl.BlockSpec((1, WIDTH, tm), lambda i: (i // nt, 0, i % nt))
    kv_shape = jax.ShapeDtypeStruct((bt, WIDTH, t), F32)
    return pl.pallas_call(
        functools.partial(_inproj_kernel, nseq=nseq),
        grid=(n // tm,),
        in_specs=[row(D_MODEL), _const_spec((1, D_MODEL)), _const_spec((D_MODEL, _C_END)),
                  _const_spec((2 * WIDTH, D_MODEL)), _const_spec((1, WIDTH)),
                  _const_spec((WIDTH, 1)), _const_spec((WIDTH, WIDTH))],
        out_specs=(s5_spec, row(GLA_COLS_PAD), row(RWKV_COLS), row(WIDTH), kv_spec, kv_spec,
                   row(GATE_COLS)),
        out_shape=(jax.ShapeDtypeStruct(s5_shape, F32),
                   jax.ShapeDtypeStruct((n, GLA_COLS_PAD), F32),
                   jax.ShapeDtypeStruct((n, RWKV_COLS), F32),
                   jax.ShapeDtypeStruct((n, WIDTH), BF16), kv_shape, kv_shape,
                   jax.ShapeDtypeStruct((n, GATE_COLS), BF16)),
        compiler_params=_params("parallel"),
        name="inproj",
    )(x2d, p["norm_mix"], p["w_in"], p["w_kv_t"], p["sb_q_norm"], p["sb_k_norm"], p["head_ones"])


def _s5_kernel(u_ref, x0_ref, bbig_ref, cbig_ref, ar_ref, ai_ref, d_ref, wglu_ref, bglu_ref,
               o_ref, xfin_ref, bu_scr, x_scr, *, nb, lc):
    @pl.when(pl.program_id(0) == 0)
    def _():
        x_scr[...] = x0_ref[...]

    u = u_ref[...]
    bu_scr[...] = _mm(u, bbig_ref[...])
    ar = jnp.broadcast_to(ar_ref[...], (nb, S5_FLAT))
    ai = jnp.broadcast_to(ai_ref[...], (nb, S5_FLAT))

    def step(t, carry):
        xr, xi = carry
        rows = pl.ds(pl.multiple_of(t * nb, nb), nb)
        nxr = ar * xr - ai * xi + bu_scr[rows, 0:S5_FLAT]
        nxi = ar * xi + ai * xr + bu_scr[rows, S5_FLAT:2 * S5_FLAT]
        bu_scr[rows, 0:S5_FLAT] = nxr
        bu_scr[rows, S5_FLAT:2 * S5_FLAT] = nxi
        return nxr, nxi

    xr, xi = lax.fori_loop(0, lc, step, (x_scr[:, 0:S5_FLAT], x_scr[:, S5_FLAT:2 * S5_FLAT]))
    x_scr[:, 0:S5_FLAT] = xr
    x_scr[:, S5_FLAT:2 * S5_FLAT] = xi
    xfin_ref[...] = x_scr[...]

    y = _mm(bu_scr[...], cbig_ref[...]) + d_ref[...] * u
    y = jax.nn.gelu(y)
    o_ref[...] = y * _sigmoid(_mm(y, wglu_ref[...]) + bglu_ref[...])


def _s5(u_tm, x0, p, nb, lc):
    rows = u_tm.shape[0]
    blk = lc * nb
    kern = functools.partial(_s5_kernel, nb=nb, lc=lc)
    return pl.pallas_call(
        kern,
        grid=(rows // blk,),
        in_specs=[pl.BlockSpec((blk, WIDTH), lambda c: (c, 0)),
                  _const_spec((nb, 2 * S5_FLAT)),
                  _const_spec((WIDTH, 2 * S5_FLAT)), _const_spec((2 * S5_FLAT, WIDTH)),
                  _const_spec((1, S5_FLAT)), _const_spec((1, S5_FLAT)), _const_spec((1, WIDTH)),
                  _const_spec((WIDTH, WIDTH)), _const_spec((1, WIDTH))],
        out_specs=(pl.BlockSpec((blk, WIDTH), lambda c: (c, 0)),
                   pl.BlockSpec((nb, 2 * S5_FLAT), lambda c: (0, 0))),
        out_shape=(jax.ShapeDtypeStruct((rows, WIDTH), F32),
                   jax.ShapeDtypeStruct((nb, 2 * S5_FLAT), F32)),
        scratch_shapes=[pltpu.VMEM((blk, 2 * S5_FLAT), F32), pltpu.VMEM((nb, 2 * S5_FLAT), F32)],
        compiler_params=_params("arbitrary"),
        name="s5",
    )(u_tm, x0, p["s5_bbig"], p["s5_cbig"], p["s5_ar"], p["s5_ai"], p["s5_d"], p["s5_w_glu"],
      p["s5_b_glu"])


def _gla_kernel(z_ref, s0_ref, walpha_ref, balpha_ref, gnorm_ref, ones_ref, o_ref, sfin_ref, s_scr,
                *, ng, nb, L):
    R = nb * L
    groups = range(ng)
    heads = range(GLA_HEADS)

    @pl.when(pl.program_id(1) == 0)
    def _():
        s_scr[...] = s0_ref[...]

    rr = lax.broadcasted_iota(jnp.int32, (R, R), 0)
    cc = lax.broadcasted_iota(jnp.int32, (R, R), 1)
    tri = jnp.logical_and((rr & -L) == (cc & -L), rr >= cc)
    tri_ones = tri.astype(BF16)
    qk_lane = lax.broadcasted_iota(jnp.int32, (1, GLA_QK), 1) & -GLA_DK
    v_lane = lax.broadcasted_iota(jnp.int32, (1, WIDTH), 1) & -GLA_DV
    head_block = ((lax.broadcasted_iota(jnp.int32, (WIDTH, GLA_QK), 0) & -GLA_DV) * GLA_DK
                  == (lax.broadcasted_iota(jnp.int32, (WIDTH, GLA_QK), 1) & -GLA_DK) * GLA_DV)

    def prologue(gi):
        z = z_ref[gi * nb:(gi + 1) * nb].reshape(R, GLA_COLS_PAD)
        q = z[:, 0:GLA_QK] * (GLA_DK ** -0.5)
        k = z[:, GLA_QK:2 * GLA_QK]
        alat = z[:, 2 * GLA_QK + 2 * WIDTH:]
        log_a = _log_sigmoid(_mm(alat, walpha_ref[...]) + balpha_ref[...]) * (1.0 / GLA_TAU)
        b = _mm_split_rhs(tri_ones, log_a)
        b_last = [b[L * (i + 1) - 1:L * (i + 1), :] for i in range(nb)]
        b_end = jnp.concatenate([jnp.broadcast_to(be, (L, GLA_QK)) for be in b_last], axis=0)
        return dict(v=z[:, 2 * GLA_QK:2 * GLA_QK + WIDTH],
                    g=z[:, 2 * GLA_QK + WIDTH:2 * GLA_QK + 2 * WIDTH], b_last=b_last,
                    qt=q * jnp.exp(b), kt=(k * jnp.exp(-b)).astype(BF16),
                    kd=(k * jnp.exp(b_end - b)).astype(BF16))

    pro = [prologue(gi) for gi in groups]
    units = [(gi, h) for gi in groups for h in heads]
    attn = [jnp.where(tri, _mm_nt(jnp.where(qk_lane == GLA_DK * h, pro[gi]["qt"], 0.0),
                                  pro[gi]["kt"]), 0.0).astype(BF16) for gi, h in units]
    o_h = [jnp.dot(attn[u], jnp.where(v_lane == GLA_DV * h, pro[gi]["v"], 0.0).astype(BF16),
                   preferred_element_type=F32) for u, (gi, h) in enumerate(units)]
    for gi in groups:
        qt, kd, v = pro[gi]["qt"].astype(BF16), pro[gi]["kd"], pro[gi]["v"].astype(BF16)
        from_s = []
        for i in range(nb):
            seq = gi * nb + i
            rows = slice(L * i, L * (i + 1))
            st = s_scr[seq]
            from_s.append(_mm_nt(qt[rows], st))
            update = lax.dot_general(v[rows], kd[rows], (((0,), (0,)), ((), ())),
                                     preferred_element_type=F32)
            s_scr[seq] = st * jnp.exp(pro[gi]["b_last"][i]) + jnp.where(head_block, update, 0.0)
        o = jnp.concatenate(from_s, axis=0)
        for h in heads:
            o = o + o_h[gi * GLA_HEADS + h]
        ms = _mm_split_lhs(o * o, ones_ref[...]) * (1.0 / GLA_DV)
        g = pro[gi]["g"]
        o_ref[gi * nb:(gi + 1) * nb] = (o * lax.rsqrt(ms + RMS_EPS) * gnorm_ref[...]
                                        * (g * _sigmoid(g))).reshape(nb, L, WIDTH)

    @pl.when(pl.program_id(1) == pl.num_programs(1) - 1)
    def _():
        sfin_ref[...] = s_scr[...]


def _gla(cols, s0, p, ng, nb, L):
    bt, t, _ = cols.shape
    ns = ng * nb
    assert bt % ns == 0 and t % L == 0 and L & (L - 1) == 0
    eye = jnp.eye(GLA_HEADS, dtype=F32)
    s0_blocks = jnp.einsum("bhcv,hg->bhvgc", s0, eye).reshape(bt, WIDTH, GLA_QK)
    kern = functools.partial(_gla_kernel, ng=ng, nb=nb, L=L)
    state = pl.BlockSpec((ns, WIDTH, GLA_QK), lambda b, n: (b, 0, 0))
    o, s_blocks = pl.pallas_call(
        kern,
        grid=(bt // ns, t // L),
        in_specs=[pl.BlockSpec((ns, L, GLA_COLS_PAD), lambda b, n: (b, n, 0)), state,
                  _const_spec((128, GLA_QK)), _const_spec((1, GLA_QK)), _const_spec((1, WIDTH)),
                  _const_spec((WIDTH, WIDTH))],
        out_specs=(pl.BlockSpec((ns, L, WIDTH), lambda b, n: (b, n, 0)), state),
        out_shape=(jax.ShapeDtypeStruct((bt, t, WIDTH), F32),
                   jax.ShapeDtypeStruct((bt, WIDTH, GLA_QK), F32)),
        scratch_shapes=[pltpu.VMEM((ns, WIDTH, GLA_QK), F32)],
        compiler_params=_params("parallel", "arbitrary"),
        name="gla",
    )(cols, s0_blocks, p["gla_w_alpha"], p["gla_b_alpha"], p["gla_norm"], p["head_ones"])
    s5d = s_blocks.reshape(bt, GLA_HEADS, GLA_DV, GLA_HEADS, GLA_DK)
    s_new = jnp.stack([s5d[:, h, :, h, :] for h in range(GLA_HEADS)], axis=1)
    return o, jnp.swapaxes(s_new, 2, 3)


def _rwkv_kernel(z_ref, shift0_ref, s0_ref, mu_ref, w0_ref, w2_ref, a0_ref, a2_ref, g2_ref, kk_ref,
                 ka_ref, rk_ref, lnw_ref, lnb_ref, bd_ref, o_ref, shiftfin_ref, sfin_ref,
                 s_scr, prev_scr, *, ng, nb, L):
    R = nb * L
    HD = RWKV_HEAD
    groups = range(ng)
    heads = range(RWKV_HEADS)
    hsl = [slice(HD * h, HD * (h + 1)) for h in heads]

    @pl.when(pl.program_id(1) == 0)
    def _():
        for gi in groups:
            for i in range(nb):
                for h in heads:
                    s_scr[gi, h, :, HD * i:HD * (i + 1)] = s0_ref[gi * nb + i, h]
        prev_scr[...] = shift0_ref[...]

    rr = lax.broadcasted_iota(jnp.int32, (R, R), 0)
    cc = lax.broadcasted_iota(jnp.int32, (R, R), 1)
    same = (rr & -L) == (cc & -L)
    tri = jnp.logical_and(same, rr >= cc)
    stri = jnp.logical_and(same, rr > cc)
    tri_ones = tri.astype(BF16)
    own = ((lax.broadcasted_iota(jnp.int32, (R, nb * HD), 0) & -L) * HD
           == (lax.broadcasted_iota(jnp.int32, (R, nb * HD), 1) & -HD) * L)
    first = (lax.broadcasted_iota(jnp.int32, (R, RWKV_COLS), 0) & (L - 1)) == 0

    def spread(x):
        return jnp.where(own, jnp.concatenate([x] * nb, axis=1), 0.0).astype(BF16)

    pro = [_rwkv_prologue(gi, z_ref, prev_scr, shiftfin_ref, first, tri_ones, mu_ref, w0_ref, w2_ref,
                          a0_ref, a2_ref, g2_ref, kk_ref, ka_ref, bd_ref, nb, L) for gi in groups]

    units = [(gi, h) for gi in groups for h in heads]
    f32dot = functools.partial(jnp.dot, preferred_element_type=F32)
    vb = [pro[gi]["v"][:, hsl[h]].astype(BF16) for gi, h in units]
    gram = [_mm_nt(jnp.concatenate([pro[gi]["a_til"][:, hsl[h]], pro[gi]["r_til"][:, hsl[h]]], axis=0),
                   jnp.concatenate([pro[gi]["b_til"][:, hsl[h]], pro[gi]["k_til"][:, hsl[h]]], axis=0))
            for gi, h in units]
    from_s = [_mm_nt(jnp.concatenate([spread(pro[gi]["a_til"][:, hsl[h]]),
                                      spread(pro[gi]["r_til"][:, hsl[h]])], axis=0), s_scr[gi, h])
              for gi, h in units]
    pw = [jnp.where(stri, gm[:R, :R], 0.0).astype(BF16) for gm in gram]
    a_ak = [jnp.where(stri, gm[:R, R:], 0.0).astype(BF16) for gm in gram]
    a_r = [jnp.concatenate([jnp.where(tri, gm[R:, :R], 0.0).astype(BF16),
                            jnp.where(tri, gm[R:, R:], 0.0).astype(BF16)], axis=1) for gm in gram]
    n_units = range(len(units))
    sa = [from_s[u][:R] + f32dot(a_ak[u], vb[u]) for u in n_units]
    sa = [sa[u] + f32dot(pw[u], sa[u].astype(BF16)) for u in n_units]
    for _ in range(int(math.log2(L)) - 1):
        pw = [f32dot(pw[u], pw[u]).astype(BF16) for u in n_units]
        sa = [sa[u] + f32dot(pw[u], sa[u].astype(BF16)) for u in n_units]
    sa_v = [jnp.concatenate([sa[u].astype(BF16), vb[u]], axis=0) for u in n_units]
    y = [from_s[u][R:] + f32dot(a_r[u], sa_v[u]) for u in n_units]
    for u, (gi, h) in enumerate(units):
        p_all = jnp.concatenate([pe[:, hsl[h]] for pe in pro[gi]["p_last"]], axis=1)
        s_scr[gi, h] = s_scr[gi, h] * p_all + _mm_tn(
            sa_v[u], jnp.concatenate([spread(pro[gi]["b_end"][:, hsl[h]]),
                                      spread(pro[gi]["k_end"][:, hsl[h]])], axis=0))
    for gi in groups:
        outs = []
        for h in heads:
            yh = y[gi * RWKV_HEADS + h]
            hs = hsl[h]
            mean = jnp.mean(yh, axis=-1, keepdims=True)
            var = jnp.mean(jnp.square(yh - mean), axis=-1, keepdims=True)
            yn = (yh - mean) * lax.rsqrt(var + RWKV_GN_EPS) * lnw_ref[:, hs] + lnb_ref[:, hs]
            bonus = jnp.sum(pro[gi]["r"][:, hs] * pro[gi]["k"][:, hs] * rk_ref[:, hs], axis=-1,
                            keepdims=True)
            outs.append(yn + bonus * pro[gi]["v"][:, hs])
        o_ref[gi * nb:(gi + 1) * nb] = (jnp.concatenate(outs, axis=-1)
                                        * pro[gi]["g"]).reshape(nb, L, WIDTH)

    @pl.when(pl.program_id(1) == pl.num_programs(1) - 1)
    def _():
        for gi in groups:
            for i in range(nb):
                for h in heads:
                    sfin_ref[gi * nb + i, h] = s_scr[gi, h, :, HD * i:HD * (i + 1)]


def _rwkv_prologue(gi, z_ref, prev_scr, shiftfin_ref, first, tri_ones, mu_ref, w0_ref, w2_ref, a0_ref,
                   a2_ref, g2_ref, kk_ref, ka_ref, bd_ref, nb, L):
    R = nb * L
    seqs = range(gi * nb, (gi + 1) * nb)
    z = z_ref[gi * nb:(gi + 1) * nb].reshape(R, RWKV_COLS)
    carried = jnp.concatenate(
        [jnp.broadcast_to(prev_scr[s], (L, RWKV_COLS)) for s in seqs], axis=0)
    prev = jnp.where(first, carried, pltpu.roll(z, 1, axis=0))
    for i, s in enumerate(seqs):
        last = z[L * (i + 1) - 1:L * (i + 1), :]
        prev_scr[s] = last
        shiftfin_ref[s] = last
    zs = z + (prev - z) * mu_ref[...]
    r = zs[:, 0:256]
    k = zs[:, 256:512]
    v = zs[:, 512:768]
    w_lat = zs[:, 768:832]
    a_lat = zs[:, 832:896]
    g_lat = zs[:, 896:1024]
    w = -jax.nn.softplus(-(w0_ref[...] + _mm(jnp.tanh(w_lat), w2_ref[...]))) - 0.5
    log_w = -jnp.exp(w)
    a = _sigmoid(a0_ref[...] + _mm(a_lat, a2_ref[...]))
    g = _mm(_sigmoid(g_lat), g2_ref[...])
    kk = k * kk_ref[...]
    kk = kk / jnp.maximum(jnp.sqrt(_mm_split_lhs(kk * kk, bd_ref[...])), 1e-12)
    k = k * (1.0 + (a - 1.0) * ka_ref[...])
    cum = _mm_split_rhs(tri_ones, log_w)
    p_t = jnp.exp(cum)
    inv_p = jnp.exp(-cum)
    p_last = [p_t[L * (i + 1) - 1:L * (i + 1), :] for i in range(nb)]
    p_end = jnp.concatenate([jnp.broadcast_to(pe, (L, WIDTH)) for pe in p_last], axis=0)
    b_til = kk * a * inv_p
    k_til = k * inv_p
    return dict(r=r, k=k, v=v, g=g, p_last=p_last,
                a_til=-kk * jnp.exp(cum - log_w),
                b_til=b_til, k_til=k_til, r_til=r * p_t,
                b_end=b_til * p_end, k_end=k_til * p_end)


def _rwkv(cols, shift0, s0, p, ng, nb, L):
    bt, t, _ = cols.shape
    ns = ng * nb
    assert bt % ns == 0 and t % L == 0 and L & (L - 1) == 0
    kern = functools.partial(_rwkv_kernel, ng=ng, nb=nb, L=L)
    state = pl.BlockSpec((ns, RWKV_HEADS, RWKV_HEAD, RWKV_HEAD), lambda b, n: (b, 0, 0, 0))
    shift = pl.BlockSpec((ns, 1, RWKV_COLS), lambda b, n: (b, 0, 0))
    vec = _const_spec((1, WIDTH))
    return pl.pallas_call(
        kern,
        grid=(bt // ns, t // L),
        in_specs=[pl.BlockSpec((ns, L, RWKV_COLS), lambda b, n: (b, n, 0)), shift, state,
                  _const_spec((1, RWKV_COLS)), vec, _const_spec((64, WIDTH)), vec,
                  _const_spec((64, WIDTH)), _const_spec((128, WIDTH)), vec, vec, vec, vec, vec,
                  _const_spec((WIDTH, WIDTH))],
        out_specs=(pl.BlockSpec((ns, L, WIDTH), lambda b, n: (b, n, 0)), shift, state),
        out_shape=(jax.ShapeDtypeStruct((bt, t, WIDTH), F32),
                   jax.ShapeDtypeStruct((bt, 1, RWKV_COLS), F32),
                   jax.ShapeDtypeStruct((bt, RWKV_HEADS, RWKV_HEAD, RWKV_HEAD), F32)),
        scratch_shapes=[pltpu.VMEM((ng, RWKV_HEADS, RWKV_HEAD, nb * RWKV_HEAD), F32),
                        pltpu.VMEM((ns, 1, RWKV_COLS), F32)],
        compiler_params=_params("parallel", "arbitrary"),
        name="rwkv",
    )(cols, shift0, s0, p["rwkv_mu"], p["rwkv_w0"], p["rwkv_w2"], p["rwkv_a0"], p["rwkv_a2"],
      p["rwkv_g2"], p["rwkv_k_k"], p["rwkv_k_a"], p["rwkv_r_k"], p["rwkv_ln_w"], p["rwkv_ln_b"],
      p["head_ones"])


SB_DEAD_LOG = -110.0
SB_LANES = 128


def _sb_block(qh, k, v, diag, suffix_ones, acc_scr, carry_scr):
    tq = qh[0].shape[0]
    tk = k.shape[1]
    kb = k.astype(BF16)
    z = jnp.concatenate(
        [jnp.dot(qh[h], kb[SB_HEAD * h:SB_HEAD * (h + 1)], preferred_element_type=F32)
         for h in range(SB_HEADS)], axis=0)
    lm = -(jnp.maximum(z, 0.0) + jnp.log(1.0 + jnp.exp(-jnp.abs(z))))
    if diag:
        row = lax.broadcasted_iota(jnp.int32, z.shape, 0) & (tq - 1)
        mask = row > lax.broadcasted_iota(jnp.int32, z.shape, 1)
        lm = jnp.where(mask, lm, 0.0)
    sums = _mm_split_lhs(lm, suffix_ones)
    carry = carry_scr[...]
    if tk <= SB_LANES:
        carry_keys = carry[:, :tk]
    else:
        carry_keys = jnp.concatenate([carry] * (tk // SB_LANES), axis=1)
    wgt = jnp.exp(z + carry_keys + sums)
    if diag:
        wgt = jnp.where(mask, wgt, 0.0)
    wgt = wgt.astype(BF16)
    vb = v.astype(BF16)
    for h in range(SB_HEADS):
        acc_scr[h] += lax.dot_general(wgt[tq * h:tq * (h + 1)], vb[SB_HEAD * h:SB_HEAD * (h + 1)],
                                      (((1,), (1,)), ((), ())), preferred_element_type=F32)
    carry = carry + jnp.broadcast_to(sums[:, 0:1], carry.shape)
    carry_scr[...] = carry
    return jnp.max(carry)


def _suffix_ones(tk):
    r = lax.broadcasted_iota(jnp.int32, (tk, tk), 0)
    c = lax.broadcasted_iota(jnp.int32, (tk, tk), 1)
    return (r >= c).astype(BF16)


def _sb_heads(q):
    return [q[:, SB_HEAD * h:SB_HEAD * (h + 1)].astype(BF16) for h in range(SB_HEADS)]


def _sb_self_kernel(q_ref, k_ref, v_ref, o_ref, acc_scr, carry_scr, *, tq):
    i = pl.program_id(1)
    acc_scr[...] = jnp.zeros_like(acc_scr)
    carry_scr[...] = jnp.zeros_like(carry_scr)
    qh = _sb_heads(q_ref[...])
    ones = _suffix_ones(tq)

    def keys(j):
        return pl.ds(pl.multiple_of(j * tq, tq), tq)

    live = _sb_block(qh, k_ref[0, :, keys(i)], v_ref[0, :, keys(i)], True, ones, acc_scr, carry_scr)

    def body(state):
        j, _ = state
        m = _sb_block(qh, k_ref[0, :, keys(j)], v_ref[0, :, keys(j)], False, ones, acc_scr, carry_scr)
        return j - 1, m

    lax.while_loop(lambda s: jnp.logical_and(s[0] >= 0, s[1] > SB_DEAD_LOG), body, (i - 1, live))
    o_ref[...] = jnp.concatenate([acc_scr[h] for h in range(SB_HEADS)], axis=-1)


def _sb_self(q2d, k_t, v_t, tq):
    bt, _, t = k_t.shape
    nq = t // tq
    assert tq & (tq - 1) == 0
    kern = functools.partial(_sb_self_kernel, tq=tq)
    seq = pl.BlockSpec((1, WIDTH, t), lambda b, i: (b, 0, 0))
    blk = pl.BlockSpec((tq, WIDTH), lambda b, i: (b * nq + i, 0))
    return pl.pallas_call(
        kern,
        grid=(bt, nq),
        in_specs=[blk, seq, seq],
        out_specs=blk,
        out_shape=jax.ShapeDtypeStruct((bt * t, WIDTH), F32),
        scratch_shapes=[pltpu.VMEM((SB_HEADS, tq, SB_HEAD), F32),
                        pltpu.VMEM((SB_HEADS * tq, SB_LANES), F32)],
        compiler_params=_params("parallel", "arbitrary"),
        name="sb_self",
    )(q2d, k_t, v_t)


SB_OLDER_SLOT = 2


def _sb_past_kernel(q_ref, k_ref, v_ref, pk_hbm, pv_hbm, o_ref, kbuf, vbuf, sem, acc_scr, carry_scr,
                    *, layer, t, tkp, np_):
    b = pl.program_id(0)
    slot = b % 2
    qh = _sb_heads(q_ref[...])

    def fetch(seq, blk, dst):
        keys = pl.ds(pl.multiple_of(blk * tkp, tkp), tkp)
        return (pltpu.make_async_copy(pk_hbm.at[layer, seq, :, keys], kbuf.at[dst], sem.at[0, dst]),
                pltpu.make_async_copy(pv_hbm.at[layer, seq, :, keys], vbuf.at[dst], sem.at[1, dst]))

    @pl.when(b == 0)
    def _():
        for cp in fetch(0, np_ - 1, 0):
            cp.start()

    @pl.when(b + 1 < pl.num_programs(0))
    def _():
        for cp in fetch(b + 1, np_ - 1, 1 - slot):
            cp.start()

    acc_scr[...] = jnp.zeros_like(acc_scr)
    carry_scr[...] = jnp.zeros_like(carry_scr)
    _sb_block(qh, k_ref[0], v_ref[0], True, _suffix_ones(t), acc_scr, carry_scr)

    ones = _suffix_ones(tkp)
    for cp in fetch(b, np_ - 1, slot):
        cp.wait()
    live = _sb_block(qh, kbuf[slot], vbuf[slot], False, ones, acc_scr, carry_scr)

    def body(state):
        j, _ = state
        copies = fetch(b, j, SB_OLDER_SLOT)
        for cp in copies:
            cp.start()
        for cp in copies:
            cp.wait()
        m = _sb_block(qh, kbuf[SB_OLDER_SLOT], vbuf[SB_OLDER_SLOT], False, ones, acc_scr, carry_scr)
        return j - 1, m

    lax.while_loop(lambda s: jnp.logical_and(s[0] >= 0, s[1] > SB_DEAD_LOG), body, (np_ - 2, live))
    o_ref[...] = jnp.concatenate([acc_scr[h] for h in range(SB_HEADS)], axis=-1)


def _sb_past(q2d, k_t, v_t, past_k_t, past_v_t, layer, tkp):
    bt, _, t = k_t.shape
    np_ = past_k_t.shape[3] // tkp
    assert t & (t - 1) == 0 and past_k_t.shape[3] % tkp == 0
    kern = functools.partial(_sb_past_kernel, layer=layer, t=t, tkp=tkp, np_=np_)
    qblk = pl.BlockSpec((t, WIDTH), lambda b: (b, 0))
    new = pl.BlockSpec((1, WIDTH, t), lambda b: (b, 0, 0))
    hbm = pl.BlockSpec(memory_space=pl.ANY)
    return pl.pallas_call(
        kern,
        grid=(bt,),
        in_specs=[qblk, new, new, hbm, hbm],
        out_specs=qblk,
        out_shape=jax.ShapeDtypeStruct((bt * t, WIDTH), F32),
        scratch_shapes=[pltpu.VMEM((3, WIDTH, tkp), F32), pltpu.VMEM((3, WIDTH, tkp), F32),
                        pltpu.SemaphoreType.DMA((2, 3)),
                        pltpu.VMEM((SB_HEADS, t, SB_HEAD), F32),
                        pltpu.VMEM((SB_HEADS * t, SB_LANES), F32)],
        compiler_params=_params("arbitrary"),
        name="sb_past",
    )(q2d, k_t, v_t, past_k_t, past_v_t)


def _merge_ffn_kernel(x_ref, oa_ref, ob_ref, oc_ref, od_ref, gate_ref, wb_ref, wout_ref, nffn_ref,
                      wg_ref, wu_ref, wd_ref, y_ref):
    merged = None
    for n, o_ref in enumerate((oa_ref, ob_ref, oc_ref, od_ref)):
        term = gate_ref[:, n * D_MODEL:(n + 1) * D_MODEL].astype(F32) * _mm(o_ref[...], wb_ref[n])
        merged = term if merged is None else merged + term
    x = x_ref[...] + _mm(merged, wout_ref[...])
    h = x * lax.rsqrt(jnp.mean(x * x, axis=-1, keepdims=True) + RMS_EPS) * nffn_ref[...]
    hb = h.astype(BF16)
    gate = jnp.dot(hb, wg_ref[...], preferred_element_type=F32)
    up = jnp.dot(hb, wu_ref[...], preferred_element_type=F32)
    y_ref[...] = x + _mm(gate * _sigmoid(gate) * up, wd_ref[...])


def _merge_ffn(x2d, oa, ob, oc, od, gates, p, tm, nt):
    n = x2d.shape[0]
    row = lambda width: pl.BlockSpec((tm, width), lambda i: (i, 0))
    oa_spec = row(WIDTH) if nt is None else _time_major_spec(tm, nt)
    return pl.pallas_call(
        _merge_ffn_kernel,
        grid=(n // tm,),
        in_specs=[row(D_MODEL), oa_spec, row(WIDTH), row(WIDTH), row(WIDTH), row(GATE_COLS),
                  _const_spec((N_BRANCH, WIDTH, D_MODEL)), _const_spec((D_MODEL, D_MODEL)),
                  _const_spec((1, D_MODEL)), _const_spec((D_MODEL, D_FF)),
                  _const_spec((D_MODEL, D_FF)), _const_spec((D_FF, D_MODEL))],
        out_specs=row(D_MODEL),
        out_shape=jax.ShapeDtypeStruct((n, D_MODEL), F32),
        compiler_params=_params("parallel"),
        name="merge_ffn",
    )(x2d, oa, ob, oc, od, gates, p["w_branch"], p["w_out"], p["norm_ffn"], p["w_ffn_gate"],
      p["w_ffn_up"], p["w_ffn_down"])


def _prep_layer(w):
    p = {}
    w_in = w["w_in"]
    o_gla = WIDTH
    o_alat = o_gla + 2 * GLA_QK + 2 * WIDTH
    o_rwkv = o_alat + GLA_RANK
    o_sb = o_rwkv + RWKV_COLS
    o_sbk = o_sb + WIDTH
    o_gate = o_sb + SB_COLS
    pad = jnp.zeros((D_MODEL, 128 - GLA_RANK), w_in.dtype)
    p["w_in"] = jnp.concatenate(
        [w_in[:, :o_rwkv], pad, w_in[:, o_rwkv:o_sbk], w_in[:, o_gate:]], axis=1).astype(BF16)
    p["w_kv_t"] = w_in[:, o_sbk:o_gate].T.astype(BF16)
    p["norm_mix"] = w["norm_mix"].reshape(1, D_MODEL)

    lam = lax.complex(w["s5_a_re"], w["s5_a_im"])
    dt = jnp.exp(w["s5_log_dt"])[:, None]
    a_bar = jnp.exp(lam * dt)
    b_bar = ((a_bar - 1.0) / lam)[..., None] * lax.complex(w["s5_b_re"], w["s5_b_im"])
    eye = jnp.eye(S5_GROUPS, dtype=F32)

    def in_map(m):
        return jnp.einsum("gpc,gh->gchp", m, eye).reshape(WIDTH, S5_FLAT)

    def out_map(m):
        return jnp.einsum("gcp,gh->gphc", m, eye).reshape(S5_FLAT, WIDTH)

    p["s5_bbig"] = jnp.concatenate([in_map(b_bar.real), in_map(b_bar.imag)], axis=1).astype(BF16)
    p["s5_cbig"] = jnp.concatenate([out_map(w["s5_c_re"]), out_map(-w["s5_c_im"])], axis=0).astype(BF16)
    p["s5_ar"] = a_bar.real.reshape(1, S5_FLAT)
    p["s5_ai"] = a_bar.imag.reshape(1, S5_FLAT)
    p["s5_d"] = w["s5_d"].reshape(1, WIDTH)
    p["s5_w_glu"] = w["s5_w_glu"].astype(BF16)
    p["s5_b_glu"] = w["s5_b_glu"].reshape(1, WIDTH)

    p["gla_w_alpha"] = jnp.concatenate(
        [w["gla_w_alpha"], jnp.zeros((128 - GLA_RANK, GLA_QK), F32)], axis=0).astype(BF16)
    p["gla_b_alpha"] = w["gla_b_alpha"].reshape(1, GLA_QK)
    p["gla_norm"] = w["gla_norm"].reshape(1, WIDTH)

    for name in ("rwkv_w0", "rwkv_a0", "rwkv_k_k", "rwkv_k_a", "rwkv_r_k", "rwkv_ln_w", "rwkv_ln_b"):
        p[name] = w[name].reshape(1, WIDTH)
    p["rwkv_mu"] = w["rwkv_mu"].reshape(1, RWKV_COLS)
    for name in ("rwkv_w2", "rwkv_a2", "rwkv_g2"):
        p[name] = w[name].astype(BF16)
    head = jnp.arange(WIDTH) // RWKV_HEAD
    p["head_ones"] = (head[:, None] == head[None, :]).astype(BF16)

    p["sb_q_norm"] = jnp.tile(w["sb_q_norm"] * (SB_HEAD ** -0.5), SB_HEADS).reshape(1, WIDTH)
    p["sb_k_norm"] = jnp.tile(w["sb_k_norm"], SB_HEADS).reshape(WIDTH, 1)

    p["w_branch"] = w["w_branch"].astype(BF16)
    p["w_out"] = w["w_out"].astype(BF16)
    p["norm_ffn"] = w["norm_ffn"].reshape(1, D_MODEL)
    for name in ("w_ffn_gate", "w_ffn_up", "w_ffn_down"):
        p[name] = w[name].astype(BF16)
    return p


def _pick(n, prefs):
    for c in prefs:
        if n % c == 0:
            return c
    return n


def _layer(x, past_k_t, past_v_t, layer, s5_0, gla_0, rwkv_0, shift_0, p):
    bt, t, _ = x.shape
    n = bt * t
    x2d = x.reshape(n, D_MODEL)
    tm = _pick(n, (256, 128, 64, 32, 16, 8))
    nt = t // tm if t % tm == 0 else None
    assert nt is not None or tm % t == 0
    tm_in = INPROJ_ROWS if nt is not None and t % INPROJ_ROWS == 0 else tm
    u, c_gla, c_rwkv, q, k_t, v_t, gates = _inproj(x2d, p, bt, t, tm_in,
                                                   None if nt is None else t // tm_in)

    if nt is None:
        u = u.reshape(bt, t, WIDTH).transpose(1, 0, 2)
    x0 = jnp.concatenate([s5_0[..., 0].reshape(bt, S5_FLAT), s5_0[..., 1].reshape(bt, S5_FLAT)], axis=1)
    lc = _pick(t, (64, 32, 16, 8))
    o_a, xfin = _s5(u.reshape(n, WIDTH), x0, p, bt, lc)
    if nt is None:
        o_a = o_a.reshape(t, bt, WIDTH).transpose(1, 0, 2).reshape(n, WIDTH)
    else:
        o_a = o_a.reshape(t, bt * WIDTH)
    s5_new = jnp.stack([xfin[:, :S5_FLAT].reshape(bt, S5_GROUPS, S5_STATE),
                        xfin[:, S5_FLAT:].reshape(bt, S5_GROUPS, S5_STATE)], axis=-1)

    chunk = _pick(t, (64, 32, 16, 8))
    nseq = _pick(bt, (RWKV_ROWS // chunk, 4, 2, 1))
    ngrp = _pick(bt // nseq, (RWKV_GROUPS, 1))
    o_b, gla_new = _gla(c_gla.reshape(bt, t, GLA_COLS_PAD), gla_0, p, ngrp, nseq, chunk)

    o_c, shift_new, rwkv_new = _rwkv(c_rwkv.reshape(bt, t, RWKV_COLS),
                                     shift_0.reshape(bt, 1, RWKV_COLS), rwkv_0, p, ngrp, nseq, chunk)

    if past_k_t is None:
        o_d = _sb_self(q, k_t, v_t, _pick(t, (256, 128, 64, 32, 16, 8)))
    else:
        tkp = _pick(past_k_t.shape[3], (256, 128))
        o_d = _sb_past(q, k_t, v_t, past_k_t, past_v_t, layer, tkp)

    y = _merge_ffn(x2d, o_a, o_b.reshape(n, WIDTH), o_c.reshape(n, WIDTH), o_d, gates, p, tm_in,
                   None if nt is None else t // tm_in)

    def cache_rows(a_t):
        return a_t.reshape(bt, SB_HEADS, SB_HEAD, t).transpose(0, 3, 1, 2)

    states = (cache_rows(k_t), cache_rows(v_t), s5_new, gla_new, rwkv_new,
              shift_new.reshape(bt, RWKV_COLS))
    return y.reshape(bt, t, D_MODEL), states


def _keys_on_lanes(cache):
    d, b, pl_, _, _ = cache.shape
    return cache.transpose(0, 1, 3, 4, 2).reshape(d, b, WIDTH, pl_)


def _trunk(x, past_k, past_v, s5_0, gla_0, rwkv_0, shift_0, layers):
    past_k_t = None if past_k is None else _keys_on_lanes(past_k)
    past_v_t = None if past_v is None else _keys_on_lanes(past_v)
    per_layer = []
    for l, p in enumerate(layers):
        x, st = _layer(x, past_k_t, past_v_t, l, s5_0[l], gla_0[l], rwkv_0[l], shift_0[l], p)
        per_layer.append(st)
    return x, [jnp.stack([st[i] for st in per_layer]) for i in range(6)]


def kernel(x_prompt, x_sample, cache_sb_k, cache_sb_v, state_s5, state_gla, state_rwkv, state_rwkv_shift, norm_mix, w_in, s5_a_re, s5_a_im, s5_log_dt, s5_b_re, s5_b_im, s5_c_re, s5_c_im, s5_d, s5_w_glu, s5_b_glu, gla_w_alpha, gla_b_alpha, gla_norm, rwkv_mu, rwkv_w0, rwkv_w2, rwkv_a0, rwkv_a2, rwkv_g2, rwkv_k_k, rwkv_k_a, rwkv_r_k, rwkv_ln_w, rwkv_ln_b, sb_q_norm, sb_k_norm, w_branch, w_out, norm_ffn, w_ffn_gate, w_ffn_up, w_ffn_down):
    weights = dict(norm_mix=norm_mix, w_in=w_in, s5_a_re=s5_a_re, s5_a_im=s5_a_im, s5_log_dt=s5_log_dt,
                   s5_b_re=s5_b_re, s5_b_im=s5_b_im, s5_c_re=s5_c_re, s5_c_im=s5_c_im, s5_d=s5_d,
                   s5_w_glu=s5_w_glu, s5_b_glu=s5_b_glu, gla_w_alpha=gla_w_alpha, gla_b_alpha=gla_b_alpha,
                   gla_norm=gla_norm, rwkv_mu=rwkv_mu, rwkv_w0=rwkv_w0, rwkv_w2=rwkv_w2, rwkv_a0=rwkv_a0,
                   rwkv_a2=rwkv_a2, rwkv_g2=rwkv_g2, rwkv_k_k=rwkv_k_k, rwkv_k_a=rwkv_k_a, rwkv_r_k=rwkv_r_k,
                   rwkv_ln_w=rwkv_ln_w, rwkv_ln_b=rwkv_ln_b, sb_q_norm=sb_q_norm, sb_k_norm=sb_k_norm,
                   w_branch=w_branch, w_out=w_out, norm_ffn=norm_ffn, w_ffn_gate=w_ffn_gate,
                   w_ffn_up=w_ffn_up, w_ffn_down=w_ffn_down)
    depth = w_in.shape[0]
    layers = [_prep_layer({name: arr[l] for name, arr in weights.items()}) for l in range(depth)]

    bp = x_prompt.shape[0]
    y_prompt, p_states = _trunk(
        x_prompt, None, None,
        jnp.zeros((depth, bp, S5_GROUPS, S5_STATE, 2), F32),
        jnp.zeros((depth, bp, GLA_HEADS, GLA_DK, GLA_DV), F32),
        jnp.zeros((depth, bp, RWKV_HEADS, RWKV_HEAD, RWKV_HEAD), F32),
        jnp.zeros((depth, bp, RWKV_COLS), F32), layers)
    y_sample, s_states = _trunk(x_sample, cache_sb_k, cache_sb_v, state_s5, state_gla, state_rwkv,
                                state_rwkv_shift, layers)
    return (y_prompt, y_sample, *p_states, *s_states)
```

```python
import functools
import math

import jax
import jax.numpy as jnp
from jax import lax
from jax.experimental import pallas as pl
from jax.experimental.pallas import tpu as pltpu

F32 = jnp.float32
BF16 = jnp.bfloat16

D_MODEL = 1024
WIDTH = 256
N_BRANCH = 4
S5_GROUPS, S5_GROUP, S5_STATE = 16, 16, 64
S5_FLAT = S5_GROUPS * S5_STATE
GLA_HEADS, GLA_DK, GLA_DV, GLA_RANK, GLA_TAU = 4, 32, 64, 16, 16.0
GLA_QK = GLA_HEADS * GLA_DK
GLA_COLS_PAD = 2 * GLA_QK + 2 * WIDTH + 128
RWKV_HEADS, RWKV_HEAD = 4, 64
RWKV_COLS = 1024
RWKV_GN_EPS = 64e-5
RWKV_ROWS = 256
RWKV_GROUPS = 2
SB_HEADS, SB_HEAD = 4, 64
SB_COLS = 3 * WIDTH
GATE_COLS = N_BRANCH * D_MODEL
D_FF = 2816
RMS_EPS = 1e-6

_C_S5 = 0
_C_GLA = _C_S5 + WIDTH
_C_RWKV = _C_GLA + GLA_COLS_PAD
_C_SBQ = _C_RWKV + RWKV_COLS
_C_GATE = _C_SBQ + WIDTH
_C_END = _C_GATE + GATE_COLS

VMEM_LIMIT = 56 * 1024 * 1024
INPROJ_ROWS = 512


def _params(*sem):
    return pltpu.CompilerParams(dimension_semantics=sem, vmem_limit_bytes=VMEM_LIMIT)


def _const_spec(shape):
    nd = len(shape)
    return pl.BlockSpec(shape, lambda *_: (0,) * nd, pipeline_mode=pl.Buffered(1))


def _mm(a, b):
    return jnp.dot(a.astype(BF16), b.astype(BF16), preferred_element_type=F32)


def _mm_nt(a, b):
    return lax.dot_general(a.astype(BF16), b.astype(BF16), (((1,), (1,)), ((), ())),
                           preferred_element_type=F32)


def _mm_tn(a, b):
    return lax.dot_general(a.astype(BF16), b.astype(BF16), (((0,), (0,)), ((), ())),
                           preferred_element_type=F32)


def _split(a):
    bits = lax.bitcast_convert_type(a, jnp.uint32) & jnp.uint32(0xFFFF0000)
    hi = lax.bitcast_convert_type(bits, F32)
    return hi.astype(BF16), (a - hi).astype(BF16)


def _mm_split_lhs(a, b01):
    hi, lo = _split(a)
    return (jnp.dot(hi, b01, preferred_element_type=F32)
            + jnp.dot(lo, b01, preferred_element_type=F32))


def _mm_split_rhs(a01, b):
    hi, lo = _split(b)
    return (jnp.dot(a01, hi, preferred_element_type=F32)
            + jnp.dot(a01, lo, preferred_element_type=F32))


def _log_sigmoid(z):
    return jnp.minimum(z, 0.0) - jnp.log1p(jnp.exp(-jnp.abs(z)))


def _sigmoid(z):
    return 1.0 / (1.0 + jnp.exp(-z))


def _tri(n, strict=False):
    r = lax.broadcasted_iota(jnp.int32, (n, n), 0)
    c = lax.broadcasted_iota(jnp.int32, (n, n), 1)
    return (r > c) if strict else (r >= c)


def _inproj_kernel(x_ref, g_ref, w_ref, wkv_ref, qn_ref, kn_ref, ones_ref,
                   s5_ref, gla_ref, rwkv_ref, q_ref, kt_ref, vt_ref, gate_ref, *, nseq):
    x = x_ref[...]
    h = x * lax.rsqrt(jnp.mean(x * x, axis=-1, keepdims=True) + RMS_EPS) * g_ref[...]
    hb = h.astype(BF16)

    def mm(lo, hi):
        return jnp.dot(hb, w_ref[:, lo:hi], preferred_element_type=F32)

    s5_ref[...] = mm(_C_S5, _C_GLA)
    gla_ref[...] = mm(_C_GLA, _C_RWKV)
    rwkv_ref[...] = mm(_C_RWKV, _C_SBQ)
    for n in range(N_BRANCH):
        lo = _C_GATE + n * D_MODEL
        gate_ref[:, n * D_MODEL:(n + 1) * D_MODEL] = _sigmoid(mm(lo, lo + D_MODEL)).astype(BF16)

    q = mm(_C_SBQ, _C_GATE)
    q_ms = _mm_split_lhs(q * q, ones_ref[...]) * (1.0 / SB_HEAD)
    q_ref[...] = (q * lax.rsqrt(q_ms + RMS_EPS) * qn_ref[...]).astype(BF16)

    kv_t = lax.dot_general(wkv_ref[...], hb, (((1,), (1,)), ((), ())), preferred_element_type=F32)
    k_t = []
    for hd in range(SB_HEADS):
        kh = kv_t[SB_HEAD * hd:SB_HEAD * (hd + 1)]
        k_t.append(kh * lax.rsqrt(jnp.mean(kh * kh, axis=0, keepdims=True) + RMS_EPS))
    k_t = jnp.concatenate(k_t, axis=0) * kn_ref[...]
    v_t = kv_t[WIDTH:2 * WIDTH]
    t = kt_ref.shape[2]
    for s in range(nseq):
        kt_ref[s] = k_t[:, t * s:t * (s + 1)]
        vt_ref[s] = v_t[:, t * s:t * (s + 1)]


def _time_major_spec(tm, nt):
    return pl.BlockSpec((tm, WIDTH), lambda i: (i % nt, i // nt))


def _inproj(x2d, p, bt, t, tm, nt):
    n = x2d.shape[0]
    row = lambda width: pl.BlockSpec((tm, width), lambda i: (i, 0))
    if nt is None:
        nseq = tm // t
        s5_spec, s5_shape = row(WIDTH), (n, WIDTH)
        kv_spec = pl.BlockSpec((nseq, WIDTH, t), lambda i: (i, 0, 0))
    else:
        nseq = 1
        s5_spec, s5_shape = _time_major_spec(tm, nt), (t, bt * WIDTH)
        kv_spec = pl.BlockSpec((1, WIDTH, tm), lambda i: (i // nt, 0, i % nt))
    kv_shape = jax.ShapeDtypeStruct((bt, WIDTH, t), F32)
    return pl.pallas_call(
        functools.partial(_inproj_kernel, nseq=nseq),
        grid=(n // tm,),
        in_specs=[row(D_MODEL), _const_spec((1, D_MODEL)), _const_spec((D_MODEL, _C_END)),
                  _const_spec((2 * WIDTH, D_MODEL)), _const_spec((1, WIDTH)),
                  _const_spec((WIDTH, 1)), _const_spec((WIDTH, WIDTH))],
        out_specs=(s5_spec, row(GLA_COLS_PAD), row(RWKV_COLS), row(WIDTH), kv_spec, kv_spec,
                   row(GATE_COLS)),
        out_shape=(jax.ShapeDtypeStruct(s5_shape, F32),
                   jax.ShapeDtypeStruct((n, GLA_COLS_PAD), F32),
                   jax.ShapeDtypeStruct((n, RWKV_COLS), F32),
                   jax.ShapeDtypeStruct((n, WIDTH), BF16), kv_shape, kv_shape,
                   jax.ShapeDtypeStruct((n, GATE_COLS), BF16)),
        compiler_params=_params("parallel"),
        name="inproj",
    )(x2d, p["norm_mix"], p["w_in"], p["w_kv_t"], p["sb_q_norm"], p["sb_k_norm"], p["head_ones"])


def _s5_kernel(u_ref, x0_ref, bbig_ref, cbig_ref, ar_ref, ai_ref, d_ref, wglu_ref, bglu_ref,
               o_ref, xfin_ref, bu_scr, xs_scr, x_scr, *, nb, lc, sub):
    @pl.when(pl.program_id(0) == 0)
    def _():
        x_scr[...] = x0_ref[...]

    ar = jnp.broadcast_to(ar_ref[...], (nb, S5_FLAT))
    ai = jnp.broadcast_to(ai_ref[...], (nb, S5_FLAT))
    xr = x_scr[:, 0:S5_FLAT]
    xi = x_scr[:, S5_FLAT:2 * S5_FLAT]
    for q in range(lc // sub):
        block = slice(q * sub * nb, (q + 1) * sub * nb)
        u = u_ref[block, :]
        ub = u.astype(BF16)
        bu_scr[block, 0:S5_FLAT] = jnp.dot(ub, bbig_ref[:, 0:S5_FLAT], preferred_element_type=F32)
        bu_scr[block, S5_FLAT:2 * S5_FLAT] = jnp.dot(ub, bbig_ref[:, S5_FLAT:2 * S5_FLAT],
                                                     preferred_element_type=F32)
        for t in range(q * sub, (q + 1) * sub):
            rows = slice(t * nb, (t + 1) * nb)
            xr, xi = (ar * xr - ai * xi + bu_scr[rows, 0:S5_FLAT],
                      ar * xi + ai * xr + bu_scr[rows, S5_FLAT:2 * S5_FLAT])
            xs_scr[rows, 0:S5_FLAT] = xr.astype(BF16)
            xs_scr[rows, S5_FLAT:2 * S5_FLAT] = xi.astype(BF16)
        y = (jnp.dot(xs_scr[block, 0:S5_FLAT], cbig_ref[0:S5_FLAT, :], preferred_element_type=F32)
             + jnp.dot(xs_scr[block, S5_FLAT:2 * S5_FLAT], cbig_ref[S5_FLAT:2 * S5_FLAT, :],
                       preferred_element_type=F32) + d_ref[...] * u)
        y = jax.nn.gelu(y)
        o_ref[block, :] = y * _sigmoid(_mm(y, wglu_ref[...]) + bglu_ref[...])
    x_scr[:, 0:S5_FLAT] = xr
    x_scr[:, S5_FLAT:2 * S5_FLAT] = xi
    xfin_ref[...] = x_scr[...]


def _s5(u_tm, x0, p, nb, lc):
    rows = u_tm.shape[0]
    blk = lc * nb
    sub = _pick(lc, (128, 64, 32, 16, 8))
    assert nb % 16 == 0
    kern = functools.partial(_s5_kernel, nb=nb, lc=lc, sub=sub)
    return pl.pallas_call(
        kern,
        grid=(rows // blk,),
        in_specs=[pl.BlockSpec((blk, WIDTH), lambda c: (c, 0)),
                  _const_spec((nb, 2 * S5_FLAT)),
                  _const_spec((WIDTH, 2 * S5_FLAT)), _const_spec((2 * S5_FLAT, WIDTH)),
                  _const_spec((1, S5_FLAT)), _const_spec((1, S5_FLAT)), _const_spec((1, WIDTH)),
                  _const_spec((WIDTH, WIDTH)), _const_spec((1, WIDTH))],
        out_specs=(pl.BlockSpec((blk, WIDTH), lambda c: (c, 0)),
                   pl.BlockSpec((nb, 2 * S5_FLAT), lambda c: (0, 0))),
        out_shape=(jax.ShapeDtypeStruct((rows, WIDTH), F32),
                   jax.ShapeDtypeStruct((nb, 2 * S5_FLAT), F32)),
        scratch_shapes=[pltpu.VMEM((blk, 2 * S5_FLAT), F32), pltpu.VMEM((blk, 2 * S5_FLAT), BF16),
                        pltpu.VMEM((nb, 2 * S5_FLAT), F32)],
        compiler_params=_params("arbitrary"),
        name="s5",
    )(u_tm, x0, p["s5_bbig"], p["s5_cbig"], p["s5_ar"], p["s5_ai"], p["s5_d"], p["s5_w_glu"],
      p["s5_b_glu"])


def _gla_kernel(z_ref, s0_ref, walpha_ref, balpha_ref, gnorm_ref, ones_ref, o_ref, sfin_ref, s_scr,
                *, ng, nb, L):
    R = nb * L
    groups = range(ng)
    heads = range(GLA_HEADS)

    @pl.when(pl.program_id(1) == 0)
    def _():
        s_scr[...] = s0_ref[...]

    rr = lax.broadcasted_iota(jnp.int32, (R, R), 0)
    cc = lax.broadcasted_iota(jnp.int32, (R, R), 1)
    tri = jnp.logical_and((rr & -L) == (cc & -L), rr >= cc)
    tri_ones = tri.astype(BF16)
    qk_lane = lax.broadcasted_iota(jnp.int32, (1, GLA_QK), 1) & -GLA_DK
    v_lane = lax.broadcasted_iota(jnp.int32, (1, WIDTH), 1) & -GLA_DV
    head_block = ((lax.broadcasted_iota(jnp.int32, (WIDTH, GLA_QK), 0) & -GLA_DV) * GLA_DK
                  == (lax.broadcasted_iota(jnp.int32, (WIDTH, GLA_QK), 1) & -GLA_DK) * GLA_DV)

    def prologue(gi):
        z = z_ref[gi * nb:(gi + 1) * nb].reshape(R, GLA_COLS_PAD)
        q = z[:, 0:GLA_QK] * (GLA_DK ** -0.5)
        k = z[:, GLA_QK:2 * GLA_QK]
        alat = z[:, 2 * GLA_QK + 2 * WIDTH:]
        log_a = _log_sigmoid(_mm(alat, walpha_ref[...]) + balpha_ref[...]) * (1.0 / GLA_TAU)
        b = _mm_split_rhs(tri_ones, log_a)
        b_last = [b[L * (i + 1) - 1:L * (i + 1), :] for i in range(nb)]
        b_end = jnp.concatenate([jnp.broadcast_to(be, (L, GLA_QK)) for be in b_last], axis=0)
        return dict(v=z[:, 2 * GLA_QK:2 * GLA_QK + WIDTH],
                    g=z[:, 2 * GLA_QK + WIDTH:2 * GLA_QK + 2 * WIDTH], b_last=b_last,
                    qt=q * jnp.exp(b), kt=(k * jnp.exp(-b)).astype(BF16),
                    kd=(k * jnp.exp(b_end - b)).astype(BF16))

    pro = [prologue(gi) for gi in groups]
    units = [(gi, h) for gi in groups for h in heads]
    attn = [jnp.where(tri, _mm_nt(jnp.where(qk_lane == GLA_DK * h, pro[gi]["qt"], 0.0),
                                  pro[gi]["kt"]), 0.0).astype(BF16) for gi, h in units]
    o_h = [jnp.dot(attn[u], jnp.where(v_lane == GLA_DV * h, pro[gi]["v"], 0.0).astype(BF16),
                   preferred_element_type=F32) for u, (gi, h) in enumerate(units)]
    for gi in groups:
        qt, kd, v = pro[gi]["qt"].astype(BF16), pro[gi]["kd"], pro[gi]["v"].astype(BF16)
        from_s = []
        for i in range(nb):
            seq = gi * nb + i
            rows = slice(L * i, L * (i + 1))
            st = s_scr[seq]
            from_s.append(_mm_nt(qt[rows], st))
            update = lax.dot_general(v[rows], kd[rows], (((0,), (0,)), ((), ())),
                                     preferred_element_type=F32)
            s_scr[seq] = st * jnp.exp(pro[gi]["b_last"][i]) + jnp.where(head_block, update, 0.0)
        o = jnp.concatenate(from_s, axis=0)
        for h in heads:
            o = o + o_h[gi * GLA_HEADS + h]
        ms = _mm_split_lhs(o * o, ones_ref[...]) * (1.0 / GLA_DV)
        g = pro[gi]["g"]
        o_ref[gi * nb:(gi + 1) * nb] = (o * lax.rsqrt(ms + RMS_EPS) * gnorm_ref[...]
                                        * (g * _sigmoid(g))).reshape(nb, L, WIDTH)

    @pl.when(pl.program_id(1) == pl.num_programs(1) - 1)
    def _():
        sfin_ref[...] = s_scr[...]


def _gla(cols, s0, p, ng, nb, L):
    bt, t, _ = cols.shape
    ns = ng * nb
    assert bt % ns == 0 and t % L == 0 and L & (L - 1) == 0
    eye = jnp.eye(GLA_HEADS, dtype=F32)
    s0_blocks = jnp.einsum("bhcv,hg->bhvgc", s0, eye).reshape(bt, WIDTH, GLA_QK)
    kern = functools.partial(_gla_kernel, ng=ng, nb=nb, L=L)
    state = pl.BlockSpec((ns, WIDTH, GLA_QK), lambda b, n: (b, 0, 0))
    o, s_blocks = pl.pallas_call(
        kern,
        grid=(bt // ns, t // L),
        in_specs=[pl.BlockSpec((ns, L, GLA_COLS_PAD), lambda b, n: (b, n, 0)), state,
                  _const_spec((128, GLA_QK)), _const_spec((1, GLA_QK)), _const_spec((1, WIDTH)),
                  _const_spec((WIDTH, WIDTH))],
        out_specs=(pl.BlockSpec((ns, L, WIDTH), lambda b, n: (b, n, 0)), state),
        out_shape=(jax.ShapeDtypeStruct((bt, t, WIDTH), F32),
                   jax.ShapeDtypeStruct((bt, WIDTH, GLA_QK), F32)),
        scratch_shapes=[pltpu.VMEM((ns, WIDTH, GLA_QK), F32)],
        compiler_params=_params("parallel", "arbitrary"),
        name="gla",
    )(cols, s0_blocks, p["gla_w_alpha"], p["gla_b_alpha"], p["gla_norm"], p["head_ones"])
    s5d = s_blocks.reshape(bt, GLA_HEADS, GLA_DV, GLA_HEADS, GLA_DK)
    s_new = jnp.stack([s5d[:, h, :, h, :] for h in range(GLA_HEADS)], axis=1)
    return o, jnp.swapaxes(s_new, 2, 3)


def _rwkv_kernel(z_ref, shift0_ref, s0_ref, mu_ref, w0_ref, w2_ref, a0_ref, a2_ref, g2_ref, kk_ref,
                 ka_ref, rk_ref, lnw_ref, lnb_ref, bd_ref, o_ref, shiftfin_ref, sfin_ref,
                 s_scr, prev_scr, *, ng, nb, L):
    R = nb * L
    HD = RWKV_HEAD
    groups = range(ng)
    heads = range(RWKV_HEADS)
    hsl = [slice(HD * h, HD * (h + 1)) for h in heads]

    @pl.when(pl.program_id(1) == 0)
    def _():
        for gi in groups:
            for i in range(nb):
                for h in heads:
                    s_scr[gi, h, :, HD * i:HD * (i + 1)] = s0_ref[gi * nb + i, h]
        prev_scr[...] = shift0_ref[...]

    rr = lax.broadcasted_iota(jnp.int32, (R, R), 0)
    cc = lax.broadcasted_iota(jnp.int32, (R, R), 1)
    same = (rr & -L) == (cc & -L)
    tri = jnp.logical_and(same, rr >= cc)
    stri = jnp.logical_and(same, rr > cc)
    tri_ones = tri.astype(BF16)
    own = ((lax.broadcasted_iota(jnp.int32, (R, nb * HD), 0) & -L) * HD
           == (lax.broadcasted_iota(jnp.int32, (R, nb * HD), 1) & -HD) * L)
    first = (lax.broadcasted_iota(jnp.int32, (R, RWKV_COLS), 0) & (L - 1)) == 0

    def spread(x):
        return jnp.where(own, jnp.concatenate([x] * nb, axis=1), 0.0).astype(BF16)

    pro = [_rwkv_prologue(gi, z_ref, prev_scr, shiftfin_ref, first, tri_ones, mu_ref, w0_ref, w2_ref,
                          a0_ref, a2_ref, g2_ref, kk_ref, ka_ref, bd_ref, nb, L) for gi in groups]

    units = [(gi, h) for gi in groups for h in heads]
    f32dot = functools.partial(jnp.dot, preferred_element_type=F32)
    vb = [pro[gi]["v"][:, hsl[h]].astype(BF16) for gi, h in units]
    gram = [_mm_nt(jnp.concatenate([pro[gi]["a_til"][:, hsl[h]], pro[gi]["r_til"][:, hsl[h]]], axis=0),
                   jnp.concatenate([pro[gi]["b_til"][:, hsl[h]], pro[gi]["k_til"][:, hsl[h]]], axis=0))
            for gi, h in units]
    from_s = [_mm_nt(jnp.concatenate([spread(pro[gi]["a_til"][:, hsl[h]]),
                                      spread(pro[gi]["r_til"][:, hsl[h]])], axis=0), s_scr[gi, h])
              for gi, h in units]
    pw = [jnp.where(stri, gm[:R, :R], 0.0).astype(BF16) for gm in gram]
    a_ak = [jnp.where(stri, gm[:R, R:], 0.0).astype(BF16) for gm in gram]
    a_r = [jnp.concatenate([jnp.where(tri, gm[R:, :R], 0.0).astype(BF16),
                            jnp.where(tri, gm[R:, R:], 0.0).astype(BF16)], axis=1) for gm in gram]
    n_units = range(len(units))
    sa = [from_s[u][:R] + f32dot(a_ak[u], vb[u]) for u in n_units]
    sa = [sa[u] + f32dot(pw[u], sa[u].astype(BF16)) for u in n_units]
    for _ in range(int(math.log2(L)) - 1):
        pw = [f32dot(pw[u], pw[u]).astype(BF16) for u in n_units]
        sa = [sa[u] + f32dot(pw[u], sa[u].astype(BF16)) for u in n_units]
    sa_v = [jnp.concatenate([sa[u].astype(BF16), vb[u]], axis=0) for u in n_units]
    y = [from_s[u][R:] + f32dot(a_r[u], sa_v[u]) for u in n_units]
    for u, (gi, h) in enumerate(units):
        p_all = jnp.concatenate([pe[:, hsl[h]] for pe in pro[gi]["p_last"]], axis=1)
        s_scr[gi, h] = s_scr[gi, h] * p_all + _mm_tn(
            sa_v[u], jnp.concatenate([spread(pro[gi]["b_end"][:, hsl[h]]),
                                      spread(pro[gi]["k_end"][:, hsl[h]])], axis=0))
    for gi in groups:
        outs = []
        for h in heads:
            yh = y[gi * RWKV_HEADS + h]
            hs = hsl[h]
            mean = jnp.mean(yh, axis=-1, keepdims=True)
            var = jnp.mean(jnp.square(yh - mean), axis=-1, keepdims=True)
            yn = (yh - mean) * lax.rsqrt(var + RWKV_GN_EPS) * lnw_ref[:, hs] + lnb_ref[:, hs]
            bonus = jnp.sum(pro[gi]["r"][:, hs] * pro[gi]["k"][:, hs] * rk_ref[:, hs], axis=-1,
                            keepdims=True)
            outs.append(yn + bonus * pro[gi]["v"][:, hs])
        o_ref[gi * nb:(gi + 1) * nb] = (jnp.concatenate(outs, axis=-1)
                                        * pro[gi]["g"]).reshape(nb, L, WIDTH)

    @pl.when(pl.program_id(1) == pl.num_programs(1) - 1)
    def _():
        for gi in groups:
            for i in range(nb):
                for h in heads:
                    sfin_ref[gi * nb + i, h] = s_scr[gi, h, :, HD * i:HD * (i + 1)]


def _rwkv_prologue(gi, z_ref, prev_scr, shiftfin_ref, first, tri_ones, mu_ref, w0_ref, w2_ref, a0_ref,
                   a2_ref, g2_ref, kk_ref, ka_ref, bd_ref, nb, L):
    R = nb * L
    seqs = range(gi * nb, (gi + 1) * nb)
    z = z_ref[gi * nb:(gi + 1) * nb].reshape(R, RWKV_COLS)
    carried = jnp.concatenate(
        [jnp.broadcast_to(prev_scr[s], (L, RWKV_COLS)) for s in seqs], axis=0)
    prev = jnp.where(first, carried, pltpu.roll(z, 1, axis=0))
    for i, s in enumerate(seqs):
        last = z[L * (i + 1) - 1:L * (i + 1), :]
        prev_scr[s] = last
        shiftfin_ref[s] = last
    zs = z + (prev - z) * mu_ref[...]
    r = zs[:, 0:256]
    k = zs[:, 256:512]
    v = zs[:, 512:768]
    w_lat = zs[:, 768:832]
    a_lat = zs[:, 832:896]
    g_lat = zs[:, 896:1024]
    w = -jax.nn.softplus(-(w0_ref[...] + _mm(jnp.tanh(w_lat), w2_ref[...]))) - 0.5
    log_w = -jnp.exp(w)
    a = _sigmoid(a0_ref[...] + _mm(a_lat, a2_ref[...]))
    g = _mm(_sigmoid(g_lat), g2_ref[...])
    kk = k * kk_ref[...]
    kk = kk / jnp.maximum(jnp.sqrt(_mm_split_lhs(kk * kk, bd_ref[...])), 1e-12)
    k = k * (1.0 + (a - 1.0) * ka_ref[...])
    cum = _mm_split_rhs(tri_ones, log_w)
    p_t = jnp.exp(cum)
    inv_p = jnp.exp(-cum)
    p_last = [p_t[L * (i + 1) - 1:L * (i + 1), :] for i in range(nb)]
    p_end = jnp.concatenate([jnp.broadcast_to(pe, (L, WIDTH)) for pe in p_last], axis=0)
    b_til = kk * a * inv_p
    k_til = k * inv_p
    return dict(r=r, k=k, v=v, g=g, p_last=p_last,
                a_til=-kk * jnp.exp(cum - log_w),
                b_til=b_til, k_til=k_til, r_til=r * p_t,
                b_end=b_til * p_end, k_end=k_til * p_end)


def _rwkv(cols, shift0, s0, p, ng, nb, L):
    bt, t, _ = cols.shape
    ns = ng * nb
    assert bt % ns == 0 and t % L == 0 and L & (L - 1) == 0
    kern = functools.partial(_rwkv_kernel, ng=ng, nb=nb, L=L)
    state = pl.BlockSpec((ns, RWKV_HEADS, RWKV_HEAD, RWKV_HEAD), lambda b, n: (b, 0, 0, 0))
    shift = pl.BlockSpec((ns, 1, RWKV_COLS), lambda b, n: (b, 0, 0))
    vec = _const_spec((1, WIDTH))
    return pl.pallas_call(
        kern,
        grid=(bt // ns, t // L),
        in_specs=[pl.BlockSpec((ns, L, RWKV_COLS), lambda b, n: (b, n, 0)), shift, state,
                  _const_spec((1, RWKV_COLS)), vec, _const_spec((64, WIDTH)), vec,
                  _const_spec((64, WIDTH)), _const_spec((128, WIDTH)), vec, vec, vec, vec, vec,
                  _const_spec((WIDTH, WIDTH))],
        out_specs=(pl.BlockSpec((ns, L, WIDTH), lambda b, n: (b, n, 0)), shift, state),
        out_shape=(jax.ShapeDtypeStruct((bt, t, WIDTH), F32),
                   jax.ShapeDtypeStruct((bt, 1, RWKV_COLS), F32),
                   jax.ShapeDtypeStruct((bt, RWKV_HEADS, RWKV_HEAD, RWKV_HEAD), F32)),
        scratch_shapes=[pltpu.VMEM((ng, RWKV_HEADS, RWKV_HEAD, nb * RWKV_HEAD), F32),
                        pltpu.VMEM((ns, 1, RWKV_COLS), F32)],
        compiler_params=_params("parallel", "arbitrary"),
        name="rwkv",
    )(cols, shift0, s0, p["rwkv_mu"], p["rwkv_w0"], p["rwkv_w2"], p["rwkv_a0"], p["rwkv_a2"],
      p["rwkv_g2"], p["rwkv_k_k"], p["rwkv_k_a"], p["rwkv_r_k"], p["rwkv_ln_w"], p["rwkv_ln_b"],
      p["head_ones"])


SB_DEAD_LOG = -110.0
SB_LANES = 128


def _sb_block(qh, k, v, diag, suffix_ones, acc_scr, carry_scr):
    tq = qh[0].shape[0]
    tk = k.shape[1]
    kb = k.astype(BF16)
    z = jnp.concatenate(
        [jnp.dot(qh[h], kb[SB_HEAD * h:SB_HEAD * (h + 1)], preferred_element_type=F32)
         for h in range(SB_HEADS)], axis=0)
    lm = -(jnp.maximum(z, 0.0) + jnp.log(1.0 + jnp.exp(-jnp.abs(z))))
    if diag:
        row = lax.broadcasted_iota(jnp.int32, z.shape, 0) & (tq - 1)
        mask = row > lax.broadcasted_iota(jnp.int32, z.shape, 1)
        lm = jnp.where(mask, lm, 0.0)
    sums = _mm_split_lhs(lm, suffix_ones)
    carry = carry_scr[...]
    if tk <= SB_LANES:
        carry_keys = carry[:, :tk]
    else:
        carry_keys = jnp.concatenate([carry] * (tk // SB_LANES), axis=1)
    wgt = jnp.exp(z + carry_keys + sums)
    if diag:
        wgt = jnp.where(mask, wgt, 0.0)
    wgt = wgt.astype(BF16)
    vb = v.astype(BF16)
    for h in range(SB_HEADS):
        acc_scr[h] += lax.dot_general(wgt[tq * h:tq * (h + 1)], vb[SB_HEAD * h:SB_HEAD * (h + 1)],
                                      (((1,), (1,)), ((), ())), preferred_element_type=F32)
    carry = carry + jnp.broadcast_to(sums[:, 0:1], carry.shape)
    carry_scr[...] = carry
    return jnp.max(carry)


def _suffix_ones(tk):
    r = lax.broadcasted_iota(jnp.int32, (tk, tk), 0)
    c = lax.broadcasted_iota(jnp.int32, (tk, tk), 1)
    return (r >= c).astype(BF16)


def _sb_heads(q):
    return [q[:, SB_HEAD * h:SB_HEAD * (h + 1)].astype(BF16) for h in range(SB_HEADS)]


def _sb_self_kernel(q_ref, k_ref, v_ref, o_ref, acc_scr, carry_scr, *, tq):
    i = pl.program_id(1)
    acc_scr[...] = jnp.zeros_like(acc_scr)
    carry_scr[...] = jnp.zeros_like(carry_scr)
    qh = _sb_heads(q_ref[...])
    ones = _suffix_ones(tq)

    def keys(j):
        return pl.ds(pl.multiple_of(j * tq, tq), tq)

    live = _sb_block(qh, k_ref[0, :, keys(i)], v_ref[0, :, keys(i)], True, ones, acc_scr, carry_scr)

    def body(state):
        j, _ = state
        m = _sb_block(qh, k_ref[0, :, keys(j)], v_ref[0, :, keys(j)], False, ones, acc_scr, carry_scr)
        return j - 1, m

    lax.while_loop(lambda s: jnp.logical_and(s[0] >= 0, s[1] > SB_DEAD_LOG), body, (i - 1, live))
    o_ref[...] = jnp.concatenate([acc_scr[h] for h in range(SB_HEADS)], axis=-1)


def _sb_self(q2d, k_t, v_t, tq):
    bt, _, t = k_t.shape
    nq = t // tq
    assert tq & (tq - 1) == 0
    kern = functools.partial(_sb_self_kernel, tq=tq)
    seq = pl.BlockSpec((1, WIDTH, t), lambda b, i: (b, 0, 0))
    blk = pl.BlockSpec((tq, WIDTH), lambda b, i: (b * nq + i, 0))
    return pl.pallas_call(
        kern,
        grid=(bt, nq),
        in_specs=[blk, seq, seq],
        out_specs=blk,
        out_shape=jax.ShapeDtypeStruct((bt * t, WIDTH), F32),
        scratch_shapes=[pltpu.VMEM((SB_HEADS, tq, SB_HEAD), F32),
                        pltpu.VMEM((SB_HEADS * tq, SB_LANES), F32)],
        compiler_params=_params("parallel", "arbitrary"),
        name="sb_self",
    )(q2d, k_t, v_t)


SB_OLDER_SLOT = 2


def _sb_past_kernel(q_ref, k_ref, v_ref, pk_hbm, pv_hbm, o_ref, kbuf, vbuf, sem, acc_scr, carry_scr,
                    *, layer, t, tkp, np_):
    b = pl.program_id(0)
    slot = b % 2
    qh = _sb_heads(q_ref[...])

    def fetch(seq, blk, dst):
        keys = pl.ds(pl.multiple_of(blk * tkp, tkp), tkp)
        return (pltpu.make_async_copy(pk_hbm.at[layer, seq, :, keys], kbuf.at[dst], sem.at[0, dst]),
                pltpu.make_async_copy(pv_hbm.at[layer, seq, :, keys], vbuf.at[dst], sem.at[1, dst]))

    @pl.when(b == 0)
    def _():
        for cp in fetch(0, np_ - 1, 0):
            cp.start()

    @pl.when(b + 1 < pl.num_programs(0))
    def _():
        for cp in fetch(b + 1, np_ - 1, 1 - slot):
            cp.start()

    acc_scr[...] = jnp.zeros_like(acc_scr)
    carry_scr[...] = jnp.zeros_like(carry_scr)
    _sb_block(qh, k_ref[0], v_ref[0], True, _suffix_ones(t), acc_scr, carry_scr)

    ones = _suffix_ones(tkp)
    for cp in fetch(b, np_ - 1, slot):
        cp.wait()
    live = _sb_block(qh, kbuf[slot], vbuf[slot], False, ones, acc_scr, carry_scr)

    def body(state):
        j, _ = state
        copies = fetch(b, j, SB_OLDER_SLOT)
        for cp in copies:
            cp.start()
        for cp in copies:
            cp.wait()
        m = _sb_block(qh, kbuf[SB_OLDER_SLOT], vbuf[SB_OLDER_SLOT], False, ones, acc_scr, carry_scr)
        return j - 1, m

    lax.while_loop(lambda s: jnp.logical_and(s[0] >= 0, s[1] > SB_DEAD_LOG), body, (np_ - 2, live))
    o_ref[...] = jnp.concatenate([acc_scr[h] for h in range(SB_HEADS)], axis=-1)


def _sb_past(q2d, k_t, v_t, past_k_t, past_v_t, layer, tkp):
    bt, _, t = k_t.shape
    np_ = past_k_t.shape[3] // tkp
    assert t & (t - 1) == 0 and past_k_t.shape[3] % tkp == 0
    kern = functools.partial(_sb_past_kernel, layer=layer, t=t, tkp=tkp, np_=np_)
    qblk = pl.BlockSpec((t, WIDTH), lambda b: (b, 0))
    new = pl.BlockSpec((1, WIDTH, t), lambda b: (b, 0, 0))
    hbm = pl.BlockSpec(memory_space=pl.ANY)
    return pl.pallas_call(
        kern,
        grid=(bt,),
        in_specs=[qblk, new, new, hbm, hbm],
        out_specs=qblk,
        out_shape=jax.ShapeDtypeStruct((bt * t, WIDTH), F32),
        scratch_shapes=[pltpu.VMEM((3, WIDTH, tkp), F32), pltpu.VMEM((3, WIDTH, tkp), F32),
                        pltpu.SemaphoreType.DMA((2, 3)),
                        pltpu.VMEM((SB_HEADS, t, SB_HEAD), F32),
                        pltpu.VMEM((SB_HEADS * t, SB_LANES), F32)],
        compiler_params=_params("arbitrary"),
        name="sb_past",
    )(q2d, k_t, v_t, past_k_t, past_v_t)


def _merge_ffn_kernel(x_ref, oa_ref, ob_ref, oc_ref, od_ref, gate_ref, wb_ref, wout_ref, nffn_ref,
                      wg_ref, wu_ref, wd_ref, y_ref):
    merged = None
    for n, o_ref in enumerate((oa_ref, ob_ref, oc_ref, od_ref)):
        term = gate_ref[:, n * D_MODEL:(n + 1) * D_MODEL].astype(F32) * _mm(o_ref[...], wb_ref[n])
        merged = term if merged is None else merged + term
    x = x_ref[...] + _mm(merged, wout_ref[...])
    h = x * lax.rsqrt(jnp.mean(x * x, axis=-1, keepdims=True) + RMS_EPS) * nffn_ref[...]
    hb = h.astype(BF16)
    gate = jnp.dot(hb, wg_ref[...], preferred_element_type=F32)
    up = jnp.dot(hb, wu_ref[...], preferred_element_type=F32)
    y_ref[...] = x + _mm(gate * _sigmoid(gate) * up, wd_ref[...])


def _merge_ffn(x2d, oa, ob, oc, od, gates, p, tm, nt):
    n = x2d.shape[0]
    row = lambda width: pl.BlockSpec((tm, width), lambda i: (i, 0))
    oa_spec = row(WIDTH) if nt is None else _time_major_spec(tm, nt)
    return pl.pallas_call(
        _merge_ffn_kernel,
        grid=(n // tm,),
        in_specs=[row(D_MODEL), oa_spec, row(WIDTH), row(WIDTH), row(WIDTH), row(GATE_COLS),
                  _const_spec((N_BRANCH, WIDTH, D_MODEL)), _const_spec((D_MODEL, D_MODEL)),
                  _const_spec((1, D_MODEL)), _const_spec((D_MODEL, D_FF)),
                  _const_spec((D_MODEL, D_FF)), _const_spec((D_FF, D_MODEL))],
        out_specs=row(D_MODEL),
        out_shape=jax.ShapeDtypeStruct((n, D_MODEL), F32),
        compiler_params=_params("parallel"),
        name="merge_ffn",
    )(x2d, oa, ob, oc, od, gates, p["w_branch"], p["w_out"], p["norm_ffn"], p["w_ffn_gate"],
      p["w_ffn_up"], p["w_ffn_down"])


def _prep_layer(w):
    p = {}
    w_in = w["w_in"]
    o_gla = WIDTH
    o_alat = o_gla + 2 * GLA_QK + 2 * WIDTH
    o_rwkv = o_alat + GLA_RANK
    o_sb = o_rwkv + RWKV_COLS
    o_sbk = o_sb + WIDTH
    o_gate = o_sb + SB_COLS
    pad = jnp.zeros((D_MODEL, 128 - GLA_RANK), w_in.dtype)
    p["w_in"] = jnp.concatenate(
        [w_in[:, :o_rwkv], pad, w_in[:, o_rwkv:o_sbk], w_in[:, o_gate:]], axis=1).astype(BF16)
    p["w_kv_t"] = w_in[:, o_sbk:o_gate].T.astype(BF16)
    p["norm_mix"] = w["norm_mix"].reshape(1, D_MODEL)

    lam = lax.complex(w["s5_a_re"], w["s5_a_im"])
    dt = jnp.exp(w["s5_log_dt"])[:, None]
    a_bar = jnp.exp(lam * dt)
    b_bar = ((a_bar - 1.0) / lam)[..., None] * lax.complex(w["s5_b_re"], w["s5_b_im"])
    eye = jnp.eye(S5_GROUPS, dtype=F32)

    def in_map(m):
        return jnp.einsum("gpc,gh->gchp", m, eye).reshape(WIDTH, S5_FLAT)

    def out_map(m):
        return jnp.einsum("gcp,gh->gphc", m, eye).reshape(S5_FLAT, WIDTH)

    p["s5_bbig"] = jnp.concatenate([in_map(b_bar.real), in_map(b_bar.imag)], axis=1).astype(BF16)
    p["s5_cbig"] = jnp.concatenate([out_map(w["s5_c_re"]), out_map(-w["s5_c_im"])], axis=0).astype(BF16)
    p["s5_ar"] = a_bar.real.reshape(1, S5_FLAT)
    p["s5_ai"] = a_bar.imag.reshape(1, S5_FLAT)
    p["s5_d"] = w["s5_d"].reshape(1, WIDTH)
    p["s5_w_glu"] = w["s5_w_glu"].astype(BF16)
    p["s5_b_glu"] = w["s5_b_glu"].reshape(1, WIDTH)

    p["gla_w_alpha"] = jnp.concatenate(
        [w["gla_w_alpha"], jnp.zeros((128 - GLA_RANK, GLA_QK), F32)], axis=0).astype(BF16)
    p["gla_b_alpha"] = w["gla_b_alpha"].reshape(1, GLA_QK)
    p["gla_norm"] = w["gla_norm"].reshape(1, WIDTH)

    for name in ("rwkv_w0", "rwkv_a0", "rwkv_k_k", "rwkv_k_a", "rwkv_r_k", "rwkv_ln_w", "rwkv_ln_b"):
        p[name] = w[name].reshape(1, WIDTH)
    p["rwkv_mu"] = w["rwkv_mu"].reshape(1, RWKV_COLS)
    for name in ("rwkv_w2", "rwkv_a2", "rwkv_g2"):
        p[name] = w[name].astype(BF16)
    head = jnp.arange(WIDTH) // RWKV_HEAD
    p["head_ones"] = (head[:, None] == head[None, :]).astype(BF16)

    p["sb_q_norm"] = jnp.tile(w["sb_q_norm"] * (SB_HEAD ** -0.5), SB_HEADS).reshape(1, WIDTH)
    p["sb_k_norm"] = jnp.tile(w["sb_k_norm"], SB_HEADS).reshape(WIDTH, 1)

    p["w_branch"] = w["w_branch"].astype(BF16)
    p["w_out"] = w["w_out"].astype(BF16)
    p["norm_ffn"] = w["norm_ffn"].reshape(1, D_MODEL)
    for name in ("w_ffn_gate", "w_ffn_up", "w_ffn_down"):
        p[name] = w[name].astype(BF16)
    return p


def _pick(n, prefs):
    for c in prefs:
        if n % c == 0:
            return c
    return n


def _layer(x, past_k_t, past_v_t, layer, s5_0, gla_0, rwkv_0, shift_0, p):
    bt, t, _ = x.shape
    n = bt * t
    x2d = x.reshape(n, D_MODEL)
    tm = _pick(n, (256, 128, 64, 32, 16, 8))
    nt = t // tm if t % tm == 0 else None
    assert nt is not None or tm % t == 0
    tm_in = INPROJ_ROWS if nt is not None and t % INPROJ_ROWS == 0 else tm
    u, c_gla, c_rwkv, q, k_t, v_t, gates = _inproj(x2d, p, bt, t, tm_in,
                                                   None if nt is None else t // tm_in)

    if nt is None:
        u = u.reshape(bt, t, WIDTH).transpose(1, 0, 2)
    x0 = jnp.concatenate([s5_0[..., 0].reshape(bt, S5_FLAT), s5_0[..., 1].reshape(bt, S5_FLAT)], axis=1)
    lc = _pick(t, (128, 64, 32, 16, 8))
    o_a, xfin = _s5(u.reshape(n, WIDTH), x0, p, bt, lc)
    if nt is None:
        o_a = o_a.reshape(t, bt, WIDTH).transpose(1, 0, 2).reshape(n, WIDTH)
    else:
        o_a = o_a.reshape(t, bt * WIDTH)
    s5_new = jnp.stack([xfin[:, :S5_FLAT].reshape(bt, S5_GROUPS, S5_STATE),
                        xfin[:, S5_FLAT:].reshape(bt, S5_GROUPS, S5_STATE)], axis=-1)

    chunk = _pick(t, (64, 32, 16, 8))
    nseq = _pick(bt, (RWKV_ROWS // chunk, 4, 2, 1))
    ngrp = _pick(bt // nseq, (RWKV_GROUPS, 1))
    o_b, gla_new = _gla(c_gla.reshape(bt, t, GLA_COLS_PAD), gla_0, p, ngrp, nseq, chunk)

    o_c, shift_new, rwkv_new = _rwkv(c_rwkv.reshape(bt, t, RWKV_COLS),
                                     shift_0.reshape(bt, 1, RWKV_COLS), rwkv_0, p, ngrp, nseq, chunk)

    if past_k_t is None:
        o_d = _sb_self(q, k_t, v_t, _pick(t, (256, 128, 64, 32, 16, 8)))
    else:
        tkp = _pick(past_k_t.shape[3], (256, 128))
        o_d = _sb_past(q, k_t, v_t, past_k_t, past_v_t, layer, tkp)

    y = _merge_ffn(x2d, o_a, o_b.reshape(n, WIDTH), o_c.reshape(n, WIDTH), o_d, gates, p, tm_in,
                   None if nt is None else t // tm_in)

    def cache_rows(a_t):
        return a_t.reshape(bt, SB_HEADS, SB_HEAD, t).transpose(0, 3, 1, 2)

    states = (cache_rows(k_t), cache_rows(v_t), s5_new, gla_new, rwkv_new,
              shift_new.reshape(bt, RWKV_COLS))
    return y.reshape(bt, t, D_MODEL), states


def _keys_on_lanes(cache):
    d, b, pl_, _, _ = cache.shape
    return cache.transpose(0, 1, 3, 4, 2).reshape(d, b, WIDTH, pl_)


def _trunk(x, past_k, past_v, s5_0, gla_0, rwkv_0, shift_0, layers):
    past_k_t = None if past_k is None else _keys_on_lanes(past_k)
    past_v_t = None if past_v is None else _keys_on_lanes(past_v)
    per_layer = []
    for l, p in enumerate(layers):
        x, st = _layer(x, past_k_t, past_v_t, l, s5_0[l], gla_0[l], rwkv_0[l], shift_0[l], p)
        per_layer.append(st)
    return x, [jnp.stack([st[i] for st in per_layer]) for i in range(6)]


def kernel(x_prompt, x_sample, cache_sb_k, cache_sb_v, state_s5, state_gla, state_rwkv, state_rwkv_shift, norm_mix, w_in, s5_a_re, s5_a_im, s5_log_dt, s5_b_re, s5_b_im, s5_c_re, s5_c_im, s5_d, s5_w_glu, s5_b_glu, gla_w_alpha, gla_b_alpha, gla_norm, rwkv_mu, rwkv_w0, rwkv_w2, rwkv_a0, rwkv_a2, rwkv_g2, rwkv_k_k, rwkv_k_a, rwkv_r_k, rwkv_ln_w, rwkv_ln_b, sb_q_norm, sb_k_norm, w_branch, w_out, norm_ffn, w_ffn_gate, w_ffn_up, w_ffn_down):
    weights = dict(norm_mix=norm_mix, w_in=w_in, s5_a_re=s5_a_re, s5_a_im=s5_a_im, s5_log_dt=s5_log_dt,
                   s5_b_re=s5_b_re, s5_b_im=s5_b_im, s5_c_re=s5_c_re, s5_c_im=s5_c_im, s5_d=s5_d,
                   s5_w_glu=s5_w_glu, s5_b_glu=s5_b_glu, gla_w_alpha=gla_w_alpha, gla_b_alpha=gla_b_alpha,
                   gla_norm=gla_norm, rwkv_mu=rwkv_mu, rwkv_w0=rwkv_w0, rwkv_w2=rwkv_w2, rwkv_a0=rwkv_a0,
                   rwkv_a2=rwkv_a2, rwkv_g2=rwkv_g2, rwkv_k_k=rwkv_k_k, rwkv_k_a=rwkv_k_a, rwkv_r_k=rwkv_r_k,
                   rwkv_ln_w=rwkv_ln_w, rwkv_ln_b=rwkv_ln_b, sb_q_norm=sb_q_norm, sb_k_norm=sb_k_norm,
                   w_branch=w_branch, w_out=w_out, norm_ffn=norm_ffn, w_ffn_gate=w_ffn_gate,
                   w_ffn_up=w_ffn_up, w_ffn_down=w_ffn_down)
    depth = w_in.shape[0]
    layers = [_prep_layer({name: arr[l] for name, arr in weights.items()}) for l in range(depth)]

    bp = x_prompt.shape[0]
    y_prompt, p_states = _trunk(
        x_prompt, None, None,
        jnp.zeros((depth, bp, S5_GROUPS, S5_STATE, 2), F32),
        jnp.zeros((depth, bp, GLA_HEADS, GLA_DK, GLA_DV), F32),
        jnp.zeros((depth, bp, RWKV_HEADS, RWKV_HEAD, RWKV_HEAD), F32),
        jnp.zeros((depth, bp, RWKV_COLS), F32), layers)
    y_sample, s_states = _trunk(x_sample, cache_sb_k, cache_sb_v, state_s5, state_gla, state_rwkv,
                                state_rwkv_shift, layers)
    return (y_prompt, y_sample, *p_states, *s_states)
```

```python
import functools
import math

import jax
import jax.numpy as jnp
from jax import lax
from jax.experimental import pallas as pl
from jax.experimental.pallas import tpu as pltpu

F32 = jnp.float32
BF16 = jnp.bfloat16

D_MODEL = 1024
WIDTH = 256
N_BRANCH = 4
S5_GROUPS, S5_GROUP, S5_STATE = 16, 16, 64
S5_FLAT = S5_GROUPS * S5_STATE
GLA_HEADS, GLA_DK, GLA_DV, GLA_RANK, GLA_TAU = 4, 32, 64, 16, 16.0
GLA_QK = GLA_HEADS * GLA_DK
GLA_COLS_PAD = 2 * GLA_QK + 2 * WIDTH + 128
RWKV_HEADS, RWKV_HEAD = 4, 64
RWKV_COLS = 1024
RWKV_GN_EPS = 64e-5
RWKV_ROWS = 256
RWKV_GROUPS = 2
SB_HEADS, SB_HEAD = 4, 64
SB_COLS = 3 * WIDTH
GATE_COLS = N_BRANCH * D_MODEL
D_FF = 2816
RMS_EPS = 1e-6

_C_S5 = 0
_C_GLA = _C_S5 + WIDTH
_C_RWKV = _C_GLA + GLA_COLS_PAD
_C_SBQ = _C_RWKV + RWKV_COLS
_C_GATE = _C_SBQ + WIDTH
_C_END = _C_GATE + GATE_COLS

VMEM_LIMIT = 56 * 1024 * 1024
INPROJ_ROWS = 512


def _params(*sem):
    return pltpu.CompilerParams(dimension_semantics=sem, vmem_limit_bytes=VMEM_LIMIT)


def _const_spec(shape):
    nd = len(shape)
    return pl.BlockSpec(shape, lambda *_: (0,) * nd, pipeline_mode=pl.Buffered(1))


def _mm(a, b):
    return jnp.dot(a.astype(BF16), b.astype(BF16), preferred_element_type=F32)


def _mm_nt(a, b):
    return lax.dot_general(a.astype(BF16), b.astype(BF16), (((1,), (1,)), ((), ())),
                           preferred_element_type=F32)


def _mm_tn(a, b):
    return lax.dot_general(a.astype(BF16), b.astype(BF16), (((0,), (0,)), ((), ())),
                           preferred_element_type=F32)


def _split(a):
    bits = lax.bitcast_convert_type(a, jnp.uint32) & jnp.uint32(0xFFFF0000)
    hi = lax.bitcast_convert_type(bits, F32)
    return hi.astype(BF16), (a - hi).astype(BF16)


def _mm_split_lhs(a, b01):
    hi, lo = _split(a)
    return (jnp.dot(hi, b01, preferred_element_type=F32)
            + jnp.dot(lo, b01, preferred_element_type=F32))


def _mm_split_rhs(a01, b):
    hi, lo = _split(b)
    return (jnp.dot(a01, hi, preferred_element_type=F32)
            + jnp.dot(a01, lo, preferred_element_type=F32))


def _log_sigmoid(z):
    return jnp.minimum(z, 0.0) - jnp.log1p(jnp.exp(-jnp.abs(z)))


def _sigmoid(z):
    return 1.0 / (1.0 + jnp.exp(-z))


def _tri(n, strict=False):
    r = lax.broadcasted_iota(jnp.int32, (n, n), 0)
    c = lax.broadcasted_iota(jnp.int32, (n, n), 1)
    return (r > c) if strict else (r >= c)


def _inproj_kernel(x_ref, g_ref, w_ref, wkv_ref, qn_ref, kn_ref, ones_ref,
                   s5_ref, gla_ref, rwkv_ref, q_ref, kt_ref, vt_ref, gate_ref, *, nseq):
    x = x_ref[...]
    h = x * lax.rsqrt(jnp.mean(x * x, axis=-1, keepdims=True) + RMS_EPS) * g_ref[...]
    hb = h.astype(BF16)

    def mm(lo, hi):
        return jnp.dot(hb, w_ref[:, lo:hi], preferred_element_type=F32)

    s5_ref[...] = mm(_C_S5, _C_GLA)
    gla_ref[...] = mm(_C_GLA, _C_RWKV)
    rwkv_ref[...] = mm(_C_RWKV, _C_SBQ)
    for n in range(N_BRANCH):
        lo = _C_GATE + n * D_MODEL
        gate_ref[:, n * D_MODEL:(n + 1) * D_MODEL] = _sigmoid(mm(lo, lo + D_MODEL)).astype(BF16)

    q = mm(_C_SBQ, _C_GATE)
    q_ms = _mm_split_lhs(q * q, ones_ref[...]) * (1.0 / SB_HEAD)
    q_ref[...] = (q * lax.rsqrt(q_ms + RMS_EPS) * qn_ref[...]).astype(BF16)

    kv_t = lax.dot_general(wkv_ref[...], hb, (((1,), (1,)), ((), ())), preferred_element_type=F32)
    k_t = []
    for hd in range(SB_HEADS):
        kh = kv_t[SB_HEAD * hd:SB_HEAD * (hd + 1)]
        k_t.append(kh * lax.rsqrt(jnp.mean(kh * kh, axis=0, keepdims=True) + RMS_EPS))
    k_t = jnp.concatenate(k_t, axis=0) * kn_ref[...]
    v_t = kv_t[WIDTH:2 * WIDTH]
    t = kt_ref.shape[2]
    for s in range(nseq):
        kt_ref[s] = k_t[:, t * s:t * (s + 1)]
        vt_ref[s] = v_t[:, t * s:t * (s + 1)]


def _inproj(x2d, p, bt, t, tm, nt):
    n = x2d.shape[0]
    row = lambda width: pl.BlockSpec((tm, width), lambda i: (i, 0))
    s5_spec, s5_shape = row(WIDTH), (n, WIDTH)
    if nt is None:
        nseq = tm // t
        kv_spec = pl.BlockSpec((nseq, WIDTH, t), lambda i: (i, 0, 0))
    else:
        nseq = 1
        kv_spec = pl.BlockSpec((1, WIDTH, tm), lambda i: (i // nt, 0, i % nt))
    kv_shape = jax.ShapeDtypeStruct((bt, WIDTH, t), F32)
    return pl.pallas_call(
        functools.partial(_inproj_kernel, nseq=nseq),
        grid=(n // tm,),
        in_specs=[row(D_MODEL), _const_spec((1, D_MODEL)), _const_spec((D_MODEL, _C_END)),
                  _const_spec((2 * WIDTH, D_MODEL)), _const_spec((1, WIDTH)),
                  _const_spec((WIDTH, 1)), _const_spec((WIDTH, WIDTH))],
        out_specs=(s5_spec, row(GLA_COLS_PAD), row(RWKV_COLS), row(WIDTH), kv_spec, kv_spec,
                   row(GATE_COLS)),
        out_shape=(jax.ShapeDtypeStruct(s5_shape, F32),
                   jax.ShapeDtypeStruct((n, GLA_COLS_PAD), F32),
                   jax.ShapeDtypeStruct((n, RWKV_COLS), F32),
                   jax.ShapeDtypeStruct((n, WIDTH), BF16), kv_shape, kv_shape,
                   jax.ShapeDtypeStruct((n, GATE_COLS), BF16)),
        compiler_params=_params("parallel"),
        name="inproj",
    )(x2d, p["norm_mix"], p["w_in"], p["w_kv_t"], p["sb_q_norm"], p["sb_k_norm"], p["head_ones"])


def _s5_kernel(u_ref, x0_ref, bbig_ref, cbig_ref, ar_ref, ai_ref, d_ref, wglu_ref, bglu_ref,
               o_ref, xfin_ref, u_scr, bu_scr, xs_scr, x_scr, *, nb, lc):
    @pl.when(pl.program_id(0) == 0)
    def _():
        x_scr[...] = x0_ref[...]

    for t in range(lc):
        u_scr[t * nb:(t + 1) * nb, :] = u_ref[:, t, :]
    u = u_scr[...]
    ub = u.astype(BF16)
    bu_scr[:, 0:S5_FLAT] = jnp.dot(ub, bbig_ref[:, 0:S5_FLAT], preferred_element_type=F32)
    bu_scr[:, S5_FLAT:2 * S5_FLAT] = jnp.dot(ub, bbig_ref[:, S5_FLAT:2 * S5_FLAT],
                                             preferred_element_type=F32)
    ar = jnp.broadcast_to(ar_ref[...], (nb, S5_FLAT))
    ai = jnp.broadcast_to(ai_ref[...], (nb, S5_FLAT))
    xr = x_scr[:, 0:S5_FLAT]
    xi = x_scr[:, S5_FLAT:2 * S5_FLAT]
    for t in range(lc):
        rows = slice(t * nb, (t + 1) * nb)
        xr, xi = (ar * xr - ai * xi + bu_scr[rows, 0:S5_FLAT],
                  ar * xi + ai * xr + bu_scr[rows, S5_FLAT:2 * S5_FLAT])
        xs_scr[rows, 0:S5_FLAT] = xr.astype(BF16)
        xs_scr[rows, S5_FLAT:2 * S5_FLAT] = xi.astype(BF16)
    x_scr[:, 0:S5_FLAT] = xr
    x_scr[:, S5_FLAT:2 * S5_FLAT] = xi
    xfin_ref[...] = x_scr[...]
    y = (jnp.dot(xs_scr[:, 0:S5_FLAT], cbig_ref[0:S5_FLAT, :], preferred_element_type=F32)
         + jnp.dot(xs_scr[:, S5_FLAT:2 * S5_FLAT], cbig_ref[S5_FLAT:2 * S5_FLAT, :],
                   preferred_element_type=F32) + d_ref[...] * u)
    y = jax.nn.gelu(y)
    out = y * _sigmoid(_mm(y, wglu_ref[...]) + bglu_ref[...])
    for t in range(lc):
        o_ref[:, t, :] = out[t * nb:(t + 1) * nb]


def _s5(u, x0, p, lc):
    nb, t, _ = u.shape
    blk = lc * nb
    assert nb % 16 == 0
    kern = functools.partial(_s5_kernel, nb=nb, lc=lc)
    seq = pl.BlockSpec((nb, lc, WIDTH), lambda c: (0, c, 0))
    return pl.pallas_call(
        kern,
        grid=(t // lc,),
        in_specs=[seq,
                  _const_spec((nb, 2 * S5_FLAT)),
                  _const_spec((WIDTH, 2 * S5_FLAT)), _const_spec((2 * S5_FLAT, WIDTH)),
                  _const_spec((1, S5_FLAT)), _const_spec((1, S5_FLAT)), _const_spec((1, WIDTH)),
                  _const_spec((WIDTH, WIDTH)), _const_spec((1, WIDTH))],
        out_specs=(seq, pl.BlockSpec((nb, 2 * S5_FLAT), lambda c: (0, 0))),
        out_shape=(jax.ShapeDtypeStruct((nb, t, WIDTH), F32),
                   jax.ShapeDtypeStruct((nb, 2 * S5_FLAT), F32)),
        scratch_shapes=[pltpu.VMEM((blk, WIDTH), F32), pltpu.VMEM((blk, 2 * S5_FLAT), F32),
                        pltpu.VMEM((blk, 2 * S5_FLAT), BF16), pltpu.VMEM((nb, 2 * S5_FLAT), F32)],
        compiler_params=_params("arbitrary"),
        name="s5",
    )(u, x0, p["s5_bbig"], p["s5_cbig"], p["s5_ar"], p["s5_ai"], p["s5_d"], p["s5_w_glu"],
      p["s5_b_glu"])


def _gla_kernel(z_ref, s0_ref, walpha_ref, balpha_ref, gnorm_ref, ones_ref, o_ref, sfin_ref, s_scr,
                *, ng, nb, L):
    R = nb * L
    groups = range(ng)
    heads = range(GLA_HEADS)

    @pl.when(pl.program_id(1) == 0)
    def _():
        s_scr[...] = s0_ref[...]

    rr = lax.broadcasted_iota(jnp.int32, (R, R), 0)
    cc = lax.broadcasted_iota(jnp.int32, (R, R), 1)
    tri = jnp.logical_and((rr & -L) == (cc & -L), rr >= cc)
    tri_ones = tri.astype(BF16)
    qk_lane = lax.broadcasted_iota(jnp.int32, (1, GLA_QK), 1) & -GLA_DK
    v_lane = lax.broadcasted_iota(jnp.int32, (1, WIDTH), 1) & -GLA_DV
    head_block = ((lax.broadcasted_iota(jnp.int32, (WIDTH, GLA_QK), 0) & -GLA_DV) * GLA_DK
                  == (lax.broadcasted_iota(jnp.int32, (WIDTH, GLA_QK), 1) & -GLA_DK) * GLA_DV)

    def prologue(gi):
        z = z_ref[gi * nb:(gi + 1) * nb].reshape(R, GLA_COLS_PAD)
        q = z[:, 0:GLA_QK] * (GLA_DK ** -0.5)
        k = z[:, GLA_QK:2 * GLA_QK]
        alat = z[:, 2 * GLA_QK + 2 * WIDTH:]
        log_a = _log_sigmoid(_mm(alat, walpha_ref[...]) + balpha_ref[...]) * (1.0 / GLA_TAU)
        b = _mm_split_rhs(tri_ones, log_a)
        b_last = [b[L * (i + 1) - 1:L * (i + 1), :] for i in range(nb)]
        b_end = jnp.concatenate([jnp.broadcast_to(be, (L, GLA_QK)) for be in b_last], axis=0)
        return dict(v=z[:, 2 * GLA_QK:2 * GLA_QK + WIDTH],
                    g=z[:, 2 * GLA_QK + WIDTH:2 * GLA_QK + 2 * WIDTH], b_last=b_last,
                    qt=q * jnp.exp(b), kt=(k * jnp.exp(-b)).astype(BF16),
                    kd=(k * jnp.exp(b_end - b)).astype(BF16))

    pro = [prologue(gi) for gi in groups]
    units = [(gi, h) for gi in groups for h in heads]
    attn = [jnp.where(tri, _mm_nt(jnp.where(qk_lane == GLA_DK * h, pro[gi]["qt"], 0.0),
                                  pro[gi]["kt"]), 0.0).astype(BF16) for gi, h in units]
    o_h = [jnp.dot(attn[u], jnp.where(v_lane == GLA_DV * h, pro[gi]["v"], 0.0).astype(BF16),
                   preferred_element_type=F32) for u, (gi, h) in enumerate(units)]
    for gi in groups:
        qt, kd, v = pro[gi]["qt"].astype(BF16), pro[gi]["kd"], pro[gi]["v"].astype(BF16)
        from_s = []
        for i in range(nb):
            seq = gi * nb + i
            rows = slice(L * i, L * (i + 1))
            st = s_scr[seq]
            from_s.append(_mm_nt(qt[rows], st))
            update = lax.dot_general(v[rows], kd[rows], (((0,), (0,)), ((), ())),
                                     preferred_element_type=F32)
            s_scr[seq] = st * jnp.exp(pro[gi]["b_last"][i]) + jnp.where(head_block, update, 0.0)
        o = jnp.concatenate(from_s, axis=0)
        for h in heads:
            o = o + o_h[gi * GLA_HEADS + h]
        ms = _mm_split_lhs(o * o, ones_ref[...]) * (1.0 / GLA_DV)
        g = pro[gi]["g"]
        o_ref[gi * nb:(gi + 1) * nb] = (o * lax.rsqrt(ms + RMS_EPS) * gnorm_ref[...]
                                        * (g * _sigmoid(g))).reshape(nb, L, WIDTH)

    @pl.when(pl.program_id(1) == pl.num_programs(1) - 1)
    def _():
        sfin_ref[...] = s_scr[...]


def _gla(cols, s0, p, ng, nb, L):
    bt, t, _ = cols.shape
    ns = ng * nb
    assert bt % ns == 0 and t % L == 0 and L & (L - 1) == 0
    eye = jnp.eye(GLA_HEADS, dtype=F32)
    s0_blocks = jnp.einsum("bhcv,hg->bhvgc", s0, eye).reshape(bt, WIDTH, GLA_QK)
    kern = functools.partial(_gla_kernel, ng=ng, nb=nb, L=L)
    state = pl.BlockSpec((ns, WIDTH, GLA_QK), lambda b, n: (b, 0, 0))
    o, s_blocks = pl.pallas_call(
        kern,
        grid=(bt // ns, t // L),
        in_specs=[pl.BlockSpec((ns, L, GLA_COLS_PAD), lambda b, n: (b, n, 0)), state,
                  _const_spec((128, GLA_QK)), _const_spec((1, GLA_QK)), _const_spec((1, WIDTH)),
                  _const_spec((WIDTH, WIDTH))],
        out_specs=(pl.BlockSpec((ns, L, WIDTH), lambda b, n: (b, n, 0)), state),
        out_shape=(jax.ShapeDtypeStruct((bt, t, WIDTH), F32),
                   jax.ShapeDtypeStruct((bt, WIDTH, GLA_QK), F32)),
        scratch_shapes=[pltpu.VMEM((ns, WIDTH, GLA_QK), F32)],
        compiler_params=_params("parallel", "arbitrary"),
        name="gla",
    )(cols, s0_blocks, p["gla_w_alpha"], p["gla_b_alpha"], p["gla_norm"], p["head_ones"])
    s5d = s_blocks.reshape(bt, GLA_HEADS, GLA_DV, GLA_HEADS, GLA_DK)
    s_new = jnp.stack([s5d[:, h, :, h, :] for h in range(GLA_HEADS)], axis=1)
    return o, jnp.swapaxes(s_new, 2, 3)


def _rwkv_kernel(z_ref, shift0_ref, s0_ref, mu_ref, w0_ref, w2_ref, a0_ref, a2_ref, g2_ref, kk_ref,
                 ka_ref, rk_ref, lnw_ref, lnb_ref, bd_ref, o_ref, shiftfin_ref, sfin_ref,
                 s_scr, prev_scr, *, ng, nb, L):
    R = nb * L
    HD = RWKV_HEAD
    groups = range(ng)
    heads = range(RWKV_HEADS)
    hsl = [slice(HD * h, HD * (h + 1)) for h in heads]

    @pl.when(pl.program_id(1) == 0)
    def _():
        for gi in groups:
            for i in range(nb):
                for h in heads:
                    s_scr[gi, h, :, HD * i:HD * (i + 1)] = s0_ref[gi * nb + i, h]
        prev_scr[...] = shift0_ref[...]

    rr = lax.broadcasted_iota(jnp.int32, (R, R), 0)
    cc = lax.broadcasted_iota(jnp.int32, (R, R), 1)
    same = (rr & -L) == (cc & -L)
    tri = jnp.logical_and(same, rr >= cc)
    stri = jnp.logical_and(same, rr > cc)
    tri_ones = tri.astype(BF16)
    own = ((lax.broadcasted_iota(jnp.int32, (R, nb * HD), 0) & -L) * HD
           == (lax.broadcasted_iota(jnp.int32, (R, nb * HD), 1) & -HD) * L)
    first = (lax.broadcasted_iota(jnp.int32, (R, RWKV_COLS), 0) & (L - 1)) == 0

    def spread(x):
        return jnp.where(own, jnp.concatenate([x] * nb, axis=1), 0.0).astype(BF16)

    pro = [_rwkv_prologue(gi, z_ref, prev_scr, shiftfin_ref, first, tri_ones, mu_ref, w0_ref, w2_ref,
                          a0_ref, a2_ref, g2_ref, kk_ref, ka_ref, bd_ref, nb, L) for gi in groups]

    units = [(gi, h) for gi in groups for h in heads]
    f32dot = functools.partial(jnp.dot, preferred_element_type=F32)
    vb = [pro[gi]["v"][:, hsl[h]].astype(BF16) for gi, h in units]
    gram = [_mm_nt(jnp.concatenate([pro[gi]["a_til"][:, hsl[h]], pro[gi]["r_til"][:, hsl[h]]], axis=0),
                   jnp.concatenate([pro[gi]["b_til"][:, hsl[h]], pro[gi]["k_til"][:, hsl[h]]], axis=0))
            for gi, h in units]
    from_s = [_mm_nt(jnp.concatenate([spread(pro[gi]["a_til"][:, hsl[h]]),
                                      spread(pro[gi]["r_til"][:, hsl[h]])], axis=0), s_scr[gi, h])
              for gi, h in units]
    pw = [jnp.where(stri, gm[:R, :R], 0.0).astype(BF16) for gm in gram]
    a_ak = [jnp.where(stri, gm[:R, R:], 0.0).astype(BF16) for gm in gram]
    a_r = [jnp.concatenate([jnp.where(tri, gm[R:, :R], 0.0).astype(BF16),
                            jnp.where(tri, gm[R:, R:], 0.0).astype(BF16)], axis=1) for gm in gram]
    n_units = range(len(units))
    sa = [from_s[u][:R] + f32dot(a_ak[u], vb[u]) for u in n_units]
    sa = [sa[u] + f32dot(pw[u], sa[u].astype(BF16)) for u in n_units]
    for _ in range(int(math.log2(L)) - 1):
        pw = [f32dot(pw[u], pw[u]).astype(BF16) for u in n_units]
        sa = [sa[u] + f32dot(pw[u], sa[u].astype(BF16)) for u in n_units]
    sa_v = [jnp.concatenate([sa[u].astype(BF16), vb[u]], axis=0) for u in n_units]
    y = [from_s[u][R:] + f32dot(a_r[u], sa_v[u]) for u in n_units]
    for gi in groups:
        outs = []
        for h in heads:
            yh = y[gi * RWKV_HEADS + h]
            hs = hsl[h]
            mean = jnp.mean(yh, axis=-1, keepdims=True)
            var = jnp.mean(jnp.square(yh - mean), axis=-1, keepdims=True)
            yn = (yh - mean) * lax.rsqrt(var + RWKV_GN_EPS) * lnw_ref[:, hs] + lnb_ref[:, hs]
            bonus = jnp.sum(pro[gi]["r"][:, hs] * pro[gi]["k"][:, hs] * rk_ref[:, hs], axis=-1,
                            keepdims=True)
            outs.append(yn + bonus * pro[gi]["v"][:, hs])
        o_ref[gi * nb:(gi + 1) * nb] = (jnp.concatenate(outs, axis=-1)
                                        * pro[gi]["g"]).reshape(nb, L, WIDTH)
    for u, (gi, h) in enumerate(units):
        p_all = jnp.concatenate([pe[:, hsl[h]] for pe in pro[gi]["p_last"]], axis=1)
        s_scr[gi, h] = s_scr[gi, h] * p_all + _mm_tn(
            sa_v[u], jnp.concatenate([spread(pro[gi]["b_end"][:, hsl[h]]),
                                      spread(pro[gi]["k_end"][:, hsl[h]])], axis=0))

    @pl.when(pl.program_id(1) == pl.num_programs(1) - 1)
    def _():
        for gi in groups:
            for i in range(nb):
                for h in heads:
                    sfin_ref[gi * nb + i, h] = s_scr[gi, h, :, HD * i:HD * (i + 1)]


def _rwkv_prologue(gi, z_ref, prev_scr, shiftfin_ref, first, tri_ones, mu_ref, w0_ref, w2_ref, a0_ref,
                   a2_ref, g2_ref, kk_ref, ka_ref, bd_ref, nb, L):
    R = nb * L
    seqs = range(gi * nb, (gi + 1) * nb)
    z = z_ref[gi * nb:(gi + 1) * nb].reshape(R, RWKV_COLS)
    carried = jnp.concatenate(
        [jnp.broadcast_to(prev_scr[s], (L, RWKV_COLS)) for s in seqs], axis=0)
    prev = jnp.where(first, carried, pltpu.roll(z, 1, axis=0))
    for i, s in enumerate(seqs):
        last = z[L * (i + 1) - 1:L * (i + 1), :]
        prev_scr[s] = last
        shiftfin_ref[s] = last
    zs = z + (prev - z) * mu_ref[...]
    r = zs[:, 0:256]
    k = zs[:, 256:512]
    v = zs[:, 512:768]
    w_lat = zs[:, 768:832]
    a_lat = zs[:, 832:896]
    g_lat = zs[:, 896:1024]
    w = -jax.nn.softplus(-(w0_ref[...] + _mm(jnp.tanh(w_lat), w2_ref[...]))) - 0.5
    log_w = -jnp.exp(w)
    a = _sigmoid(a0_ref[...] + _mm(a_lat, a2_ref[...]))
    g = _mm(_sigmoid(g_lat), g2_ref[...])
    kk = k * kk_ref[...]
    kk = kk / jnp.maximum(jnp.sqrt(_mm_split_lhs(kk * kk, bd_ref[...])), 1e-12)
    k = k * (1.0 + (a - 1.0) * ka_ref[...])
    cum = _mm_split_rhs(tri_ones, log_w)
    p_t = jnp.exp(cum)
    inv_p = jnp.exp(-cum)
    p_last = [p_t[L * (i + 1) - 1:L * (i + 1), :] for i in range(nb)]
    p_end = jnp.concatenate([jnp.broadcast_to(pe, (L, WIDTH)) for pe in p_last], axis=0)
    b_til = kk * a * inv_p
    k_til = k * inv_p
    return dict(r=r, k=k, v=v, g=g, p_last=p_last,
                a_til=-kk * jnp.exp(cum - log_w),
                b_til=b_til, k_til=k_til, r_til=r * p_t,
                b_end=b_til * p_end, k_end=k_til * p_end)


def _rwkv(cols, shift0, s0, p, ng, nb, L):
    bt, t, _ = cols.shape
    ns = ng * nb
    assert bt % ns == 0 and t % L == 0 and L & (L - 1) == 0
    kern = functools.partial(_rwkv_kernel, ng=ng, nb=nb, L=L)
    state = pl.BlockSpec((ns, RWKV_HEADS, RWKV_HEAD, RWKV_HEAD), lambda b, n: (b, 0, 0, 0))
    shift = pl.BlockSpec((ns, 1, RWKV_COLS), lambda b, n: (b, 0, 0))
    vec = _const_spec((1, WIDTH))
    return pl.pallas_call(
        kern,
        grid=(bt // ns, t // L),
        in_specs=[pl.BlockSpec((ns, L, RWKV_COLS), lambda b, n: (b, n, 0)), shift, state,
                  _const_spec((1, RWKV_COLS)), vec, _const_spec((64, WIDTH)), vec,
                  _const_spec((64, WIDTH)), _const_spec((128, WIDTH)), vec, vec, vec, vec, vec,
                  _const_spec((WIDTH, WIDTH))],
        out_specs=(pl.BlockSpec((ns, L, WIDTH), lambda b, n: (b, n, 0)), shift, state),
        out_shape=(jax.ShapeDtypeStruct((bt, t, WIDTH), F32),
                   jax.ShapeDtypeStruct((bt, 1, RWKV_COLS), F32),
                   jax.ShapeDtypeStruct((bt, RWKV_HEADS, RWKV_HEAD, RWKV_HEAD), F32)),
        scratch_shapes=[pltpu.VMEM((ng, RWKV_HEADS, RWKV_HEAD, nb * RWKV_HEAD), F32),
                        pltpu.VMEM((ns, 1, RWKV_COLS), F32)],
        compiler_params=_params("parallel", "arbitrary"),
        name="rwkv",
    )(cols, shift0, s0, p["rwkv_mu"], p["rwkv_w0"], p["rwkv_w2"], p["rwkv_a0"], p["rwkv_a2"],
      p["rwkv_g2"], p["rwkv_k_k"], p["rwkv_k_a"], p["rwkv_r_k"], p["rwkv_ln_w"], p["rwkv_ln_b"],
      p["head_ones"])


SB_DEAD_LOG = -110.0
SB_LANES = 128


def _sb_block(qh, k, v, diag, suffix_ones, acc_scr, carry_scr):
    tq = qh[0].shape[0]
    tk = k.shape[1]
    kb = k.astype(BF16)
    z = jnp.concatenate(
        [jnp.dot(qh[h], kb[SB_HEAD * h:SB_HEAD * (h + 1)], preferred_element_type=F32)
         for h in range(SB_HEADS)], axis=0)
    nlm = jnp.maximum(z, 0.0) + jnp.log(1.0 + jnp.exp(-jnp.abs(z)))
    if diag:
        row = lax.broadcasted_iota(jnp.int32, z.shape, 0) & (tq - 1)
        mask = row > lax.broadcasted_iota(jnp.int32, z.shape, 1)
        nlm = jnp.where(mask, nlm, 0.0)
    sums = _mm_split_lhs(nlm, suffix_ones)
    carry = carry_scr[...]
    if tk <= SB_LANES:
        carry_keys = carry[:, :tk]
    else:
        carry_keys = jnp.concatenate([carry] * (tk // SB_LANES), axis=1)
    wgt = jnp.exp(z - carry_keys - sums)
    if diag:
        wgt = jnp.where(mask, wgt, 0.0)
    wgt = wgt.astype(BF16)
    vb = v.astype(BF16)
    for h in range(SB_HEADS):
        acc_scr[h] += lax.dot_general(wgt[tq * h:tq * (h + 1)], vb[SB_HEAD * h:SB_HEAD * (h + 1)],
                                      (((1,), (1,)), ((), ())), preferred_element_type=F32)
    carry = carry + jnp.broadcast_to(sums[:, 0:1], carry.shape)
    carry_scr[...] = carry
    return -jnp.min(carry)


def _suffix_ones(tk):
    r = lax.broadcasted_iota(jnp.int32, (tk, tk), 0)
    c = lax.broadcasted_iota(jnp.int32, (tk, tk), 1)
    return (r >= c).astype(BF16)


def _sb_heads(q):
    return [q[:, SB_HEAD * h:SB_HEAD * (h + 1)].astype(BF16) for h in range(SB_HEADS)]


def _sb_self_kernel(q_ref, k_ref, v_ref, o_ref, acc_scr, carry_scr, *, tq):
    i = pl.program_id(1)
    acc_scr[...] = jnp.zeros_like(acc_scr)
    carry_scr[...] = jnp.zeros_like(carry_scr)
    qh = _sb_heads(q_ref[...])
    ones = _suffix_ones(tq)

    def keys(j):
        return pl.ds(pl.multiple_of(j * tq, tq), tq)

    live = _sb_block(qh, k_ref[0, :, keys(i)], v_ref[0, :, keys(i)], True, ones, acc_scr, carry_scr)

    def body(state):
        j, _ = state
        m = _sb_block(qh, k_ref[0, :, keys(j)], v_ref[0, :, keys(j)], False, ones, acc_scr, carry_scr)
        return j - 1, m

    lax.while_loop(lambda s: jnp.logical_and(s[0] >= 0, s[1] > SB_DEAD_LOG), body, (i - 1, live))
    o_ref[...] = jnp.concatenate([acc_scr[h] for h in range(SB_HEADS)], axis=-1)


def _sb_self(q2d, k_t, v_t, tq):
    bt, _, t = k_t.shape
    nq = t // tq
    assert tq & (tq - 1) == 0
    kern = functools.partial(_sb_self_kernel, tq=tq)
    seq = pl.BlockSpec((1, WIDTH, t), lambda b, i: (b, 0, 0))
    blk = pl.BlockSpec((tq, WIDTH), lambda b, i: (b * nq + i, 0))
    return pl.pallas_call(
        kern,
        grid=(bt, nq),
        in_specs=[blk, seq, seq],
        out_specs=blk,
        out_shape=jax.ShapeDtypeStruct((bt * t, WIDTH), F32),
        scratch_shapes=[pltpu.VMEM((SB_HEADS, tq, SB_HEAD), F32),
                        pltpu.VMEM((SB_HEADS * tq, SB_LANES), F32)],
        compiler_params=_params("parallel", "arbitrary"),
        name="sb_self",
    )(q2d, k_t, v_t)


SB_OLDER_SLOT = 2


def _sb_past_kernel(q_ref, k_ref, v_ref, pk_hbm, pv_hbm, o_ref, kbuf, vbuf, sem, acc_scr, carry_scr,
                    *, layer, t, tkp, np_):
    b = pl.program_id(0)
    slot = b % 2
    qh = _sb_heads(q_ref[...])

    def fetch(seq, blk, dst):
        keys = pl.ds(pl.multiple_of(blk * tkp, tkp), tkp)
        return (pltpu.make_async_copy(pk_hbm.at[layer, seq, :, keys], kbuf.at[dst], sem.at[0, dst]),
                pltpu.make_async_copy(pv_hbm.at[layer, seq, :, keys], vbuf.at[dst], sem.at[1, dst]))

    @pl.when(b == 0)
    def _():
        for cp in fetch(0, np_ - 1, 0):
            cp.start()

    @pl.when(b + 1 < pl.num_programs(0))
    def _():
        for cp in fetch(b + 1, np_ - 1, 1 - slot):
            cp.start()

    acc_scr[...] = jnp.zeros_like(acc_scr)
    carry_scr[...] = jnp.zeros_like(carry_scr)
    _sb_block(qh, k_ref[0], v_ref[0], True, _suffix_ones(t), acc_scr, carry_scr)

    ones = _suffix_ones(tkp)
    for cp in fetch(b, np_ - 1, slot):
        cp.wait()
    live = _sb_block(qh, kbuf[slot], vbuf[slot], False, ones, acc_scr, carry_scr)

    def body(state):
        j, _ = state
        copies = fetch(b, j, SB_OLDER_SLOT)
        for cp in copies:
            cp.start()
        for cp in copies:
            cp.wait()
        m = _sb_block(qh, kbuf[SB_OLDER_SLOT], vbuf[SB_OLDER_SLOT], False, ones, acc_scr, carry_scr)
        return j - 1, m

    lax.while_loop(lambda s: jnp.logical_and(s[0] >= 0, s[1] > SB_DEAD_LOG), body, (np_ - 2, live))
    o_ref[...] = jnp.concatenate([acc_scr[h] for h in range(SB_HEADS)], axis=-1)


def _sb_past(q2d, k_t, v_t, past_k_t, past_v_t, layer, tkp):
    bt, _, t = k_t.shape
    np_ = past_k_t.shape[3] // tkp
    assert t & (t - 1) == 0 and past_k_t.shape[3] % tkp == 0
    kern = functools.partial(_sb_past_kernel, layer=layer, t=t, tkp=tkp, np_=np_)
    qblk = pl.BlockSpec((t, WIDTH), lambda b: (b, 0))
    new = pl.BlockSpec((1, WIDTH, t), lambda b: (b, 0, 0))
    hbm = pl.BlockSpec(memory_space=pl.ANY)
    return pl.pallas_call(
        kern,
        grid=(bt,),
        in_specs=[qblk, new, new, hbm, hbm],
        out_specs=qblk,
        out_shape=jax.ShapeDtypeStruct((bt * t, WIDTH), F32),
        scratch_shapes=[pltpu.VMEM((3, WIDTH, tkp), F32), pltpu.VMEM((3, WIDTH, tkp), F32),
                        pltpu.SemaphoreType.DMA((2, 3)),
                        pltpu.VMEM((SB_HEADS, t, SB_HEAD), F32),
                        pltpu.VMEM((SB_HEADS * t, SB_LANES), F32)],
        compiler_params=_params("arbitrary"),
        name="sb_past",
    )(q2d, k_t, v_t, past_k_t, past_v_t)


def _merge_ffn_kernel(x_ref, oa_ref, ob_ref, oc_ref, od_ref, gate_ref, wb_ref, wout_ref, nffn_ref,
                      wg_ref, wu_ref, wd_ref, y_ref):
    merged = None
    for n, o_ref in enumerate((oa_ref, ob_ref, oc_ref, od_ref)):
        term = gate_ref[:, n * D_MODEL:(n + 1) * D_MODEL].astype(F32) * _mm(o_ref[...], wb_ref[n])
        merged = term if merged is None else merged + term
    x = x_ref[...] + _mm(merged, wout_ref[...])
    h = x * lax.rsqrt(jnp.mean(x * x, axis=-1, keepdims=True) + RMS_EPS) * nffn_ref[...]
    hb = h.astype(BF16)
    gate = jnp.dot(hb, wg_ref[...], preferred_element_type=F32)
    up = jnp.dot(hb, wu_ref[...], preferred_element_type=F32)
    y_ref[...] = x + _mm(gate * _sigmoid(gate) * up, wd_ref[...])


def _merge_ffn(x2d, oa, ob, oc, od, gates, p, tm):
    n = x2d.shape[0]
    row = lambda width: pl.BlockSpec((tm, width), lambda i: (i, 0))
    return pl.pallas_call(
        _merge_ffn_kernel,
        grid=(n // tm,),
        in_specs=[row(D_MODEL), row(WIDTH), row(WIDTH), row(WIDTH), row(WIDTH), row(GATE_COLS),
                  _const_spec((N_BRANCH, WIDTH, D_MODEL)), _const_spec((D_MODEL, D_MODEL)),
                  _const_spec((1, D_MODEL)), _const_spec((D_MODEL, D_FF)),
                  _const_spec((D_MODEL, D_FF)), _const_spec((D_FF, D_MODEL))],
        out_specs=row(D_MODEL),
        out_shape=jax.ShapeDtypeStruct((n, D_MODEL), F32),
        compiler_params=_params("parallel"),
        name="merge_ffn",
    )(x2d, oa, ob, oc, od, gates, p["w_branch"], p["w_out"], p["norm_ffn"], p["w_ffn_gate"],
      p["w_ffn_up"], p["w_ffn_down"])


def _prep_layer(w):
    p = {}
    w_in = w["w_in"]
    o_gla = WIDTH
    o_alat = o_gla + 2 * GLA_QK + 2 * WIDTH
    o_rwkv = o_alat + GLA_RANK
    o_sb = o_rwkv + RWKV_COLS
    o_sbk = o_sb + WIDTH
    o_gate = o_sb + SB_COLS
    pad = jnp.zeros((D_MODEL, 128 - GLA_RANK), w_in.dtype)
    p["w_in"] = jnp.concatenate(
        [w_in[:, :o_rwkv], pad, w_in[:, o_rwkv:o_sbk], w_in[:, o_gate:]], axis=1).astype(BF16)
    p["w_kv_t"] = w_in[:, o_sbk:o_gate].T.astype(BF16)
    p["norm_mix"] = w["norm_mix"].reshape(1, D_MODEL)

    lam = lax.complex(w["s5_a_re"], w["s5_a_im"])
    dt = jnp.exp(w["s5_log_dt"])[:, None]
    a_bar = jnp.exp(lam * dt)
    b_bar = ((a_bar - 1.0) / lam)[..., None] * lax.complex(w["s5_b_re"], w["s5_b_im"])
    eye = jnp.eye(S5_GROUPS, dtype=F32)

    def in_map(m):
        return jnp.einsum("gpc,gh->gchp", m, eye).reshape(WIDTH, S5_FLAT)

    def out_map(m):
        return jnp.einsum("gcp,gh->gphc", m, eye).reshape(S5_FLAT, WIDTH)

    p["s5_bbig"] = jnp.concatenate([in_map(b_bar.real), in_map(b_bar.imag)], axis=1).astype(BF16)
    p["s5_cbig"] = jnp.concatenate([out_map(w["s5_c_re"]), out_map(-w["s5_c_im"])], axis=0).astype(BF16)
    p["s5_ar"] = a_bar.real.reshape(1, S5_FLAT)
    p["s5_ai"] = a_bar.imag.reshape(1, S5_FLAT)
    p["s5_d"] = w["s5_d"].reshape(1, WIDTH)
    p["s5_w_glu"] = w["s5_w_glu"].astype(BF16)
    p["s5_b_glu"] = w["s5_b_glu"].reshape(1, WIDTH)

    p["gla_w_alpha"] = jnp.concatenate(
        [w["gla_w_alpha"], jnp.zeros((128 - GLA_RANK, GLA_QK), F32)], axis=0).astype(BF16)
    p["gla_b_alpha"] = w["gla_b_alpha"].reshape(1, GLA_QK)
    p["gla_norm"] = w["gla_norm"].reshape(1, WIDTH)

    for name in ("rwkv_w0", "rwkv_a0", "rwkv_k_k", "rwkv_k_a", "rwkv_r_k", "rwkv_ln_w", "rwkv_ln_b"):
        p[name] = w[name].reshape(1, WIDTH)
    p["rwkv_mu"] = w["rwkv_mu"].reshape(1, RWKV_COLS)
    for name in ("rwkv_w2", "rwkv_a2", "rwkv_g2"):
        p[name] = w[name].astype(BF16)
    head = jnp.arange(WIDTH) // RWKV_HEAD
    p["head_ones"] = (head[:, None] == head[None, :]).astype(BF16)

    p["sb_q_norm"] = jnp.tile(w["sb_q_norm"] * (SB_HEAD ** -0.5), SB_HEADS).reshape(1, WIDTH)
    p["sb_k_norm"] = jnp.tile(w["sb_k_norm"], SB_HEADS).reshape(WIDTH, 1)

    p["w_branch"] = w["w_branch"].astype(BF16)
    p["w_out"] = w["w_out"].astype(BF16)
    p["norm_ffn"] = w["norm_ffn"].reshape(1, D_MODEL)
    for name in ("w_ffn_gate", "w_ffn_up", "w_ffn_down"):
        p[name] = w[name].astype(BF16)
    return p


def _pick(n, prefs):
    for c in prefs:
        if n % c == 0:
            return c
    return n


def _layer(x, past_k_t, past_v_t, layer, s5_0, gla_0, rwkv_0, shift_0, p):
    bt, t, _ = x.shape
    n = bt * t
    x2d = x.reshape(n, D_MODEL)
    tm = _pick(n, (256, 128, 64, 32, 16, 8))
    nt = t // tm if t % tm == 0 else None
    assert nt is not None or tm % t == 0
    tm_in = INPROJ_ROWS if nt is not None and t % INPROJ_ROWS == 0 else tm
    u, c_gla, c_rwkv, q, k_t, v_t, gates = _inproj(x2d, p, bt, t, tm_in,
                                                   None if nt is None else t // tm_in)

    x0 = jnp.concatenate([s5_0[..., 0].reshape(bt, S5_FLAT), s5_0[..., 1].reshape(bt, S5_FLAT)], axis=1)
    o_a, xfin = _s5(u.reshape(bt, t, WIDTH), x0, p, _pick(t, (128, 64, 32, 16, 8)))
    o_a = o_a.reshape(n, WIDTH)
    s5_new = jnp.stack([xfin[:, :S5_FLAT].reshape(bt, S5_GROUPS, S5_STATE),
                        xfin[:, S5_FLAT:].reshape(bt, S5_GROUPS, S5_STATE)], axis=-1)

    chunk = _pick(t, (64, 32, 16, 8))
    nseq = _pick(bt, (RWKV_ROWS // chunk, 4, 2, 1))
    ngrp = _pick(bt // nseq, (RWKV_GROUPS, 1))
    o_b, gla_new = _gla(c_gla.reshape(bt, t, GLA_COLS_PAD), gla_0, p, ngrp, nseq, chunk)

    o_c, shift_new, rwkv_new = _rwkv(c_rwkv.reshape(bt, t, RWKV_COLS),
                                     shift_0.reshape(bt, 1, RWKV_COLS), rwkv_0, p, ngrp, nseq, chunk)

    if past_k_t is None:
        o_d = _sb_self(q, k_t, v_t, _pick(t, (256, 128, 64, 32, 16, 8)))
    else:
        tkp = _pick(past_k_t.shape[3], (256, 128))
        o_d = _sb_past(q, k_t, v_t, past_k_t, past_v_t, layer, tkp)

    y = _merge_ffn(x2d, o_a, o_b.reshape(n, WIDTH), o_c.reshape(n, WIDTH), o_d, gates, p, tm_in)

    def cache_rows(a_t):
        return a_t.reshape(bt, SB_HEADS, SB_HEAD, t).transpose(0, 3, 1, 2)

    states = (cache_rows(k_t), cache_rows(v_t), s5_new, gla_new, rwkv_new,
              shift_new.reshape(bt, RWKV_COLS))
    return y.reshape(bt, t, D_MODEL), states


def _keys_on_lanes(cache):
    d, b, pl_, _, _ = cache.shape
    return cache.transpose(0, 1, 3, 4, 2).reshape(d, b, WIDTH, pl_)


def _trunk(x, past_k, past_v, s5_0, gla_0, rwkv_0, shift_0, layers):
    past_k_t = None if past_k is None else _keys_on_lanes(past_k)
    past_v_t = None if past_v is None else _keys_on_lanes(past_v)
    per_layer = []
    for l, p in enumerate(layers):
        x, st = _layer(x, past_k_t, past_v_t, l, s5_0[l], gla_0[l], rwkv_0[l], shift_0[l], p)
        per_layer.append(st)
    return x, [jnp.stack([st[i] for st in per_layer]) for i in range(6)]


def kernel(x_prompt, x_sample, cache_sb_k, cache_sb_v, state_s5, state_gla, state_rwkv, state_rwkv_shift, norm_mix, w_in, s5_a_re, s5_a_im, s5_log_dt, s5_b_re, s5_b_im, s5_c_re, s5_c_im, s5_d, s5_w_glu, s5_b_glu, gla_w_alpha, gla_b_alpha, gla_norm, rwkv_mu, rwkv_w0, rwkv_w2, rwkv_a0, rwkv_a2, rwkv_g2, rwkv_k_k, rwkv_k_a, rwkv_r_k, rwkv_ln_w, rwkv_ln_b, sb_q_norm, sb_k_norm, w_branch, w_out, norm_ffn, w_ffn_gate, w_ffn_up, w_ffn_down):
    weights = dict(norm_mix=norm_mix, w_in=w_in, s5_a_re=s5_a_re, s5_a_im=s5_a_im, s5_log_dt=s5_log_dt,
                   s5_b_re=s5_b_re, s5_b_im=s5_b_im, s5_c_re=s5_c_re, s5_c_im=s5_c_im, s5_d=s5_d,
                   s5_w_glu=s5_w_glu, s5_b_glu=s5_b_glu, gla_w_alpha=gla_w_alpha, gla_b_alpha=gla_b_alpha,
                   gla_norm=gla_norm, rwkv_mu=rwkv_mu, rwkv_w0=rwkv_w0, rwkv_w2=rwkv_w2, rwkv_a0=rwkv_a0,
                   rwkv_a2=rwkv_a2, rwkv_g2=rwkv_g2, rwkv_k_k=rwkv_k_k, rwkv_k_a=rwkv_k_a, rwkv_r_k=rwkv_r_k,
                   rwkv_ln_w=rwkv_ln_w, rwkv_ln_b=rwkv_ln_b, sb_q_norm=sb_q_norm, sb_k_norm=sb_k_norm,
                   w_branch=w_branch, w_out=w_out, norm_ffn=norm_ffn, w_ffn_gate=w_ffn_gate,
                   w_ffn_up=w_ffn_up, w_ffn_down=w_ffn_down)
    depth = w_in.shape[0]
    layers = [_prep_layer({name: arr[l] for name, arr in weights.items()}) for l in range(depth)]

    bp = x_prompt.shape[0]
    y_prompt, p_states = _trunk(
        x_prompt, None, None,
        jnp.zeros((depth, bp, S5_GROUPS, S5_STATE, 2), F32),
        jnp.zeros((depth, bp, GLA_HEADS, GLA_DK, GLA_DV), F32),
        jnp.zeros((depth, bp, RWKV_HEADS, RWKV_HEAD, RWKV_HEAD), F32),
        jnp.zeros((depth, bp, RWKV_COLS), F32), layers)
    y_sample, s_states = _trunk(x_sample, cache_sb_k, cache_sb_v, state_s5, state_gla, state_rwkv,
                                state_rwkv_shift, layers)
    return (y_prompt, y_sample, *p_states, *s_states)
```

```python
import functools
import math

import jax
import jax.numpy as jnp
from jax import lax
from jax.experimental import pallas as pl
from jax.experimental.pallas import tpu as pltpu

F32 = jnp.float32
BF16 = jnp.bfloat16

D_MODEL = 1024
WIDTH = 256
N_BRANCH = 4
S5_GROUPS, S5_GROUP, S5_STATE = 16, 16, 64
S5_FLAT = S5_GROUPS * S5_STATE
GLA_HEADS, GLA_DK, GLA_DV, GLA_RANK, GLA_TAU = 4, 32, 64, 16, 16.0
GLA_QK = GLA_HEADS * GLA_DK
GLA_COLS_PAD = 2 * GLA_QK + 2 * WIDTH + 128
RWKV_HEADS, RWKV_HEAD = 4, 64
RWKV_COLS = 1024
RWKV_GN_EPS = 64e-5
RWKV_ROWS = 256
RWKV_GROUPS = 2
SB_HEADS, SB_HEAD = 4, 64
SB_COLS = 3 * WIDTH
GATE_COLS = N_BRANCH * D_MODEL
D_FF = 2816
RMS_EPS = 1e-6

_C_S5 = 0
_C_GLA = _C_S5 + WIDTH
_C_RWKV = _C_GLA + GLA_COLS_PAD
_C_SBQ = _C_RWKV + RWKV_COLS
_C_GATE = _C_SBQ + WIDTH
_C_END = _C_GATE + GATE_COLS

VMEM_LIMIT = 56 * 1024 * 1024
INPROJ_ROWS = 512


def _params(*sem):
    return pltpu.CompilerParams(dimension_semantics=sem, vmem_limit_bytes=VMEM_LIMIT)


def _const_spec(shape):
    nd = len(shape)
    return pl.BlockSpec(shape, lambda *_: (0,) * nd, pipeline_mode=pl.Buffered(1))


def _mm(a, b):
    return jnp.dot(a.astype(BF16), b.astype(BF16), preferred_element_type=F32)


def _mm_nt(a, b):
    return lax.dot_general(a.astype(BF16), b.astype(BF16), (((1,), (1,)), ((), ())),
                           preferred_element_type=F32)


def _mm_tn(a, b):
    return lax.dot_general(a.astype(BF16), b.astype(BF16), (((0,), (0,)), ((), ())),
                           preferred_element_type=F32)


def _split(a):
    bits = lax.bitcast_convert_type(a, jnp.uint32) & jnp.uint32(0xFFFF0000)
    hi = lax.bitcast_convert_type(bits, F32)
    return hi.astype(BF16), (a - hi).astype(BF16)


def _mm_split_lhs(a, b01):
    hi, lo = _split(a)
    return (jnp.dot(hi, b01, preferred_element_type=F32)
            + jnp.dot(lo, b01, preferred_element_type=F32))


def _mm_split_rhs(a01, b):
    hi, lo = _split(b)
    return (jnp.dot(a01, hi, preferred_element_type=F32)
            + jnp.dot(a01, lo, preferred_element_type=F32))


def _log_sigmoid(z):
    return jnp.minimum(z, 0.0) - jnp.log1p(jnp.exp(-jnp.abs(z)))


def _sigmoid(z):
    return 1.0 / (1.0 + jnp.exp(-z))


def _tri(n, strict=False):
    r = lax.broadcasted_iota(jnp.int32, (n, n), 0)
    c = lax.broadcasted_iota(jnp.int32, (n, n), 1)
    return (r > c) if strict else (r >= c)


def _inproj_kernel(*refs, nseq):
    x_ref, g_ref, w_ref, wkv_ref, qn_ref, kn_ref, ones_ref = refs[:7]
    s5_ref, gla_ref, rwkv_ref, q_ref, kt_ref, vt_ref, gate_ref = refs[-7:]
    x = x_ref[...]
    h = x * lax.rsqrt(jnp.mean(x * x, axis=-1, keepdims=True) + RMS_EPS) * g_ref[...]
    hb = h.astype(BF16)

    def mm(lo, hi):
        return jnp.dot(hb, w_ref[:, lo:hi], preferred_element_type=F32)

    s5_ref[...] = mm(_C_S5, _C_GLA)
    gla_ref[...] = mm(_C_GLA, _C_RWKV)
    rwkv_ref[...] = mm(_C_RWKV, _C_SBQ)
    for n in range(N_BRANCH):
        lo = _C_GATE + n * D_MODEL
        gate_ref[:, n * D_MODEL:(n + 1) * D_MODEL] = _sigmoid(mm(lo, lo + D_MODEL)).astype(BF16)

    q = mm(_C_SBQ, _C_GATE)
    q_ms = _mm_split_lhs(q * q, ones_ref[...]) * (1.0 / SB_HEAD)
    q_ref[...] = (q * lax.rsqrt(q_ms + RMS_EPS) * qn_ref[...]).astype(BF16)

    kv_t = lax.dot_general(wkv_ref[...], hb, (((1,), (1,)), ((), ())), preferred_element_type=F32)
    k_t = []
    for hd in range(SB_HEADS):
        kh = kv_t[SB_HEAD * hd:SB_HEAD * (hd + 1)]
        k_t.append(kh * lax.rsqrt(jnp.mean(kh * kh, axis=0, keepdims=True) + RMS_EPS))
    k_t = jnp.concatenate(k_t, axis=0) * kn_ref[...]
    v_t = kv_t[WIDTH:2 * WIDTH]
    t = kt_ref.shape[3]
    for s in range(nseq):
        kt_ref[0, s] = k_t[:, t * s:t * (s + 1)]
        vt_ref[0, s] = v_t[:, t * s:t * (s + 1)]
    for other in range(1, kt_ref.shape[0]):
        kt_ref[other] = jnp.zeros(kt_ref.shape[1:], F32)
        vt_ref[other] = jnp.zeros(vt_ref.shape[1:], F32)


def _inproj(x2d, p, bt, t, tm, nt, layer, depth, kv_prev):
    n = x2d.shape[0]
    row = lambda width: pl.BlockSpec((tm, width), lambda i: (i, 0))
    assert (layer == 0) == (kv_prev is None)
    nlay = depth if layer == 0 else 1
    if nt is None:
        nseq = tm // t
        kv_spec = pl.BlockSpec((nlay, nseq, WIDTH, t), lambda i: (layer, i, 0, 0))
    else:
        nseq = 1
        kv_spec = pl.BlockSpec((nlay, 1, WIDTH, tm), lambda i: (layer, i // nt, 0, i % nt))
    kv_shape = jax.ShapeDtypeStruct((depth, bt, WIDTH, t), F32)
    in_specs = [row(D_MODEL), _const_spec((1, D_MODEL)), _const_spec((D_MODEL, _C_END)),
                _const_spec((2 * WIDTH, D_MODEL)), _const_spec((1, WIDTH)),
                _const_spec((WIDTH, 1)), _const_spec((WIDTH, WIDTH))]
    operands = [x2d, p["norm_mix"], p["w_in"], p["w_kv_t"], p["sb_q_norm"], p["sb_k_norm"],
                p["head_ones"]]
    aliases = {}
    if kv_prev is not None:
        aliases = {len(operands): 4, len(operands) + 1: 5}
        in_specs += [pl.BlockSpec(memory_space=pl.ANY)] * 2
        operands += list(kv_prev)
    return pl.pallas_call(
        functools.partial(_inproj_kernel, nseq=nseq),
        grid=(n // tm,),
        in_specs=in_specs,
        out_specs=(row(WIDTH), row(GLA_COLS_PAD), row(RWKV_COLS), row(WIDTH), kv_spec, kv_spec,
                   row(GATE_COLS)),
        out_shape=(jax.ShapeDtypeStruct((n, WIDTH), F32),
                   jax.ShapeDtypeStruct((n, GLA_COLS_PAD), F32),
                   jax.ShapeDtypeStruct((n, RWKV_COLS), F32),
                   jax.ShapeDtypeStruct((n, WIDTH), BF16), kv_shape, kv_shape,
                   jax.ShapeDtypeStruct((n, GATE_COLS), BF16)),
        input_output_aliases=aliases,
        compiler_params=_params("parallel"),
        name="inproj",
    )(*operands)


def _s5_kernel(u_ref, x0_ref, bbig_ref, cbig_ref, ar_ref, ai_ref, d_ref, wglu_ref, bglu_ref,
               o_ref, xfin_ref, u_scr, bu_scr, xs_scr, x_scr, *, nb, lc):
    @pl.when(pl.program_id(0) == 0)
    def _():
        x_scr[...] = x0_ref[...]

    for t in range(lc):
        u_scr[t * nb:(t + 1) * nb, :] = u_ref[:, t, :]
    u = u_scr[...]
    ub = u.astype(BF16)
    bu_scr[:, 0:S5_FLAT] = jnp.dot(ub, bbig_ref[:, 0:S5_FLAT], preferred_element_type=F32)
    bu_scr[:, S5_FLAT:2 * S5_FLAT] = jnp.dot(ub, bbig_ref[:, S5_FLAT:2 * S5_FLAT],
                                             preferred_element_type=F32)
    ar = jnp.broadcast_to(ar_ref[...], (nb, S5_FLAT))
    ai = jnp.broadcast_to(ai_ref[...], (nb, S5_FLAT))
    xr = x_scr[:, 0:S5_FLAT]
    xi = x_scr[:, S5_FLAT:2 * S5_FLAT]
    for t in range(lc):
        rows = slice(t * nb, (t + 1) * nb)
        xr, xi = (ar * xr - ai * xi + bu_scr[rows, 0:S5_FLAT],
                  ar * xi + ai * xr + bu_scr[rows, S5_FLAT:2 * S5_FLAT])
        xs_scr[rows, 0:S5_FLAT] = xr.astype(BF16)
        xs_scr[rows, S5_FLAT:2 * S5_FLAT] = xi.astype(BF16)
    x_scr[:, 0:S5_FLAT] = xr
    x_scr[:, S5_FLAT:2 * S5_FLAT] = xi
    xfin_ref[...] = x_scr[...]
    y = (jnp.dot(xs_scr[:, 0:S5_FLAT], cbig_ref[0:S5_FLAT, :], preferred_element_type=F32)
         + jnp.dot(xs_scr[:, S5_FLAT:2 * S5_FLAT], cbig_ref[S5_FLAT:2 * S5_FLAT, :],
                   preferred_element_type=F32) + d_ref[...] * u)
    y = jax.nn.gelu(y)
    out = y * _sigmoid(_mm(y, wglu_ref[...]) + bglu_ref[...])
    for t in range(lc):
        o_ref[:, t, :] = out[t * nb:(t + 1) * nb]


def _s5(u, x0, p, lc):
    nb, t, _ = u.shape
    blk = lc * nb
    assert nb % 16 == 0
    kern = functools.partial(_s5_kernel, nb=nb, lc=lc)
    seq = pl.BlockSpec((nb, lc, WIDTH), lambda c: (0, c, 0))
    return pl.pallas_call(
        kern,
        grid=(t // lc,),
        in_specs=[seq,
                  _const_spec((nb, 2 * S5_FLAT)),
                  _const_spec((WIDTH, 2 * S5_FLAT)), _const_spec((2 * S5_FLAT, WIDTH)),
                  _const_spec((1, S5_FLAT)), _const_spec((1, S5_FLAT)), _const_spec((1, WIDTH)),
                  _const_spec((WIDTH, WIDTH)), _const_spec((1, WIDTH))],
        out_specs=(seq, pl.BlockSpec((nb, 2 * S5_FLAT), lambda c: (0, 0))),
        out_shape=(jax.ShapeDtypeStruct((nb, t, WIDTH), F32),
                   jax.ShapeDtypeStruct((nb, 2 * S5_FLAT), F32)),
        scratch_shapes=[pltpu.VMEM((blk, WIDTH), F32), pltpu.VMEM((blk, 2 * S5_FLAT), F32),
                        pltpu.VMEM((blk, 2 * S5_FLAT), BF16), pltpu.VMEM((nb, 2 * S5_FLAT), F32)],
        compiler_params=_params("arbitrary"),
        name="s5",
    )(u, x0, p["s5_bbig"], p["s5_cbig"], p["s5_ar"], p["s5_ai"], p["s5_d"], p["s5_w_glu"],
      p["s5_b_glu"])


def _gla_kernel(z_ref, s0_ref, walpha_ref, balpha_ref, gnorm_ref, ones_ref, o_ref, sfin_ref, s_scr,
                *, ng, nb, L):
    R = nb * L
    groups = range(ng)
    heads = range(GLA_HEADS)

    @pl.when(pl.program_id(1) == 0)
    def _():
        s_scr[...] = s0_ref[...]

    rr = lax.broadcasted_iota(jnp.int32, (R, R), 0)
    cc = lax.broadcasted_iota(jnp.int32, (R, R), 1)
    tri = jnp.logical_and((rr & -L) == (cc & -L), rr >= cc)
    tri_ones = tri.astype(BF16)
    qk_lane = lax.broadcasted_iota(jnp.int32, (1, GLA_QK), 1) & -GLA_DK
    v_lane = lax.broadcasted_iota(jnp.int32, (1, WIDTH), 1) & -GLA_DV
    head_block = ((lax.broadcasted_iota(jnp.int32, (WIDTH, GLA_QK), 0) & -GLA_DV) * GLA_DK
                  == (lax.broadcasted_iota(jnp.int32, (WIDTH, GLA_QK), 1) & -GLA_DK) * GLA_DV)

    def prologue(gi):
        z = z_ref[gi * nb:(gi + 1) * nb].reshape(R, GLA_COLS_PAD)
        q = z[:, 0:GLA_QK] * (GLA_DK ** -0.5)
        k = z[:, GLA_QK:2 * GLA_QK]
        alat = z[:, 2 * GLA_QK + 2 * WIDTH:]
        log_a = _log_sigmoid(_mm(alat, walpha_ref[...]) + balpha_ref[...]) * (1.0 / GLA_TAU)
        b = _mm_split_rhs(tri_ones, log_a)
        b_last = [b[L * (i + 1) - 1:L * (i + 1), :] for i in range(nb)]
        b_end = jnp.concatenate([jnp.broadcast_to(be, (L, GLA_QK)) for be in b_last], axis=0)
        return dict(v=z[:, 2 * GLA_QK:2 * GLA_QK + WIDTH],
                    g=z[:, 2 * GLA_QK + WIDTH:2 * GLA_QK + 2 * WIDTH], b_last=b_last,
                    qt=q * jnp.exp(b), kt=(k * jnp.exp(-b)).astype(BF16),
                    kd=(k * jnp.exp(b_end - b)).astype(BF16))

    pro = [prologue(gi) for gi in groups]
    units = [(gi, h) for gi in groups for h in heads]
    attn = [jnp.where(tri, _mm_nt(jnp.where(qk_lane == GLA_DK * h, pro[gi]["qt"], 0.0),
                                  pro[gi]["kt"]), 0.0).astype(BF16) for gi, h in units]
    o_h = [jnp.dot(attn[u], jnp.where(v_lane == GLA_DV * h, pro[gi]["v"], 0.0).astype(BF16),
                   preferred_element_type=F32) for u, (gi, h) in enumerate(units)]
    for gi in groups:
        qt, kd, v = pro[gi]["qt"].astype(BF16), pro[gi]["kd"], pro[gi]["v"].astype(BF16)
        from_s = []
        for i in range(nb):
            seq = gi * nb + i
            rows = slice(L * i, L * (i + 1))
            st = s_scr[seq]
            from_s.append(_mm_nt(qt[rows], st))
            update = lax.dot_general(v[rows], kd[rows], (((0,), (0,)), ((), ())),
                                     preferred_element_type=F32)
            s_scr[seq] = st * jnp.exp(pro[gi]["b_last"][i]) + jnp.where(head_block, update, 0.0)
        o = jnp.concatenate(from_s, axis=0)
        for h in heads:
            o = o + o_h[gi * GLA_HEADS + h]
        ms = _mm_split_lhs(o * o, ones_ref[...]) * (1.0 / GLA_DV)
        g = pro[gi]["g"]
        o_ref[gi * nb:(gi + 1) * nb] = (o * lax.rsqrt(ms + RMS_EPS) * gnorm_ref[...]
                                        * (g * _sigmoid(g))).reshape(nb, L, WIDTH)

    @pl.when(pl.program_id(1) == pl.num_programs(1) - 1)
    def _():
        sfin_ref[...] = s_scr[...]


def _gla(cols, s0, p, ng, nb, L):
    bt, t, _ = cols.shape
    ns = ng * nb
    assert bt % ns == 0 and t % L == 0 and L & (L - 1) == 0
    eye = jnp.eye(GLA_HEADS, dtype=F32)
    s0_blocks = jnp.einsum("bhcv,hg->bhvgc", s0, eye).reshape(bt, WIDTH, GLA_QK)
    kern = functools.partial(_gla_kernel, ng=ng, nb=nb, L=L)
    state = pl.BlockSpec((ns, WIDTH, GLA_QK), lambda b, n: (b, 0, 0))
    o, s_blocks = pl.pallas_call(
        kern,
        grid=(bt // ns, t // L),
        in_specs=[pl.BlockSpec((ns, L, GLA_COLS_PAD), lambda b, n: (b, n, 0)), state,
                  _const_spec((128, GLA_QK)), _const_spec((1, GLA_QK)), _const_spec((1, WIDTH)),
                  _const_spec((WIDTH, WIDTH))],
        out_specs=(pl.BlockSpec((ns, L, WIDTH), lambda b, n: (b, n, 0)), state),
        out_shape=(jax.ShapeDtypeStruct((bt, t, WIDTH), F32),
                   jax.ShapeDtypeStruct((bt, WIDTH, GLA_QK), F32)),
        scratch_shapes=[pltpu.VMEM((ns, WIDTH, GLA_QK), F32)],
        compiler_params=_params("parallel", "arbitrary"),
        name="gla",
    )(cols, s0_blocks, p["gla_w_alpha"], p["gla_b_alpha"], p["gla_norm"], p["head_ones"])
    s5d = s_blocks.reshape(bt, GLA_HEADS, GLA_DV, GLA_HEADS, GLA_DK)
    s_new = jnp.stack([s5d[:, h, :, h, :] for h in range(GLA_HEADS)], axis=1)
    return o, jnp.swapaxes(s_new, 2, 3)


def _rwkv_kernel(z_ref, shift0_ref, s0_ref, mu_ref, w0_ref, w2_ref, a0_ref, a2_ref, g2_ref, kk_ref,
                 ka_ref, rk_ref, lnw_ref, lnb_ref, bd_ref, o_ref, shiftfin_ref, sfin_ref,
                 s_scr, prev_scr, *, ng, nb, L):
    R = nb * L
    HD = RWKV_HEAD
    groups = range(ng)
    heads = range(RWKV_HEADS)
    hsl = [slice(HD * h, HD * (h + 1)) for h in heads]

    @pl.when(pl.program_id(1) == 0)
    def _():
        for gi in groups:
            for i in range(nb):
                for h in heads:
                    s_scr[gi, h, :, HD * i:HD * (i + 1)] = s0_ref[gi * nb + i, h]
        prev_scr[...] = shift0_ref[...]

    rr = lax.broadcasted_iota(jnp.int32, (R, R), 0)
    cc = lax.broadcasted_iota(jnp.int32, (R, R), 1)
    same = (rr & -L) == (cc & -L)
    tri = jnp.logical_and(same, rr >= cc)
    stri = jnp.logical_and(same, rr > cc)
    tri_ones = tri.astype(BF16)
    own = ((lax.broadcasted_iota(jnp.int32, (R, nb * HD), 0) & -L) * HD
           == (lax.broadcasted_iota(jnp.int32, (R, nb * HD), 1) & -HD) * L)
    first = (lax.broadcasted_iota(jnp.int32, (R, RWKV_COLS), 0) & (L - 1)) == 0

    def spread(x):
        return jnp.where(own, jnp.concatenate([x] * nb, axis=1), 0.0).astype(BF16)

    pro = [_rwkv_prologue(gi, z_ref, prev_scr, shiftfin_ref, first, tri_ones, mu_ref, w0_ref, w2_ref,
                          a0_ref, a2_ref, g2_ref, kk_ref, ka_ref, bd_ref, nb, L) for gi in groups]

    units = [(gi, h) for gi in groups for h in heads]
    f32dot = functools.partial(jnp.dot, preferred_element_type=F32)
    vb = [pro[gi]["v"][:, hsl[h]].astype(BF16) for gi, h in units]
    gram = [_mm_nt(jnp.concatenate([pro[gi]["a_til"][:, hsl[h]], pro[gi]["r_til"][:, hsl[h]]], axis=0),
                   jnp.concatenate([pro[gi]["b_til"][:, hsl[h]], pro[gi]["k_til"][:, hsl[h]]], axis=0))
            for gi, h in units]
    from_s = [_mm_nt(jnp.concatenate([spread(pro[gi]["a_til"][:, hsl[h]]),
                                      spread(pro[gi]["r_til"][:, hsl[h]])], axis=0), s_scr[gi, h])
              for gi, h in units]
    pw = [jnp.where(stri, gm[:R, :R], 0.0).astype(BF16) for gm in gram]
    a_ak = [jnp.where(stri, gm[:R, R:], 0.0).astype(BF16) for gm in gram]
    a_r = [jnp.concatenate([jnp.where(tri, gm[R:, :R], 0.0).astype(BF16),
                            jnp.where(tri, gm[R:, R:], 0.0).astype(BF16)], axis=1) for gm in gram]
    n_units = range(len(units))
    sa = [from_s[u][:R] + f32dot(a_ak[u], vb[u]) for u in n_units]
    sa = [sa[u] + f32dot(pw[u], sa[u].astype(BF16)) for u in n_units]
    for _ in range(int(math.log2(L)) - 1):
        pw = [f32dot(pw[u], pw[u]).astype(BF16) for u in n_units]
        sa = [sa[u] + f32dot(pw[u], sa[u].astype(BF16)) for u in n_units]
    sa_v = [jnp.concatenate([sa[u].astype(BF16), vb[u]], axis=0) for u in n_units]
    y = [from_s[u][R:] + f32dot(a_r[u], sa_v[u]) for u in n_units]
    for gi in groups:
        outs = []
        for h in heads:
            yh = y[gi * RWKV_HEADS + h]
            hs = hsl[h]
            mean = jnp.mean(yh, axis=-1, keepdims=True)
            var = jnp.mean(jnp.square(yh - mean), axis=-1, keepdims=True)
            yn = (yh - mean) * lax.rsqrt(var + RWKV_GN_EPS) * lnw_ref[:, hs] + lnb_ref[:, hs]
            bonus = jnp.sum(pro[gi]["r"][:, hs] * pro[gi]["k"][:, hs] * rk_ref[:, hs], axis=-1,
                            keepdims=True)
            outs.append(yn + bonus * pro[gi]["v"][:, hs])
        o_ref[gi * nb:(gi + 1) * nb] = (jnp.concatenate(outs, axis=-1)
                                        * pro[gi]["g"]).reshape(nb, L, WIDTH)
    for u, (gi, h) in enumerate(units):
        p_all = jnp.concatenate([pe[:, hsl[h]] for pe in pro[gi]["p_last"]], axis=1)
        s_scr[gi, h] = s_scr[gi, h] * p_all + _mm_tn(
            sa_v[u], jnp.concatenate([spread(pro[gi]["b_end"][:, hsl[h]]),
                                      spread(pro[gi]["k_end"][:, hsl[h]])], axis=0))

    @pl.when(pl.program_id(1) == pl.num_programs(1) - 1)
    def _():
        for gi in groups:
            for i in range(nb):
                for h in heads:
                    sfin_ref[gi * nb + i, h] = s_scr[gi, h, :, HD * i:HD * (i + 1)]


def _rwkv_prologue(gi, z_ref, prev_scr, shiftfin_ref, first, tri_ones, mu_ref, w0_ref, w2_ref, a0_ref,
                   a2_ref, g2_ref, kk_ref, ka_ref, bd_ref, nb, L):
    R = nb * L
    seqs = range(gi * nb, (gi + 1) * nb)
    z = z_ref[gi * nb:(gi + 1) * nb].reshape(R, RWKV_COLS)
    carried = jnp.concatenate(
        [jnp.broadcast_to(prev_scr[s], (L, RWKV_COLS)) for s in seqs], axis=0)
    prev = jnp.where(first, carried, pltpu.roll(z, 1, axis=0))
    for i, s in enumerate(seqs):
        last = z[L * (i + 1) - 1:L * (i + 1), :]
        prev_scr[s] = last
        shiftfin_ref[s] = last
    zs = z + (prev - z) * mu_ref[...]
    r = zs[:, 0:256]
    k = zs[:, 256:512]
    v = zs[:, 512:768]
    w_lat = zs[:, 768:832]
    a_lat = zs[:, 832:896]
    g_lat = zs[:, 896:1024]
    w = -jax.nn.softplus(-(w0_ref[...] + _mm(jnp.tanh(w_lat), w2_ref[...]))) - 0.5
    log_w = -jnp.exp(w)
    a = _sigmoid(a0_ref[...] + _mm(a_lat, a2_ref[...]))
    g = _mm(_sigmoid(g_lat), g2_ref[...])
    kk = k * kk_ref[...]
    kk = kk / jnp.maximum(jnp.sqrt(_mm_split_lhs(kk * kk, bd_ref[...])), 1e-12)
    k = k * (1.0 + (a - 1.0) * ka_ref[...])
    cum = _mm_split_rhs(tri_ones, log_w)
    p_t = jnp.exp(cum)
    inv_p = jnp.exp(-cum)
    p_last = [p_t[L * (i + 1) - 1:L * (i + 1), :] for i in range(nb)]
    p_end = jnp.concatenate([jnp.broadcast_to(pe, (L, WIDTH)) for pe in p_last], axis=0)
    b_til = kk * a * inv_p
    k_til = k * inv_p
    return dict(r=r, k=k, v=v, g=g, p_last=p_last,
                a_til=-kk * jnp.exp(cum - log_w),
                b_til=b_til, k_til=k_til, r_til=r * p_t,
                b_end=b_til * p_end, k_end=k_til * p_end)


def _rwkv(cols, shift0, s0, p, ng, nb, L):
    bt, t, _ = cols.shape
    ns = ng * nb
    assert bt % ns == 0 and t % L == 0 and L & (L - 1) == 0
    kern = functools.partial(_rwkv_kernel, ng=ng, nb=nb, L=L)
    state = pl.BlockSpec((ns, RWKV_HEADS, RWKV_HEAD, RWKV_HEAD), lambda b, n: (b, 0, 0, 0))
    shift = pl.BlockSpec((ns, 1, RWKV_COLS), lambda b, n: (b, 0, 0))
    vec = _const_spec((1, WIDTH))
    return pl.pallas_call(
        kern,
        grid=(bt // ns, t // L),
        in_specs=[pl.BlockSpec((ns, L, RWKV_COLS), lambda b, n: (b, n, 0)), shift, state,
                  _const_spec((1, RWKV_COLS)), vec, _const_spec((64, WIDTH)), vec,
                  _const_spec((64, WIDTH)), _const_spec((128, WIDTH)), vec, vec, vec, vec, vec,
                  _const_spec((WIDTH, WIDTH))],
        out_specs=(pl.BlockSpec((ns, L, WIDTH), lambda b, n: (b, n, 0)), shift, state),
        out_shape=(jax.ShapeDtypeStruct((bt, t, WIDTH), F32),
                   jax.ShapeDtypeStruct((bt, 1, RWKV_COLS), F32),
                   jax.ShapeDtypeStruct((bt, RWKV_HEADS, RWKV_HEAD, RWKV_HEAD), F32)),
        scratch_shapes=[pltpu.VMEM((ng, RWKV_HEADS, RWKV_HEAD, nb * RWKV_HEAD), F32),
                        pltpu.VMEM((ns, 1, RWKV_COLS), F32)],
        compiler_params=_params("parallel", "arbitrary"),
        name="rwkv",
    )(cols, shift0, s0, p["rwkv_mu"], p["rwkv_w0"], p["rwkv_w2"], p["rwkv_a0"], p["rwkv_a2"],
      p["rwkv_g2"], p["rwkv_k_k"], p["rwkv_k_a"], p["rwkv_r_k"], p["rwkv_ln_w"], p["rwkv_ln_b"],
      p["head_ones"])


SB_DEAD_LOG = -110.0
SB_LANES = 128


def _sb_block(qh, k, v, diag, suffix_ones, acc_scr, carry_scr):
    tq = qh[0].shape[0]
    tk = k.shape[1]
    kb = k.astype(BF16)
    z = jnp.concatenate(
        [jnp.dot(qh[h], kb[SB_HEAD * h:SB_HEAD * (h + 1)], preferred_element_type=F32)
         for h in range(SB_HEADS)], axis=0)
    nlm = jnp.maximum(z, 0.0) + jnp.log(1.0 + jnp.exp(-jnp.abs(z)))
    if diag:
        row = lax.broadcasted_iota(jnp.int32, z.shape, 0) & (tq - 1)
        mask = row > lax.broadcasted_iota(jnp.int32, z.shape, 1)
        nlm = jnp.where(mask, nlm, 0.0)
    sums = _mm_split_lhs(nlm, suffix_ones)
    carry = carry_scr[...]
    if tk <= SB_LANES:
        carry_keys = carry[:, :tk]
    else:
        carry_keys = jnp.concatenate([carry] * (tk // SB_LANES), axis=1)
    wgt = jnp.exp(z - carry_keys - sums)
    if diag:
        wgt = jnp.where(mask, wgt, 0.0)
    wgt = wgt.astype(BF16)
    vb = v.astype(BF16)
    for h in range(SB_HEADS):
        acc_scr[h] += lax.dot_general(wgt[tq * h:tq * (h + 1)], vb[SB_HEAD * h:SB_HEAD * (h + 1)],
                                      (((1,), (1,)), ((), ())), preferred_element_type=F32)
    carry = carry + jnp.broadcast_to(sums[:, 0:1], carry.shape)
    carry_scr[...] = carry
    return -jnp.min(carry)


def _suffix_ones(tk):
    r = lax.broadcasted_iota(jnp.int32, (tk, tk), 0)
    c = lax.broadcasted_iota(jnp.int32, (tk, tk), 1)
    return (r >= c).astype(BF16)


def _sb_heads(q):
    return [q[:, SB_HEAD * h:SB_HEAD * (h + 1)].astype(BF16) for h in range(SB_HEADS)]


def _sb_self_kernel(q_ref, k_ref, v_ref, o_ref, acc_scr, carry_scr, *, tq):
    i = pl.program_id(1)
    acc_scr[...] = jnp.zeros_like(acc_scr)
    carry_scr[...] = jnp.zeros_like(carry_scr)
    qh = _sb_heads(q_ref[...])
    ones = _suffix_ones(tq)

    def keys(j):
        return pl.ds(pl.multiple_of(j * tq, tq), tq)

    live = _sb_block(qh, k_ref[0, 0, :, keys(i)], v_ref[0, 0, :, keys(i)], True, ones,
                     acc_scr, carry_scr)

    def body(state):
        j, _ = state
        m = _sb_block(qh, k_ref[0, 0, :, keys(j)], v_ref[0, 0, :, keys(j)], False, ones,
                      acc_scr, carry_scr)
        return j - 1, m

    lax.while_loop(lambda s: jnp.logical_and(s[0] >= 0, s[1] > SB_DEAD_LOG), body, (i - 1, live))
    o_ref[...] = jnp.concatenate([acc_scr[h] for h in range(SB_HEADS)], axis=-1)


def _sb_self(q2d, k_t, v_t, layer, tq):
    _, bt, _, t = k_t.shape
    nq = t // tq
    assert tq & (tq - 1) == 0
    kern = functools.partial(_sb_self_kernel, tq=tq)
    seq = pl.BlockSpec((1, 1, WIDTH, t), lambda b, i: (layer, b, 0, 0))
    blk = pl.BlockSpec((tq, WIDTH), lambda b, i: (b * nq + i, 0))
    return pl.pallas_call(
        kern,
        grid=(bt, nq),
        in_specs=[blk, seq, seq],
        out_specs=blk,
        out_shape=jax.ShapeDtypeStruct((bt * t, WIDTH), F32),
        scratch_shapes=[pltpu.VMEM((SB_HEADS, tq, SB_HEAD), F32),
                        pltpu.VMEM((SB_HEADS * tq, SB_LANES), F32)],
        compiler_params=_params("parallel", "arbitrary"),
        name="sb_self",
    )(q2d, k_t, v_t)


SB_OLDER_SLOT = 2


def _sb_past_kernel(q_ref, k_ref, v_ref, pk_hbm, pv_hbm, o_ref, kbuf, vbuf, sem, acc_scr, carry_scr,
                    *, layer, t, tkp, np_):
    b = pl.program_id(0)
    slot = b % 2
    qh = _sb_heads(q_ref[...])

    def fetch(seq, blk, dst):
        keys = pl.ds(pl.multiple_of(blk * tkp, tkp), tkp)
        return (pltpu.make_async_copy(pk_hbm.at[layer, seq, :, keys], kbuf.at[dst], sem.at[0, dst]),
                pltpu.make_async_copy(pv_hbm.at[layer, seq, :, keys], vbuf.at[dst], sem.at[1, dst]))

    @pl.when(b == 0)
    def _():
        for cp in fetch(0, np_ - 1, 0):
            cp.start()

    @pl.when(b + 1 < pl.num_programs(0))
    def _():
        for cp in fetch(b + 1, np_ - 1, 1 - slot):
            cp.start()

    acc_scr[...] = jnp.zeros_like(acc_scr)
    carry_scr[...] = jnp.zeros_like(carry_scr)
    _sb_block(qh, k_ref[0, 0], v_ref[0, 0], True, _suffix_ones(t), acc_scr, carry_scr)

    ones = _suffix_ones(tkp)
    for cp in fetch(b, np_ - 1, slot):
        cp.wait()
    live = _sb_block(qh, kbuf[slot], vbuf[slot], False, ones, acc_scr, carry_scr)

    def body(state):
        j, _ = state
        copies = fetch(b, j, SB_OLDER_SLOT)
        for cp in copies:
            cp.start()
        for cp in copies:
            cp.wait()
        m = _sb_block(qh, kbuf[SB_OLDER_SLOT], vbuf[SB_OLDER_SLOT], False, ones, acc_scr, carry_scr)
        return j - 1, m

    lax.while_loop(lambda s: jnp.logical_and(s[0] >= 0, s[1] > SB_DEAD_LOG), body, (np_ - 2, live))
    o_ref[...] = jnp.concatenate([acc_scr[h] for h in range(SB_HEADS)], axis=-1)


def _sb_past(q2d, k_t, v_t, past_k_t, past_v_t, layer, tkp):
    _, bt, _, t = k_t.shape
    np_ = past_k_t.shape[3] // tkp
    assert t & (t - 1) == 0 and past_k_t.shape[3] % tkp == 0
    kern = functools.partial(_sb_past_kernel, layer=layer, t=t, tkp=tkp, np_=np_)
    qblk = pl.BlockSpec((t, WIDTH), lambda b: (b, 0))
    new = pl.BlockSpec((1, 1, WIDTH, t), lambda b: (layer, b, 0, 0))
    hbm = pl.BlockSpec(memory_space=pl.ANY)
    return pl.pallas_call(
        kern,
        grid=(bt,),
        in_specs=[qblk, new, new, hbm, hbm],
        out_specs=qblk,
        out_shape=jax.ShapeDtypeStruct((bt * t, WIDTH), F32),
        scratch_shapes=[pltpu.VMEM((3, WIDTH, tkp), F32), pltpu.VMEM((3, WIDTH, tkp), F32),
                        pltpu.SemaphoreType.DMA((2, 3)),
                        pltpu.VMEM((SB_HEADS, t, SB_HEAD), F32),
                        pltpu.VMEM((SB_HEADS * t, SB_LANES), F32)],
        compiler_params=_params("arbitrary"),
        name="sb_past",
    )(q2d, k_t, v_t, past_k_t, past_v_t)


def _merge_ffn_kernel(x_ref, oa_ref, ob_ref, oc_ref, od_ref, gate_ref, wb_ref, wout_ref, nffn_ref,
                      wg_ref, wu_ref, wd_ref, y_ref):
    merged = None
    for n, o_ref in enumerate((oa_ref, ob_ref, oc_ref, od_ref)):
        term = gate_ref[:, n * D_MODEL:(n + 1) * D_MODEL].astype(F32) * _mm(o_ref[...], wb_ref[n])
        merged = term if merged is None else merged + term
    x = x_ref[...] + _mm(merged, wout_ref[...])
    h = x * lax.rsqrt(jnp.mean(x * x, axis=-1, keepdims=True) + RMS_EPS) * nffn_ref[...]
    hb = h.astype(BF16)
    gate = jnp.dot(hb, wg_ref[...], preferred_element_type=F32)
    up = jnp.dot(hb, wu_ref[...], preferred_element_type=F32)
    y_ref[...] = x + _mm(gate * _sigmoid(gate) * up, wd_ref[...])


def _merge_ffn(x2d, oa, ob, oc, od, gates, p, tm):
    n = x2d.shape[0]
    row = lambda width: pl.BlockSpec((tm, width), lambda i: (i, 0))
    return pl.pallas_call(
        _merge_ffn_kernel,
        grid=(n // tm,),
        in_specs=[row(D_MODEL), row(WIDTH), row(WIDTH), row(WIDTH), row(WIDTH), row(GATE_COLS),
                  _const_spec((N_BRANCH, WIDTH, D_MODEL)), _const_spec((D_MODEL, D_MODEL)),
                  _const_spec((1, D_MODEL)), _const_spec((D_MODEL, D_FF)),
                  _const_spec((D_MODEL, D_FF)), _const_spec((D_FF, D_MODEL))],
        out_specs=row(D_MODEL),
        out_shape=jax.ShapeDtypeStruct((n, D_MODEL), F32),
        compiler_params=_params("parallel"),
        name="merge_ffn",
    )(x2d, oa, ob, oc, od, gates, p["w_branch"], p["w_out"], p["norm_ffn"], p["w_ffn_gate"],
      p["w_ffn_up"], p["w_ffn_down"])


def _prep_layer(w):
    p = {}
    w_in = w["w_in"]
    o_gla = WIDTH
    o_alat = o_gla + 2 * GLA_QK + 2 * WIDTH
    o_rwkv = o_alat + GLA_RANK
    o_sb = o_rwkv + RWKV_COLS
    o_sbk = o_sb + WIDTH
    o_gate = o_sb + SB_COLS
    pad = jnp.zeros((D_MODEL, 128 - GLA_RANK), w_in.dtype)
    p["w_in"] = jnp.concatenate(
        [w_in[:, :o_rwkv], pad, w_in[:, o_rwkv:o_sbk], w_in[:, o_gate:]], axis=1).astype(BF16)
    p["w_kv_t"] = w_in[:, o_sbk:o_gate].T.astype(BF16)
    p["norm_mix"] = w["norm_mix"].reshape(1, D_MODEL)

    lam = lax.complex(w["s5_a_re"], w["s5_a_im"])
    dt = jnp.exp(w["s5_log_dt"])[:, None]
    a_bar = jnp.exp(lam * dt)
    b_bar = ((a_bar - 1.0) / lam)[..., None] * lax.complex(w["s5_b_re"], w["s5_b_im"])
    eye = jnp.eye(S5_GROUPS, dtype=F32)

    def in_map(m):
        return jnp.einsum("gpc,gh->gchp", m, eye).reshape(WIDTH, S5_FLAT)

    def out_map(m):
        return jnp.einsum("gcp,gh->gphc", m, eye).reshape(S5_FLAT, WIDTH)

    p["s5_bbig"] = jnp.concatenate([in_map(b_bar.real), in_map(b_bar.imag)], axis=1).astype(BF16)
    p["s5_cbig"] = jnp.concatenate([out_map(w["s5_c_re"]), out_map(-w["s5_c_im"])], axis=0).astype(BF16)
    p["s5_ar"] = a_bar.real.reshape(1, S5_FLAT)
    p["s5_ai"] = a_bar.imag.reshape(1, S5_FLAT)
    p["s5_d"] = w["s5_d"].reshape(1, WIDTH)
    p["s5_w_glu"] = w["s5_w_glu"].astype(BF16)
    p["s5_b_glu"] = w["s5_b_glu"].reshape(1, WIDTH)

    p["gla_w_alpha"] = jnp.concatenate(
        [w["gla_w_alpha"], jnp.zeros((128 - GLA_RANK, GLA_QK), F32)], axis=0).astype(BF16)
    p["gla_b_alpha"] = w["gla_b_alpha"].reshape(1, GLA_QK)
    p["gla_norm"] = w["gla_norm"].reshape(1, WIDTH)

    for name in ("rwkv_w0", "rwkv_a0", "rwkv_k_k", "rwkv_k_a", "rwkv_r_k", "rwkv_ln_w", "rwkv_ln_b"):
        p[name] = w[name].reshape(1, WIDTH)
    p["rwkv_mu"] = w["rwkv_mu"].reshape(1, RWKV_COLS)
    for name in ("rwkv_w2", "rwkv_a2", "rwkv_g2"):
        p[name] = w[name].astype(BF16)
    head = jnp.arange(WIDTH) // RWKV_HEAD
    p["head_ones"] = (head[:, None] == head[None, :]).astype(BF16)

    p["sb_q_norm"] = jnp.tile(w["sb_q_norm"] * (SB_HEAD ** -0.5), SB_HEADS).reshape(1, WIDTH)
    p["sb_k_norm"] = jnp.tile(w["sb_k_norm"], SB_HEADS).reshape(WIDTH, 1)

    p["w_branch"] = w["w_branch"].astype(BF16)
    p["w_out"] = w["w_out"].astype(BF16)
    p["norm_ffn"] = w["norm_ffn"].reshape(1, D_MODEL)
    for name in ("w_ffn_gate", "w_ffn_up", "w_ffn_down"):
        p[name] = w[name].astype(BF16)
    return p


def _pick(n, prefs):
    for c in prefs:
        if n % c == 0:
            return c
    return n


def _layer(x, past_k_t, past_v_t, layer, depth, kv_prev, s5_0, gla_0, rwkv_0, shift_0, p):
    bt, t, _ = x.shape
    n = bt * t
    x2d = x.reshape(n, D_MODEL)
    tm = _pick(n, (256, 128, 64, 32, 16, 8))
    nt = t // tm if t % tm == 0 else None
    assert nt is not None or tm % t == 0
    tm_in = INPROJ_ROWS if nt is not None and t % INPROJ_ROWS == 0 else tm
    u, c_gla, c_rwkv, q, k_t, v_t, gates = _inproj(x2d, p, bt, t, tm_in,
                                                   None if nt is None else t // tm_in,
                                                   layer, depth, kv_prev)

    x0 = jnp.concatenate([s5_0[..., 0].reshape(bt, S5_FLAT), s5_0[..., 1].reshape(bt, S5_FLAT)], axis=1)
    o_a, xfin = _s5(u.reshape(bt, t, WIDTH), x0, p, _pick(t, (128, 64, 32, 16, 8)))
    o_a = o_a.reshape(n, WIDTH)
    s5_new = jnp.stack([xfin[:, :S5_FLAT].reshape(bt, S5_GROUPS, S5_STATE),
                        xfin[:, S5_FLAT:].reshape(bt, S5_GROUPS, S5_STATE)], axis=-1)

    chunk = _pick(t, (64, 32, 16, 8))
    nseq = _pick(bt, (RWKV_ROWS // chunk, 4, 2, 1))
    ngrp = _pick(bt // nseq, (RWKV_GROUPS, 1))
    o_b, gla_new = _gla(c_gla.reshape(bt, t, GLA_COLS_PAD), gla_0, p, ngrp, nseq, chunk)

    o_c, shift_new, rwkv_new = _rwkv(c_rwkv.reshape(bt, t, RWKV_COLS),
                                     shift_0.reshape(bt, 1, RWKV_COLS), rwkv_0, p, ngrp, nseq, chunk)

    if past_k_t is None:
        o_d = _sb_self(q, k_t, v_t, layer, _pick(t, (256, 128, 64, 32, 16, 8)))
    else:
        tkp = _pick(past_k_t.shape[3], (256, 128))
        o_d = _sb_past(q, k_t, v_t, past_k_t, past_v_t, layer, tkp)

    y = _merge_ffn(x2d, o_a, o_b.reshape(n, WIDTH), o_c.reshape(n, WIDTH), o_d, gates, p, tm_in)
    states = (s5_new, gla_new, rwkv_new, shift_new.reshape(bt, RWKV_COLS))
    return y.reshape(bt, t, D_MODEL), (k_t, v_t), states


def _keys_on_lanes(cache):
    d, b, pl_, _, _ = cache.shape
    return cache.transpose(0, 1, 3, 4, 2).reshape(d, b, WIDTH, pl_)


def _trunk(x, past_k, past_v, s5_0, gla_0, rwkv_0, shift_0, layers):
    past_k_t = None if past_k is None else _keys_on_lanes(past_k)
    past_v_t = None if past_v is None else _keys_on_lanes(past_v)
    per_layer = []
    kv = None
    for l, p in enumerate(layers):
        x, kv, st = _layer(x, past_k_t, past_v_t, l, len(layers), kv, s5_0[l], gla_0[l], rwkv_0[l],
                           shift_0[l], p)
        per_layer.append(st)

    def cache_rows(a_t):
        d, bt, _, t = a_t.shape
        return a_t.reshape(d, bt, SB_HEADS, SB_HEAD, t).transpose(0, 1, 4, 2, 3)

    return x, [cache_rows(kv[0]), cache_rows(kv[1])] + [jnp.stack([st[i] for st in per_layer])
                                                         for i in range(4)]


def kernel(x_prompt, x_sample, cache_sb_k, cache_sb_v, state_s5, state_gla, state_rwkv, state_rwkv_shift, norm_mix, w_in, s5_a_re, s5_a_im, s5_log_dt, s5_b_re, s5_b_im, s5_c_re, s5_c_im, s5_d, s5_w_glu, s5_b_glu, gla_w_alpha, gla_b_alpha, gla_norm, rwkv_mu, rwkv_w0, rwkv_w2, rwkv_a0, rwkv_a2, rwkv_g2, rwkv_k_k, rwkv_k_a, rwkv_r_k, rwkv_ln_w, rwkv_ln_b, sb_q_norm, sb_k_norm, w_branch, w_out, norm_ffn, w_ffn_gate, w_ffn_up, w_ffn_down):
    weights = dict(norm_mix=norm_mix, w_in=w_in, s5_a_re=s5_a_re, s5_a_im=s5_a_im, s5_log_dt=s5_log_dt,
                   s5_b_re=s5_b_re, s5_b_im=s5_b_im, s5_c_re=s5_c_re, s5_c_im=s5_c_im, s5_d=s5_d,
                   s5_w_glu=s5_w_glu, s5_b_glu=s5_b_glu, gla_w_alpha=gla_w_alpha, gla_b_alpha=gla_b_alpha,
                   gla_norm=gla_norm, rwkv_mu=rwkv_mu, rwkv_w0=rwkv_w0, rwkv_w2=rwkv_w2, rwkv_a0=rwkv_a0,
                   rwkv_a2=rwkv_a2, rwkv_g2=rwkv_g2, rwkv_k_k=rwkv_k_k, rwkv_k_a=rwkv_k_a, rwkv_r_k=rwkv_r_k,
                   rwkv_ln_w=rwkv_ln_w, rwkv_ln_b=rwkv_ln_b, sb_q_norm=sb_q_norm, sb_k_norm=sb_k_norm,
                   w_branch=w_branch, w_out=w_out, norm_ffn=norm_ffn, w_ffn_gate=w_ffn_gate,
                   w_ffn_up=w_ffn_up, w_ffn_down=w_ffn_down)
    depth = w_in.shape[0]
    layers = [_prep_layer({name: arr[l] for name, arr in weights.items()}) for l in range(depth)]

    bp = x_prompt.shape[0]
    y_prompt, p_states = _trunk(
        x_prompt, None, None,
        jnp.zeros((depth, bp, S5_GROUPS, S5_STATE, 2), F32),
        jnp.zeros((depth, bp, GLA_HEADS, GLA_DK, GLA_DV), F32),
        jnp.zeros((depth, bp, RWKV_HEADS, RWKV_HEAD, RWKV_HEAD), F32),
        jnp.zeros((depth, bp, RWKV_COLS), F32), layers)
    y_sample, s_states = _trunk(x_sample, cache_sb_k, cache_sb_v, state_s5, state_gla, state_rwkv,
                                state_rwkv_shift, layers)
    return (y_prompt, y_sample, *p_states, *s_states)
```

```python
import functools
import math

import jax
import jax.numpy as jnp
from jax import lax
from jax.experimental import pallas as pl
from jax.experimental.pallas import tpu as pltpu

F32 = jnp.float32
BF16 = jnp.bfloat16

D_MODEL = 1024
WIDTH = 256
N_BRANCH = 4
S5_GROUPS, S5_GROUP, S5_STATE = 16, 16, 64
S5_FLAT = S5_GROUPS * S5_STATE
GLA_HEADS, GLA_DK, GLA_DV, GLA_RANK, GLA_TAU = 4, 32, 64, 16, 16.0
GLA_QK = GLA_HEADS * GLA_DK
GLA_COLS_PAD = 2 * GLA_QK + 2 * WIDTH + 128
RWKV_HEADS, RWKV_HEAD = 4, 64
RWKV_COLS = 1024
RWKV_GN_EPS = 64e-5
RWKV_ROWS = 256
GLA_GROUPS = 4
RWKV_GROUPS = 2
SB_HEADS, SB_HEAD = 4, 64
SB_COLS = 3 * WIDTH
GATE_COLS = N_BRANCH * D_MODEL
D_FF = 2816
RMS_EPS = 1e-6

_C_S5 = 0
_C_GLA = _C_S5 + WIDTH
_C_RWKV = _C_GLA + GLA_COLS_PAD
_C_SBQ = _C_RWKV + RWKV_COLS
_C_GATE = _C_SBQ + WIDTH
_C_END = _C_GATE + GATE_COLS

VMEM_LIMIT = 56 * 1024 * 1024
INPROJ_ROWS = 512


def _params(*sem):
    return pltpu.CompilerParams(dimension_semantics=sem, vmem_limit_bytes=VMEM_LIMIT)


def _const_spec(shape):
    nd = len(shape)
    return pl.BlockSpec(shape, lambda *_: (0,) * nd, pipeline_mode=pl.Buffered(1))


def _mm(a, b):
    return jnp.dot(a.astype(BF16), b.astype(BF16), preferred_element_type=F32)


def _mm_nt(a, b):
    return lax.dot_general(a.astype(BF16), b.astype(BF16), (((1,), (1,)), ((), ())),
                           preferred_element_type=F32)


def _mm_tn(a, b):
    return lax.dot_general(a.astype(BF16), b.astype(BF16), (((0,), (0,)), ((), ())),
                           preferred_element_type=F32)


def _split(a):
    bits = lax.bitcast_convert_type(a, jnp.uint32) & jnp.uint32(0xFFFF0000)
    hi = lax.bitcast_convert_type(bits, F32)
    return hi.astype(BF16), (a - hi).astype(BF16)


def _mm_split_lhs(a, b01):
    hi, lo = _split(a)
    return (jnp.dot(hi, b01, preferred_element_type=F32)
            + jnp.dot(lo, b01, preferred_element_type=F32))


def _mm_split_rhs(a01, b):
    hi, lo = _split(b)
    return (jnp.dot(a01, hi, preferred_element_type=F32)
            + jnp.dot(a01, lo, preferred_element_type=F32))


def _log_sigmoid(z):
    return jnp.minimum(z, 0.0) - jnp.log1p(jnp.exp(-jnp.abs(z)))


def _sigmoid(z):
    return 1.0 / (1.0 + jnp.exp(-z))


def _tri(n, strict=False):
    r = lax.broadcasted_iota(jnp.int32, (n, n), 0)
    c = lax.broadcasted_iota(jnp.int32, (n, n), 1)
    return (r > c) if strict else (r >= c)


def _inproj_kernel(*refs, nseq):
    x_ref, g_ref, w_ref, wkv_ref, qn_ref, kn_ref, ones_ref = refs[:7]
    s5_ref, gla_ref, rwkv_ref, q_ref, kt_ref, vt_ref, gate_ref = refs[-7:]
    x = x_ref[...]
    h = x * lax.rsqrt(jnp.mean(x * x, axis=-1, keepdims=True) + RMS_EPS) * g_ref[...]
    hb = h.astype(BF16)

    def mm(lo, hi):
        return jnp.dot(hb, w_ref[:, lo:hi], preferred_element_type=F32)

    s5_ref[...] = mm(_C_S5, _C_GLA)
    gla_ref[...] = mm(_C_GLA, _C_RWKV)
    rwkv_ref[...] = mm(_C_RWKV, _C_SBQ)
    for n in range(N_BRANCH):
        lo = _C_GATE + n * D_MODEL
        gate_ref[:, n * D_MODEL:(n + 1) * D_MODEL] = _sigmoid(mm(lo, lo + D_MODEL)).astype(BF16)

    q = mm(_C_SBQ, _C_GATE)
    q_ms = _mm_split_lhs(q * q, ones_ref[...]) * (1.0 / SB_HEAD)
    q_ref[...] = (q * lax.rsqrt(q_ms + RMS_EPS) * qn_ref[...]).astype(BF16)

    kv_t = lax.dot_general(wkv_ref[...], hb, (((1,), (1,)), ((), ())), preferred_element_type=F32)
    k_t = []
    for hd in range(SB_HEADS):
        kh = kv_t[SB_HEAD * hd:SB_HEAD * (hd + 1)]
        k_t.append(kh * lax.rsqrt(jnp.mean(kh * kh, axis=0, keepdims=True) + RMS_EPS))
    k_t = jnp.concatenate(k_t, axis=0) * kn_ref[...]
    v_t = kv_t[WIDTH:2 * WIDTH]
    t = kt_ref.shape[3]
    for s in range(nseq):
        kt_ref[0, s] = k_t[:, t * s:t * (s + 1)]
        vt_ref[0, s] = v_t[:, t * s:t * (s + 1)]
    for other in range(1, kt_ref.shape[0]):
        kt_ref[other] = jnp.zeros(kt_ref.shape[1:], F32)
        vt_ref[other] = jnp.zeros(vt_ref.shape[1:], F32)


def _inproj(x2d, p, bt, t, tm, nt, layer, depth, kv_prev):
    n = x2d.shape[0]
    row = lambda width: pl.BlockSpec((tm, width), lambda i: (i, 0))
    assert (layer == 0) == (kv_prev is None)
    nlay = depth if layer == 0 else 1
    if nt is None:
        nseq = tm // t
        kv_spec = pl.BlockSpec((nlay, nseq, WIDTH, t), lambda i: (layer, i, 0, 0))
    else:
        nseq = 1
        kv_spec = pl.BlockSpec((nlay, 1, WIDTH, tm), lambda i: (layer, i // nt, 0, i % nt))
    kv_shape = jax.ShapeDtypeStruct((depth, bt, WIDTH, t), F32)
    in_specs = [row(D_MODEL), _const_spec((1, D_MODEL)), _const_spec((D_MODEL, _C_END)),
                _const_spec((2 * WIDTH, D_MODEL)), _const_spec((1, WIDTH)),
                _const_spec((WIDTH, 1)), _const_spec((WIDTH, WIDTH))]
    operands = [x2d, p["norm_mix"], p["w_in"], p["w_kv_t"], p["sb_q_norm"], p["sb_k_norm"],
                p["head_ones"]]
    aliases = {}
    if kv_prev is not None:
        aliases = {len(operands): 4, len(operands) + 1: 5}
        in_specs += [pl.BlockSpec(memory_space=pl.ANY)] * 2
        operands += list(kv_prev)
    return pl.pallas_call(
        functools.partial(_inproj_kernel, nseq=nseq),
        grid=(n // tm,),
        in_specs=in_specs,
        out_specs=(row(WIDTH), row(GLA_COLS_PAD), row(RWKV_COLS), row(WIDTH), kv_spec, kv_spec,
                   row(GATE_COLS)),
        out_shape=(jax.ShapeDtypeStruct((n, WIDTH), F32),
                   jax.ShapeDtypeStruct((n, GLA_COLS_PAD), F32),
                   jax.ShapeDtypeStruct((n, RWKV_COLS), F32),
                   jax.ShapeDtypeStruct((n, WIDTH), BF16), kv_shape, kv_shape,
                   jax.ShapeDtypeStruct((n, GATE_COLS), BF16)),
        input_output_aliases=aliases,
        compiler_params=_params("parallel"),
        name="inproj",
    )(*operands)


def _s5_kernel(u_ref, x0_ref, bbig_ref, cbig_ref, ar_ref, ai_ref, d_ref, wglu_ref, bglu_ref,
               o_ref, xfin_ref, u_scr, bu_scr, xs_scr, x_scr, *, nb, lc):
    @pl.when(pl.program_id(0) == 0)
    def _():
        x_scr[...] = x0_ref[...]

    for t in range(lc):
        u_scr[t * nb:(t + 1) * nb, :] = u_ref[:, t, :]
    u = u_scr[...]
    ub = u.astype(BF16)
    bu_scr[:, 0:S5_FLAT] = jnp.dot(ub, bbig_ref[:, 0:S5_FLAT], preferred_element_type=F32)
    bu_scr[:, S5_FLAT:2 * S5_FLAT] = jnp.dot(ub, bbig_ref[:, S5_FLAT:2 * S5_FLAT],
                                             preferred_element_type=F32)
    ar = jnp.broadcast_to(ar_ref[...], (nb, S5_FLAT))
    ai = jnp.broadcast_to(ai_ref[...], (nb, S5_FLAT))
    xr = x_scr[:, 0:S5_FLAT]
    xi = x_scr[:, S5_FLAT:2 * S5_FLAT]
    for t in range(lc):
        rows = slice(t * nb, (t + 1) * nb)
        xr, xi = (ar * xr - ai * xi + bu_scr[rows, 0:S5_FLAT],
                  ar * xi + ai * xr + bu_scr[rows, S5_FLAT:2 * S5_FLAT])
        xs_scr[rows, 0:S5_FLAT] = xr.astype(BF16)
        xs_scr[rows, S5_FLAT:2 * S5_FLAT] = xi.astype(BF16)
    x_scr[:, 0:S5_FLAT] = xr
    x_scr[:, S5_FLAT:2 * S5_FLAT] = xi
    xfin_ref[...] = x_scr[...]
    y = (jnp.dot(xs_scr[:, 0:S5_FLAT], cbig_ref[0:S5_FLAT, :], preferred_element_type=F32)
         + jnp.dot(xs_scr[:, S5_FLAT:2 * S5_FLAT], cbig_ref[S5_FLAT:2 * S5_FLAT, :],
                   preferred_element_type=F32) + d_ref[...] * u)
    y = jax.nn.gelu(y)
    out = y * _sigmoid(_mm(y, wglu_ref[...]) + bglu_ref[...])
    for t in range(lc):
        o_ref[:, t, :] = out[t * nb:(t + 1) * nb]


def _s5(u, x0, p, lc):
    nb, t, _ = u.shape
    blk = lc * nb
    assert nb % 16 == 0
    kern = functools.partial(_s5_kernel, nb=nb, lc=lc)
    seq = pl.BlockSpec((nb, lc, WIDTH), lambda c: (0, c, 0))
    return pl.pallas_call(
        kern,
        grid=(t // lc,),
        in_specs=[seq,
                  _const_spec((nb, 2 * S5_FLAT)),
                  _const_spec((WIDTH, 2 * S5_FLAT)), _const_spec((2 * S5_FLAT, WIDTH)),
                  _const_spec((1, S5_FLAT)), _const_spec((1, S5_FLAT)), _const_spec((1, WIDTH)),
                  _const_spec((WIDTH, WIDTH)), _const_spec((1, WIDTH))],
        out_specs=(seq, pl.BlockSpec((nb, 2 * S5_FLAT), lambda c: (0, 0))),
        out_shape=(jax.ShapeDtypeStruct((nb, t, WIDTH), F32),
                   jax.ShapeDtypeStruct((nb, 2 * S5_FLAT), F32)),
        scratch_shapes=[pltpu.VMEM((blk, WIDTH), F32), pltpu.VMEM((blk, 2 * S5_FLAT), F32),
                        pltpu.VMEM((blk, 2 * S5_FLAT), BF16), pltpu.VMEM((nb, 2 * S5_FLAT), F32)],
        compiler_params=_params("arbitrary"),
        name="s5",
    )(u, x0, p["s5_bbig"], p["s5_cbig"], p["s5_ar"], p["s5_ai"], p["s5_d"], p["s5_w_glu"],
      p["s5_b_glu"])


def _gla_kernel(z_ref, s0_ref, walpha_ref, balpha_ref, gnorm_ref, ones_ref, o_ref, sfin_ref, s_scr,
                *, ng, nb, L):
    R = nb * L
    groups = range(ng)
    heads = range(GLA_HEADS)

    @pl.when(pl.program_id(1) == 0)
    def _():
        s_scr[...] = s0_ref[...]

    rr = lax.broadcasted_iota(jnp.int32, (R, R), 0)
    cc = lax.broadcasted_iota(jnp.int32, (R, R), 1)
    tri = jnp.logical_and((rr & -L) == (cc & -L), rr >= cc)
    tri_ones = tri.astype(BF16)
    qk_lane = lax.broadcasted_iota(jnp.int32, (1, GLA_QK), 1) & -GLA_DK
    v_lane = lax.broadcasted_iota(jnp.int32, (1, WIDTH), 1) & -GLA_DV
    head_block = ((lax.broadcasted_iota(jnp.int32, (WIDTH, GLA_QK), 0) & -GLA_DV) * GLA_DK
                  == (lax.broadcasted_iota(jnp.int32, (WIDTH, GLA_QK), 1) & -GLA_DK) * GLA_DV)

    def prologue(gi):
        z = z_ref[gi * nb:(gi + 1) * nb].reshape(R, GLA_COLS_PAD)
        q = z[:, 0:GLA_QK] * (GLA_DK ** -0.5)
        k = z[:, GLA_QK:2 * GLA_QK]
        alat = z[:, 2 * GLA_QK + 2 * WIDTH:]
        log_a = _log_sigmoid(_mm(alat, walpha_ref[...]) + balpha_ref[...]) * (1.0 / GLA_TAU)
        b = _mm_split_rhs(tri_ones, log_a)
        b_last = [b[L * (i + 1) - 1:L * (i + 1), :] for i in range(nb)]
        b_end = jnp.concatenate([jnp.broadcast_to(be, (L, GLA_QK)) for be in b_last], axis=0)
        return dict(v=z[:, 2 * GLA_QK:2 * GLA_QK + WIDTH],
                    g=z[:, 2 * GLA_QK + WIDTH:2 * GLA_QK + 2 * WIDTH], b_last=b_last,
                    qt=q * jnp.exp(b), kt=(k * jnp.exp(-b)).astype(BF16),
                    kd=(k * jnp.exp(b_end - b)).astype(BF16))

    pro = [prologue(gi) for gi in groups]
    units = [(gi, h) for gi in groups for h in heads]
    attn = [jnp.where(tri, _mm_nt(jnp.where(qk_lane == GLA_DK * h, pro[gi]["qt"], 0.0),
                                  pro[gi]["kt"]), 0.0).astype(BF16) for gi, h in units]
    o_h = [jnp.dot(attn[u], jnp.where(v_lane == GLA_DV * h, pro[gi]["v"], 0.0).astype(BF16),
                   preferred_element_type=F32) for u, (gi, h) in enumerate(units)]
    for gi in groups:
        qt, kd, v = pro[gi]["qt"].astype(BF16), pro[gi]["kd"], pro[gi]["v"].astype(BF16)
        from_s = []
        for i in range(nb):
            seq = gi * nb + i
            rows = slice(L * i, L * (i + 1))
            st = s_scr[seq]
            from_s.append(_mm_nt(qt[rows], st))
            update = lax.dot_general(v[rows], kd[rows], (((0,), (0,)), ((), ())),
                                     preferred_element_type=F32)
            s_scr[seq] = st * jnp.exp(pro[gi]["b_last"][i]) + jnp.where(head_block, update, 0.0)
        o = jnp.concatenate(from_s, axis=0)
        for h in heads:
            o = o + o_h[gi * GLA_HEADS + h]
        ms = _mm_split_lhs(o * o, ones_ref[...]) * (1.0 / GLA_DV)
        g = pro[gi]["g"]
        o_ref[gi * nb:(gi + 1) * nb] = (o * lax.rsqrt(ms + RMS_EPS) * gnorm_ref[...]
                                        * (g * _sigmoid(g))).reshape(nb, L, WIDTH)

    @pl.when(pl.program_id(1) == pl.num_programs(1) - 1)
    def _():
        sfin_ref[...] = s_scr[...]


def _gla(cols, s0, p, ng, nb, L):
    bt, t, _ = cols.shape
    ns = ng * nb
    assert bt % ns == 0 and t % L == 0 and L & (L - 1) == 0
    eye = jnp.eye(GLA_HEADS, dtype=F32)
    s0_blocks = jnp.einsum("bhcv,hg->bhvgc", s0, eye).reshape(bt, WIDTH, GLA_QK)
    kern = functools.partial(_gla_kernel, ng=ng, nb=nb, L=L)
    state = pl.BlockSpec((ns, WIDTH, GLA_QK), lambda b, n: (b, 0, 0))
    o, s_blocks = pl.pallas_call(
        kern,
        grid=(bt // ns, t // L),
        in_specs=[pl.BlockSpec((ns, L, GLA_COLS_PAD), lambda b, n: (b, n, 0)), state,
                  _const_spec((128, GLA_QK)), _const_spec((1, GLA_QK)), _const_spec((1, WIDTH)),
                  _const_spec((WIDTH, WIDTH))],
        out_specs=(pl.BlockSpec((ns, L, WIDTH), lambda b, n: (b, n, 0)), state),
        out_shape=(jax.ShapeDtypeStruct((bt, t, WIDTH), F32),
                   jax.ShapeDtypeStruct((bt, WIDTH, GLA_QK), F32)),
        scratch_shapes=[pltpu.VMEM((ns, WIDTH, GLA_QK), F32)],
        compiler_params=_params("parallel", "arbitrary"),
        name="gla",
    )(cols, s0_blocks, p["gla_w_alpha"], p["gla_b_alpha"], p["gla_norm"], p["head_ones"])
    s5d = s_blocks.reshape(bt, GLA_HEADS, GLA_DV, GLA_HEADS, GLA_DK)
    s_new = jnp.stack([s5d[:, h, :, h, :] for h in range(GLA_HEADS)], axis=1)
    return o, jnp.swapaxes(s_new, 2, 3)


def _rwkv_kernel(z_ref, shift0_ref, s0_ref, mu_ref, w0_ref, w2_ref, a0_ref, a2_ref, g2_ref, kk_ref,
                 ka_ref, rk_ref, lnw_ref, lnb_ref, bd_ref, o_ref, shiftfin_ref, sfin_ref,
                 s_scr, prev_scr, *, ng, nb, L):
    R = nb * L
    HD = RWKV_HEAD
    groups = range(ng)
    heads = range(RWKV_HEADS)
    hsl = [slice(HD * h, HD * (h + 1)) for h in heads]

    @pl.when(pl.program_id(1) == 0)
    def _():
        for gi in groups:
            for i in range(nb):
                for h in heads:
                    s_scr[gi, h, :, HD * i:HD * (i + 1)] = s0_ref[gi * nb + i, h]
        prev_scr[...] = shift0_ref[...]

    rr = lax.broadcasted_iota(jnp.int32, (R, R), 0)
    cc = lax.broadcasted_iota(jnp.int32, (R, R), 1)
    same = (rr & -L) == (cc & -L)
    tri = jnp.logical_and(same, rr >= cc)
    stri = jnp.logical_and(same, rr > cc)
    tri_ones = tri.astype(BF16)
    own = ((lax.broadcasted_iota(jnp.int32, (R, nb * HD), 0) & -L) * HD
           == (lax.broadcasted_iota(jnp.int32, (R, nb * HD), 1) & -HD) * L)
    first = (lax.broadcasted_iota(jnp.int32, (R, RWKV_COLS), 0) & (L - 1)) == 0

    def spread(x):
        return jnp.where(own, jnp.concatenate([x] * nb, axis=1), 0.0).astype(BF16)

    pro = [_rwkv_prologue(gi, z_ref, prev_scr, shiftfin_ref, first, tri_ones, mu_ref, w0_ref, w2_ref,
                          a0_ref, a2_ref, g2_ref, kk_ref, ka_ref, bd_ref, nb, L) for gi in groups]

    units = [(gi, h) for gi in groups for h in heads]
    f32dot = functools.partial(jnp.dot, preferred_element_type=F32)
    vb = [pro[gi]["v"][:, hsl[h]].astype(BF16) for gi, h in units]
    gram = [_mm_nt(jnp.concatenate([pro[gi]["a_til"][:, hsl[h]], pro[gi]["r_til"][:, hsl[h]]], axis=0),
                   jnp.concatenate([pro[gi]["b_til"][:, hsl[h]], pro[gi]["k_til"][:, hsl[h]]], axis=0))
            for gi, h in units]
    from_s = [_mm_nt(jnp.concatenate([spread(pro[gi]["a_til"][:, hsl[h]]),
                                      spread(pro[gi]["r_til"][:, hsl[h]])], axis=0), s_scr[gi, h])
              for gi, h in units]
    pw = [jnp.where(stri, gm[:R, :R], 0.0).astype(BF16) for gm in gram]
    a_ak = [jnp.where(stri, gm[:R, R:], 0.0).astype(BF16) for gm in gram]
    a_r = [jnp.concatenate([jnp.where(tri, gm[R:, :R], 0.0).astype(BF16),
                            jnp.where(tri, gm[R:, R:], 0.0).astype(BF16)], axis=1) for gm in gram]
    n_units = range(len(units))
    sa = [from_s[u][:R] + f32dot(a_ak[u], vb[u]) for u in n_units]
    sa = [sa[u] + f32dot(pw[u], sa[u].astype(BF16)) for u in n_units]
    for _ in range(int(math.log2(L)) - 1):
        pw = [f32dot(pw[u], pw[u]).astype(BF16) for u in n_units]
        sa = [sa[u] + f32dot(pw[u], sa[u].astype(BF16)) for u in n_units]
    sa_v = [jnp.concatenate([sa[u].astype(BF16), vb[u]], axis=0) for u in n_units]
    y = [from_s[u][R:] + f32dot(a_r[u], sa_v[u]) for u in n_units]
    for gi in groups:
        outs = []
        for h in heads:
            yh = y[gi * RWKV_HEADS + h]
            hs = hsl[h]
            mean = jnp.mean(yh, axis=-1, keepdims=True)
            var = jnp.mean(jnp.square(yh - mean), axis=-1, keepdims=True)
            yn = (yh - mean) * lax.rsqrt(var + RWKV_GN_EPS) * lnw_ref[:, hs] + lnb_ref[:, hs]
            bonus = jnp.sum(pro[gi]["r"][:, hs] * pro[gi]["k"][:, hs] * rk_ref[:, hs], axis=-1,
                            keepdims=True)
            outs.append(yn + bonus * pro[gi]["v"][:, hs])
        o_ref[gi * nb:(gi + 1) * nb] = (jnp.concatenate(outs, axis=-1)
                                        * pro[gi]["g"]).reshape(nb, L, WIDTH)
    for u, (gi, h) in enumerate(units):
        p_all = jnp.concatenate([pe[:, hsl[h]] for pe in pro[gi]["p_last"]], axis=1)
        s_scr[gi, h] = s_scr[gi, h] * p_all + _mm_tn(
            sa_v[u], jnp.concatenate([spread(pro[gi]["b_end"][:, hsl[h]]),
                                      spread(pro[gi]["k_end"][:, hsl[h]])], axis=0))

    @pl.when(pl.program_id(1) == pl.num_programs(1) - 1)
    def _():
        for gi in groups:
            for i in range(nb):
                for h in heads:
                    sfin_ref[gi * nb + i, h] = s_scr[gi, h, :, HD * i:HD * (i + 1)]


def _rwkv_prologue(gi, z_ref, prev_scr, shiftfin_ref, first, tri_ones, mu_ref, w0_ref, w2_ref, a0_ref,
                   a2_ref, g2_ref, kk_ref, ka_ref, bd_ref, nb, L):
    R = nb * L
    seqs = range(gi * nb, (gi + 1) * nb)
    z = z_ref[gi * nb:(gi + 1) * nb].reshape(R, RWKV_COLS)
    carried = jnp.concatenate(
        [jnp.broadcast_to(prev_scr[s], (L, RWKV_COLS)) for s in seqs], axis=0)
    prev = jnp.where(first, carried, pltpu.roll(z, 1, axis=0))
    for i, s in enumerate(seqs):
        last = z[L * (i + 1) - 1:L * (i + 1), :]
        prev_scr[s] = last
        shiftfin_ref[s] = last
    zs = z + (prev - z) * mu_ref[...]
    r = zs[:, 0:256]
    k = zs[:, 256:512]
    v = zs[:, 512:768]
    w_lat = zs[:, 768:832]
    a_lat = zs[:, 832:896]
    g_lat = zs[:, 896:1024]
    w = -jax.nn.softplus(-(w0_ref[...] + _mm(jnp.tanh(w_lat), w2_ref[...]))) - 0.5
    log_w = -jnp.exp(w)
    a = _sigmoid(a0_ref[...] + _mm(a_lat, a2_ref[...]))
    g = _mm(_sigmoid(g_lat), g2_ref[...])
    kk = k * kk_ref[...]
    kk = kk / jnp.maximum(jnp.sqrt(_mm_split_lhs(kk * kk, bd_ref[...])), 1e-12)
    k = k * (1.0 + (a - 1.0) * ka_ref[...])
    cum = _mm_split_rhs(tri_ones, log_w)
    p_t = jnp.exp(cum)
    inv_p = jnp.exp(-cum)
    p_last = [p_t[L * (i + 1) - 1:L * (i + 1), :] for i in range(nb)]
    p_end = jnp.concatenate([jnp.broadcast_to(pe, (L, WIDTH)) for pe in p_last], axis=0)
    b_til = kk * a * inv_p
    k_til = k * inv_p
    return dict(r=r, k=k, v=v, g=g, p_last=p_last,
                a_til=-kk * jnp.exp(cum - log_w),
                b_til=b_til, k_til=k_til, r_til=r * p_t,
                b_end=b_til * p_end, k_end=k_til * p_end)


def _rwkv(cols, shift0, s0, p, ng, nb, L):
    bt, t, _ = cols.shape
    ns = ng * nb
    assert bt % ns == 0 and t % L == 0 and L & (L - 1) == 0
    kern = functools.partial(_rwkv_kernel, ng=ng, nb=nb, L=L)
    state = pl.BlockSpec((ns, RWKV_HEADS, RWKV_HEAD, RWKV_HEAD), lambda b, n: (b, 0, 0, 0))
    shift = pl.BlockSpec((ns, 1, RWKV_COLS), lambda b, n: (b, 0, 0))
    vec = _const_spec((1, WIDTH))
    return pl.pallas_call(
        kern,
        grid=(bt // ns, t // L),
        in_specs=[pl.BlockSpec((ns, L, RWKV_COLS), lambda b, n: (b, n, 0)), shift, state,
                  _const_spec((1, RWKV_COLS)), vec, _const_spec((64, WIDTH)), vec,
                  _const_spec((64, WIDTH)), _const_spec((128, WIDTH)), vec, vec, vec, vec, vec,
                  _const_spec((WIDTH, WIDTH))],
        out_specs=(pl.BlockSpec((ns, L, WIDTH), lambda b, n: (b, n, 0)), shift, state),
        out_shape=(jax.ShapeDtypeStruct((bt, t, WIDTH), F32),
                   jax.ShapeDtypeStruct((bt, 1, RWKV_COLS), F32),
                   jax.ShapeDtypeStruct((bt, RWKV_HEADS, RWKV_HEAD, RWKV_HEAD), F32)),
        scratch_shapes=[pltpu.VMEM((ng, RWKV_HEADS, RWKV_HEAD, nb * RWKV_HEAD), F32),
                        pltpu.VMEM((ns, 1, RWKV_COLS), F32)],
        compiler_params=_params("parallel", "arbitrary"),
        name="rwkv",
    )(cols, shift0, s0, p["rwkv_mu"], p["rwkv_w0"], p["rwkv_w2"], p["rwkv_a0"], p["rwkv_a2"],
      p["rwkv_g2"], p["rwkv_k_k"], p["rwkv_k_a"], p["rwkv_r_k"], p["rwkv_ln_w"], p["rwkv_ln_b"],
      p["head_ones"])


SB_DEAD_LOG = -110.0
SB_LANES = 128


def _sb_block(qh, k, v, diag, suffix_ones, acc_scr, carry_scr):
    tq = qh[0].shape[0]
    tk = k.shape[1]
    kb = k.astype(BF16)
    z = jnp.concatenate(
        [jnp.dot(qh[h], kb[SB_HEAD * h:SB_HEAD * (h + 1)], preferred_element_type=F32)
         for h in range(SB_HEADS)], axis=0)
    nlm = jnp.maximum(z, 0.0) + jnp.log(1.0 + jnp.exp(-jnp.abs(z)))
    if diag:
        row = lax.broadcasted_iota(jnp.int32, z.shape, 0) & (tq - 1)
        mask = row > lax.broadcasted_iota(jnp.int32, z.shape, 1)
        nlm = jnp.where(mask, nlm, 0.0)
    sums = _mm_split_lhs(nlm, suffix_ones)
    carry = carry_scr[...]
    if tk <= SB_LANES:
        carry_keys = carry[:, :tk]
    else:
        carry_keys = jnp.concatenate([carry] * (tk // SB_LANES), axis=1)
    wgt = jnp.exp(z - carry_keys - sums)
    if diag:
        wgt = jnp.where(mask, wgt, 0.0)
    wgt = wgt.astype(BF16)
    vb = v.astype(BF16)
    for h in range(SB_HEADS):
        acc_scr[h] += lax.dot_general(wgt[tq * h:tq * (h + 1)], vb[SB_HEAD * h:SB_HEAD * (h + 1)],
                                      (((1,), (1,)), ((), ())), preferred_element_type=F32)
    carry = carry + jnp.broadcast_to(sums[:, 0:1], carry.shape)
    carry_scr[...] = carry
    return -jnp.min(carry)


def _suffix_ones(tk):
    r = lax.broadcasted_iota(jnp.int32, (tk, tk), 0)
    c = lax.broadcasted_iota(jnp.int32, (tk, tk), 1)
    return (r >= c).astype(BF16)


def _sb_heads(q):
    return [q[:, SB_HEAD * h:SB_HEAD * (h + 1)].astype(BF16) for h in range(SB_HEADS)]


def _sb_self_kernel(q_ref, k_ref, v_ref, o_ref, acc_scr, carry_scr, *, tq):
    i = pl.program_id(1)
    acc_scr[...] = jnp.zeros_like(acc_scr)
    carry_scr[...] = jnp.zeros_like(carry_scr)
    qh = _sb_heads(q_ref[...])
    ones = _suffix_ones(tq)

    def keys(j):
        return pl.ds(pl.multiple_of(j * tq, tq), tq)

    live = _sb_block(qh, k_ref[0, 0, :, keys(i)], v_ref[0, 0, :, keys(i)], True, ones,
                     acc_scr, carry_scr)

    def body(state):
        j, _ = state
        m = _sb_block(qh, k_ref[0, 0, :, keys(j)], v_ref[0, 0, :, keys(j)], False, ones,
                      acc_scr, carry_scr)
        return j - 1, m

    lax.while_loop(lambda s: jnp.logical_and(s[0] >= 0, s[1] > SB_DEAD_LOG), body, (i - 1, live))
    o_ref[...] = jnp.concatenate([acc_scr[h] for h in range(SB_HEADS)], axis=-1)


def _sb_self(q2d, k_t, v_t, layer, tq):
    _, bt, _, t = k_t.shape
    nq = t // tq
    assert tq & (tq - 1) == 0
    kern = functools.partial(_sb_self_kernel, tq=tq)
    seq = pl.BlockSpec((1, 1, WIDTH, t), lambda b, i: (layer, b, 0, 0))
    blk = pl.BlockSpec((tq, WIDTH), lambda b, i: (b * nq + i, 0))
    return pl.pallas_call(
        kern,
        grid=(bt, nq),
        in_specs=[blk, seq, seq],
        out_specs=blk,
        out_shape=jax.ShapeDtypeStruct((bt * t, WIDTH), F32),
        scratch_shapes=[pltpu.VMEM((SB_HEADS, tq, SB_HEAD), F32),
                        pltpu.VMEM((SB_HEADS * tq, SB_LANES), F32)],
        compiler_params=_params("parallel", "arbitrary"),
        name="sb_self",
    )(q2d, k_t, v_t)


SB_OLDER_SLOT = 2


def _sb_past_kernel(q_ref, k_ref, v_ref, pk_hbm, pv_hbm, o_ref, kbuf, vbuf, sem, acc_scr, carry_scr,
                    *, layer, t, tkp, np_):
    b = pl.program_id(0)
    slot = b % 2
    qh = _sb_heads(q_ref[...])

    def fetch(seq, blk, dst):
        keys = pl.ds(pl.multiple_of(blk * tkp, tkp), tkp)
        return (pltpu.make_async_copy(pk_hbm.at[layer, seq, :, keys], kbuf.at[dst], sem.at[0, dst]),
                pltpu.make_async_copy(pv_hbm.at[layer, seq, :, keys], vbuf.at[dst], sem.at[1, dst]))

    @pl.when(b == 0)
    def _():
        for cp in fetch(0, np_ - 1, 0):
            cp.start()

    @pl.when(b + 1 < pl.num_programs(0))
    def _():
        for cp in fetch(b + 1, np_ - 1, 1 - slot):
            cp.start()

    acc_scr[...] = jnp.zeros_like(acc_scr)
    carry_scr[...] = jnp.zeros_like(carry_scr)
    _sb_block(qh, k_ref[0, 0], v_ref[0, 0], True, _suffix_ones(t), acc_scr, carry_scr)

    ones = _suffix_ones(tkp)
    for cp in fetch(b, np_ - 1, slot):
        cp.wait()
    live = _sb_block(qh, kbuf[slot], vbuf[slot], False, ones, acc_scr, carry_scr)

    def body(state):
        j, _ = state
        copies = fetch(b, j, SB_OLDER_SLOT)
        for cp in copies:
            cp.start()
        for cp in copies:
            cp.wait()
        m = _sb_block(qh, kbuf[SB_OLDER_SLOT], vbuf[SB_OLDER_SLOT], False, ones, acc_scr, carry_scr)
        return j - 1, m

    lax.while_loop(lambda s: jnp.logical_and(s[0] >= 0, s[1] > SB_DEAD_LOG), body, (np_ - 2, live))
    o_ref[...] = jnp.concatenate([acc_scr[h] for h in range(SB_HEADS)], axis=-1)


def _sb_past(q2d, k_t, v_t, past_k_t, past_v_t, layer, tkp):
    _, bt, _, t = k_t.shape
    np_ = past_k_t.shape[3] // tkp
    assert t & (t - 1) == 0 and past_k_t.shape[3] % tkp == 0
    kern = functools.partial(_sb_past_kernel, layer=layer, t=t, tkp=tkp, np_=np_)
    qblk = pl.BlockSpec((t, WIDTH), lambda b: (b, 0))
    new = pl.BlockSpec((1, 1, WIDTH, t), lambda b: (layer, b, 0, 0))
    hbm = pl.BlockSpec(memory_space=pl.ANY)
    return pl.pallas_call(
        kern,
        grid=(bt,),
        in_specs=[qblk, new, new, hbm, hbm],
        out_specs=qblk,
        out_shape=jax.ShapeDtypeStruct((bt * t, WIDTH), F32),
        scratch_shapes=[pltpu.VMEM((3, WIDTH, tkp), F32), pltpu.VMEM((3, WIDTH, tkp), F32),
                        pltpu.SemaphoreType.DMA((2, 3)),
                        pltpu.VMEM((SB_HEADS, t, SB_HEAD), F32),
                        pltpu.VMEM((SB_HEADS * t, SB_LANES), F32)],
        compiler_params=_params("arbitrary"),
        name="sb_past",
    )(q2d, k_t, v_t, past_k_t, past_v_t)


def _merge_ffn_kernel(x_ref, oa_ref, ob_ref, oc_ref, od_ref, gate_ref, wb_ref, wout_ref, nffn_ref,
                      wg_ref, wu_ref, wd_ref, y_ref):
    merged = None
    for n, o_ref in enumerate((oa_ref, ob_ref, oc_ref, od_ref)):
        term = gate_ref[:, n * D_MODEL:(n + 1) * D_MODEL].astype(F32) * _mm(o_ref[...], wb_ref[n])
        merged = term if merged is None else merged + term
    x = x_ref[...] + _mm(merged, wout_ref[...])
    h = x * lax.rsqrt(jnp.mean(x * x, axis=-1, keepdims=True) + RMS_EPS) * nffn_ref[...]
    hb = h.astype(BF16)
    gate = jnp.dot(hb, wg_ref[...], preferred_element_type=F32)
    up = jnp.dot(hb, wu_ref[...], preferred_element_type=F32)
    y_ref[...] = x + _mm(gate * _sigmoid(gate) * up, wd_ref[...])


def _merge_ffn(x2d, oa, ob, oc, od, gates, p, tm):
    n = x2d.shape[0]
    row = lambda width: pl.BlockSpec((tm, width), lambda i: (i, 0))
    return pl.pallas_call(
        _merge_ffn_kernel,
        grid=(n // tm,),
        in_specs=[row(D_MODEL), row(WIDTH), row(WIDTH), row(WIDTH), row(WIDTH), row(GATE_COLS),
                  _const_spec((N_BRANCH, WIDTH, D_MODEL)), _const_spec((D_MODEL, D_MODEL)),
                  _const_spec((1, D_MODEL)), _const_spec((D_MODEL, D_FF)),
                  _const_spec((D_MODEL, D_FF)), _const_spec((D_FF, D_MODEL))],
        out_specs=row(D_MODEL),
        out_shape=jax.ShapeDtypeStruct((n, D_MODEL), F32),
        compiler_params=_params("parallel"),
        name="merge_ffn",
    )(x2d, oa, ob, oc, od, gates, p["w_branch"], p["w_out"], p["norm_ffn"], p["w_ffn_gate"],
      p["w_ffn_up"], p["w_ffn_down"])


def _prep_layer(w):
    p = {}
    w_in = w["w_in"]
    o_gla = WIDTH
    o_alat = o_gla + 2 * GLA_QK + 2 * WIDTH
    o_rwkv = o_alat + GLA_RANK
    o_sb = o_rwkv + RWKV_COLS
    o_sbk = o_sb + WIDTH
    o_gate = o_sb + SB_COLS
    pad = jnp.zeros((D_MODEL, 128 - GLA_RANK), w_in.dtype)
    p["w_in"] = jnp.concatenate(
        [w_in[:, :o_rwkv], pad, w_in[:, o_rwkv:o_sbk], w_in[:, o_gate:]], axis=1).astype(BF16)
    p["w_kv_t"] = w_in[:, o_sbk:o_gate].astype(BF16).T
    p["norm_mix"] = w["norm_mix"].reshape(1, D_MODEL)

    lam = lax.complex(w["s5_a_re"], w["s5_a_im"])
    dt = jnp.exp(w["s5_log_dt"])[:, None]
    a_bar = jnp.exp(lam * dt)
    b_bar = ((a_bar - 1.0) / lam)[..., None] * lax.complex(w["s5_b_re"], w["s5_b_im"])
    eye = jnp.eye(S5_GROUPS, dtype=F32)

    def in_map(m):
        return jnp.einsum("gpc,gh->gchp", m, eye).reshape(WIDTH, S5_FLAT)

    def out_map(m):
        return jnp.einsum("gcp,gh->gphc", m, eye).reshape(S5_FLAT, WIDTH)

    p["s5_bbig"] = jnp.concatenate([in_map(b_bar.real), in_map(b_bar.imag)], axis=1).astype(BF16)
    p["s5_cbig"] = jnp.concatenate([out_map(w["s5_c_re"]), out_map(-w["s5_c_im"])], axis=0).astype(BF16)
    p["s5_ar"] = a_bar.real.reshape(1, S5_FLAT)
    p["s5_ai"] = a_bar.imag.reshape(1, S5_FLAT)
    p["s5_d"] = w["s5_d"].reshape(1, WIDTH)
    p["s5_w_glu"] = w["s5_w_glu"].astype(BF16)
    p["s5_b_glu"] = w["s5_b_glu"].reshape(1, WIDTH)

    p["gla_w_alpha"] = jnp.concatenate(
        [w["gla_w_alpha"], jnp.zeros((128 - GLA_RANK, GLA_QK), F32)], axis=0).astype(BF16)
    p["gla_b_alpha"] = w["gla_b_alpha"].reshape(1, GLA_QK)
    p["gla_norm"] = w["gla_norm"].reshape(1, WIDTH)

    for name in ("rwkv_w0", "rwkv_a0", "rwkv_k_k", "rwkv_k_a", "rwkv_r_k", "rwkv_ln_w", "rwkv_ln_b"):
        p[name] = w[name].reshape(1, WIDTH)
    p["rwkv_mu"] = w["rwkv_mu"].reshape(1, RWKV_COLS)
    for name in ("rwkv_w2", "rwkv_a2", "rwkv_g2"):
        p[name] = w[name].astype(BF16)
    head = jnp.arange(WIDTH) // RWKV_HEAD
    p["head_ones"] = (head[:, None] == head[None, :]).astype(BF16)

    p["sb_q_norm"] = jnp.tile(w["sb_q_norm"] * (SB_HEAD ** -0.5), SB_HEADS).reshape(1, WIDTH)
    p["sb_k_norm"] = jnp.tile(w["sb_k_norm"], SB_HEADS).reshape(WIDTH, 1)

    p["w_branch"] = w["w_branch"].astype(BF16)
    p["w_out"] = w["w_out"].astype(BF16)
    p["norm_ffn"] = w["norm_ffn"].reshape(1, D_MODEL)
    for name in ("w_ffn_gate", "w_ffn_up", "w_ffn_down"):
        p[name] = w[name].astype(BF16)
    return p


def _pick(n, prefs):
    for c in prefs:
        if n % c == 0:
            return c
    return n


def _layer(x, past_k_t, past_v_t, layer, depth, kv_prev, s5_0, gla_0, rwkv_0, shift_0, p):
    bt, t, _ = x.shape
    n = bt * t
    x2d = x.reshape(n, D_MODEL)
    tm = _pick(n, (256, 128, 64, 32, 16, 8))
    nt = t // tm if t % tm == 0 else None
    assert nt is not None or tm % t == 0
    tm_in = INPROJ_ROWS if nt is not None and t % INPROJ_ROWS == 0 else tm
    u, c_gla, c_rwkv, q, k_t, v_t, gates = _inproj(x2d, p, bt, t, tm_in,
                                                   None if nt is None else t // tm_in,
                                                   layer, depth, kv_prev)

    x0 = jnp.concatenate([s5_0[..., 0].reshape(bt, S5_FLAT), s5_0[..., 1].reshape(bt, S5_FLAT)], axis=1)
    o_a, xfin = _s5(u.reshape(bt, t, WIDTH), x0, p, _pick(t, (128, 64, 32, 16, 8)))
    o_a = o_a.reshape(n, WIDTH)
    s5_new = jnp.stack([xfin[:, :S5_FLAT].reshape(bt, S5_GROUPS, S5_STATE),
                        xfin[:, S5_FLAT:].reshape(bt, S5_GROUPS, S5_STATE)], axis=-1)

    chunk = _pick(t, (64, 32, 16, 8))
    nseq = _pick(bt, (RWKV_ROWS // chunk, 4, 2, 1))
    ngrp = _pick(bt // nseq, (RWKV_GROUPS, 1))
    o_b, gla_new = _gla(c_gla.reshape(bt, t, GLA_COLS_PAD), gla_0, p,
                        _pick(bt // nseq, (GLA_GROUPS, 2, 1)), nseq, chunk)

    o_c, shift_new, rwkv_new = _rwkv(c_rwkv.reshape(bt, t, RWKV_COLS),
                                     shift_0.reshape(bt, 1, RWKV_COLS), rwkv_0, p, ngrp, nseq, chunk)

    if past_k_t is None:
        o_d = _sb_self(q, k_t, v_t, layer, _pick(t, (256, 128, 64, 32, 16, 8)))
    else:
        tkp = _pick(past_k_t.shape[3], (256, 128))
        o_d = _sb_past(q, k_t, v_t, past_k_t, past_v_t, layer, tkp)

    y = _merge_ffn(x2d, o_a, o_b.reshape(n, WIDTH), o_c.reshape(n, WIDTH), o_d, gates, p, tm_in)
    states = (s5_new, gla_new, rwkv_new, shift_new.reshape(bt, RWKV_COLS))
    return y.reshape(bt, t, D_MODEL), (k_t, v_t), states


def _keys_on_lanes(cache):
    d, b, pl_, _, _ = cache.shape
    return cache.transpose(0, 1, 3, 4, 2).reshape(d, b, WIDTH, pl_)


def _trunk(x, past_k, past_v, s5_0, gla_0, rwkv_0, shift_0, layers):
    past_k_t = None if past_k is None else _keys_on_lanes(past_k)
    past_v_t = None if past_v is None else _keys_on_lanes(past_v)
    per_layer = []
    kv = None
    for l, p in enumerate(layers):
        x, kv, st = _layer(x, past_k_t, past_v_t, l, len(layers), kv, s5_0[l], gla_0[l], rwkv_0[l],
                           shift_0[l], p)
        per_layer.append(st)

    def cache_rows(a_t):
        d, bt, _, t = a_t.shape
        return a_t.reshape(d, bt, SB_HEADS, SB_HEAD, t).transpose(0, 1, 4, 2, 3)

    return x, [cache_rows(kv[0]), cache_rows(kv[1])] + [jnp.stack([st[i] for st in per_layer])
                                                         for i in range(4)]


def kernel(x_prompt, x_sample, cache_sb_k, cache_sb_v, state_s5, state_gla, state_rwkv, state_rwkv_shift, norm_mix, w_in, s5_a_re, s5_a_im, s5_log_dt, s5_b_re, s5_b_im, s5_c_re, s5_c_im, s5_d, s5_w_glu, s5_b_glu, gla_w_alpha, gla_b_alpha, gla_norm, rwkv_mu, rwkv_w0, rwkv_w2, rwkv_a0, rwkv_a2, rwkv_g2, rwkv_k_k, rwkv_k_a, rwkv_r_k, rwkv_ln_w, rwkv_ln_b, sb_q_norm, sb_k_norm, w_branch, w_out, norm_ffn, w_ffn_gate, w_ffn_up, w_ffn_down):
    weights = dict(norm_mix=norm_mix, w_in=w_in, s5_a_re=s5_a_re, s5_a_im=s5_a_im, s5_log_dt=s5_log_dt,
                   s5_b_re=s5_b_re, s5_b_im=s5_b_im, s5_c_re=s5_c_re, s5_c_im=s5_c_im, s5_d=s5_d,
                   s5_w_glu=s5_w_glu, s5_b_glu=s5_b_glu, gla_w_alpha=gla_w_alpha, gla_b_alpha=gla_b_alpha,
                   gla_norm=gla_norm, rwkv_mu=rwkv_mu, rwkv_w0=rwkv_w0, rwkv_w2=rwkv_w2, rwkv_a0=rwkv_a0,
                   rwkv_a2=rwkv_a2, rwkv_g2=rwkv_g2, rwkv_k_k=rwkv_k_k, rwkv_k_a=rwkv_k_a, rwkv_r_k=rwkv_r_k,
                   rwkv_ln_w=rwkv_ln_w, rwkv_ln_b=rwkv_ln_b, sb_q_norm=sb_q_norm, sb_k_norm=sb_k_norm,
                   w_branch=w_branch, w_out=w_out, norm_ffn=norm_ffn, w_ffn_gate=w_ffn_gate,
                   w_ffn_up=w_ffn_up, w_ffn_down=w_ffn_down)
    depth = w_in.shape[0]
    layers = [_prep_layer({name: arr[l] for name, arr in weights.items()}) for l in range(depth)]

    bp = x_prompt.shape[0]
    y_prompt, p_states = _trunk(
        x_prompt, None, None,
        jnp.zeros((depth, bp, S5_GROUPS, S5_STATE, 2), F32),
        jnp.zeros((depth, bp, GLA_HEADS, GLA_DK, GLA_DV), F32),
        jnp.zeros((depth, bp, RWKV_HEADS, RWKV_HEAD, RWKV_HEAD), F32),
        jnp.zeros((depth, bp, RWKV_COLS), F32), layers)
    y_sample, s_states = _trunk(x_sample, cache_sb_k, cache_sb_v, state_s5, state_gla, state_rwkv,
                                state_rwkv_shift, layers)
    return (y_prompt, y_sample, *p_states, *s_states)
```
